```python
import jax, jax.numpy as jnp
from jax import lax
import numpy as np

D_MODEL = 1024
BATCH = 4
SEQ = 8192
DEPTH = 4

GRID_W = 64
CTX_LEN = 256
N_EVEN = (DEPTH + 1) // 2
N_ODD = DEPTH // 2
RWKV_WIDTH = D_MODEL // 2
HEAD_DIM = 64
RWKV_HEADS = RWKV_WIDTH // HEAD_DIM
R_DECAY = 64
R_LR = 64
R_GATE = 128
GN_EPS = 64e-5
POOL_WIDTH = D_MODEL // 2
POOL_WINDOWS = (2, 4, 8, 16)
POOL_GROUP = POOL_WIDTH // len(POOL_WINDOWS)
IN_WIDTH = 3 * RWKV_WIDTH + POOL_WIDTH
MIX_WIDTH = RWKV_WIDTH + POOL_WIDTH
FOURIER_GROUPS = 4
MOE_GROUPS = 4
EXPERTS_PER_GROUP = 8
N_EXPERTS = MOE_GROUPS * EXPERTS_PER_GROUP
TOP_K = 2
D_EXPERT = 512
MOE_BLOCK = 128
NORM_EPS = 1e-6

kernel_name = "hybrid_rwkv7_pool_fnet_hmoe_dit"


def rms_norm(x, g):
    xf = x.astype(jnp.float32)
    y = xf * lax.rsqrt(jnp.mean(xf * xf, axis=-1, keepdims=True) + NORM_EPS)
    return y.astype(x.dtype) * g


def modulate(x, g, shift, scale):
    return rms_norm(x, g) * (1 + scale) + shift


def grid_shift(x):
    B, T, C = x.shape
    rows = T // GRID_W
    g = x.reshape(B, rows, GRID_W, 4, C // 4)
    zc = jnp.zeros_like(g[:, :, :1, 0])
    zr = jnp.zeros_like(g[:, :1, :, 2])
    left = jnp.concatenate([zc, g[:, :, :-1, 0]], axis=2)
    right = jnp.concatenate([g[:, :, 1:, 1], zc], axis=2)
    up = jnp.concatenate([zr, g[:, :-1, :, 2]], axis=1)
    down = jnp.concatenate([g[:, 1:, :, 3], zr], axis=1)
    return jnp.stack([left, right, up, down], axis=3).reshape(B, T, C)


def seq_shift(x):
    B, T, C = x.shape
    x2 = x.reshape(B, T, 2, C // 2)
    z = jnp.zeros_like(x2[:, :1, 0])
    prev = jnp.concatenate([z, x2[:, :-1, 0]], axis=1)
    nxt = jnp.concatenate([x2[:, 1:, 1], z], axis=1)
    return jnp.stack([prev, nxt], axis=2).reshape(B, T, C)


def to_heads(x):
    return x.reshape(*x.shape[:-1], RWKV_HEADS, HEAD_DIM).astype(jnp.float32)


def wkv_scan(r, w, k, v, a, b, s0, reverse):
    def step(s, inp):
        r_t, w_t, k_t, v_t, a_t, b_t = inp
        sa = jnp.einsum('bhvk,bhk->bhv', s, a_t)
        s = s * w_t[:, :, None, :] + sa[..., None] * b_t[:, :, None, :] + v_t[..., None] * k_t[:, :, None, :]
        return s, jnp.einsum('bhvk,bhk->bhv', s, r_t)
    xs = tuple(jnp.moveaxis(t, 1, 0) for t in (r, w, k, v, a, b))
    s_final, ys = lax.scan(step, s0, xs, reverse=reverse)
    return jnp.moveaxis(ys, 0, 1), s_final


def rwkv_mix(h, p_rkv, shift_fn, init_states, need_out,
             mu_x, mu_p, decay_w0, decay_w1, decay_w2, lr_a0, lr_a1, lr_a2,
             gate_g1, gate_g2, k_k, k_a, r_k, gn_w, gn_b):
    B, T, _ = h.shape
    f32 = jnp.float32
    hx = shift_fn(h) - h
    x_w = h + hx * mu_x[0]
    x_a = h + hx * mu_x[1]
    r, k, v = [p + (shift_fn(p) - p) * mu_p[n] for n, p in enumerate(jnp.split(p_rkv, 3, axis=-1))]
    kk = to_heads(k * k_k)
    kk = kk * lax.rsqrt(jnp.maximum(jnp.sum(kk * kk, axis=-1, keepdims=True), 1e-12))
    r_h, v_h, k_h, ka_h = to_heads(r), to_heads(v), to_heads(k), to_heads(k_a)
    if init_states is None:
        z = jnp.zeros((B, RWKV_HEADS, HEAD_DIM, HEAD_DIM), f32)
        init_states = (z, z)
    ys, k_dirs, states = [], [], []
    for d in range(2):
        z_w = (decay_w0[d] + jnp.tanh(x_w @ decay_w1[d]) @ decay_w2[d]).astype(f32)
        decay = jnp.exp(-jnp.exp(-jax.nn.softplus(-z_w) - 0.5))
        a = to_heads(jax.nn.sigmoid(lr_a0[d] + (x_a @ lr_a1[d]) @ lr_a2[d]))
        k_d = k_h * (1 + (a - 1) * ka_h)
        y_d, s_d = wkv_scan(r_h, to_heads(decay), k_d, v_h, -kk, kk * a, init_states[d], reverse=(d == 1))
        ys.append(y_d)
        k_dirs.append(k_d)
        states.append(s_d)
    if not need_out:
        return None, (states[0], states[1])
    y = ys[0] + ys[1]
    mu = jnp.mean(y, axis=-1, keepdims=True)
    var = jnp.mean(jnp.square(y - mu), axis=-1, keepdims=True)
    yn = ((y - mu) * lax.rsqrt(var + GN_EPS)).reshape(B, T, RWKV_WIDTH) * gn_w + gn_b
    bonus = (jnp.sum(r_h * (k_dirs[0] + k_dirs[1]) * r_k, axis=-1, keepdims=True) * v_h).reshape(B, T, RWKV_WIDTH)
    x_g = h + hx * mu_x[2]
    g = jax.nn.sigmoid(x_g @ gate_g1) @ gate_g2
    return (yn + bonus).astype(h.dtype) * g, (states[0], states[1])


def pool_mix(u, pool_w, pool_scale):
    B, T, _ = u.shape
    ug = u.reshape(B, T, len(POOL_WINDOWS), POOL_GROUP).astype(jnp.float32)
    cs = jnp.concatenate([jnp.zeros_like(ug[:, :1]), jnp.cumsum(ug, axis=1)], axis=1)
    pos = jnp.arange(T)
    pooled = []
    for gi, win in enumerate(POOL_WINDOWS):
        half = win // 2
        hi = jnp.minimum(pos + half, T)
        lo = jnp.maximum(pos - half, 0)
        cg = cs[:, :, gi]
        pooled.append((cg[:, hi] - cg[:, lo]) / (hi - lo).astype(jnp.float32)[None, :, None])
    diff = (jnp.stack(pooled, axis=2) - ug).astype(u.dtype)
    y = jnp.einsum('btgc,gcd->btgd', diff, pool_w).reshape(B, T, POOL_WIDTH)
    return y * pool_scale


def even_mixer(h, shift_fn, init_states, need_out, w_in, w_out, pool_w, pool_scale, rwkv_params):
    proj = h @ w_in if need_out else h @ w_in[:, :3 * RWKV_WIDTH]
    y_a, states = rwkv_mix(h, proj[..., :3 * RWKV_WIDTH], shift_fn, init_states, need_out, *rwkv_params)
    if not need_out:
        return None, states
    y_b = pool_mix(proj[..., 3 * RWKV_WIDTH:], pool_w, pool_scale)
    return jnp.concatenate([y_a, y_b], axis=-1) @ w_out, states


def fourier_mix(h, w_f):
    B, T, D = h.shape
    hg = h.reshape(B, T, FOURIER_GROUPS, D // FOURIER_GROUPS).astype(jnp.float32)
    f = jnp.fft.fftn(hg, axes=(1, 3), norm='ortho').real
    return f.reshape(B, T, D).astype(h.dtype) @ w_f


def hier_moe(rows, router_c, router_c_b, router_f, router_f_b, w1, w3, w2):
    n, d = rows.shape
    f32 = jnp.float32
    p_group = jax.nn.softmax((rows @ router_c).astype(f32) + router_c_b, axis=-1)
    g_val, g_idx = lax.top_k(p_group, 1)
    logits_f = ((rows @ router_f).astype(f32) + router_f_b).reshape(n, MOE_GROUPS, EXPERTS_PER_GROUP)
    sel = jnp.broadcast_to(g_idx[:, :, None], (n, 1, EXPERTS_PER_GROUP))
    p_local = jax.nn.softmax(jnp.take_along_axis(logits_f, sel, axis=1)[:, 0], axis=-1)
    e_val, e_loc = lax.top_k(p_local, TOP_K)
    gate = g_val * e_val / jnp.sum(e_val, axis=-1, keepdims=True)
    expert = g_idx * EXPERTS_PER_GROUP + e_loc
    nk = n * TOP_K
    e_flat = expert.reshape(nk)
    order = jnp.argsort(e_flat)
    e_sorted = e_flat[order]
    tok_sorted = jnp.repeat(jnp.arange(n), TOP_K)[order]
    gate_sorted = gate.reshape(nk)[order]
    counts = jnp.bincount(e_flat, length=N_EXPERTS)
    padded = (counts + MOE_BLOCK - 1) // MOE_BLOCK * MOE_BLOCK
    start = jnp.cumsum(counts) - counts
    pend = jnp.cumsum(padded)
    pstart = pend - padded
    dest = pstart[e_sorted] + (jnp.arange(nk) - start[e_sorted])
    n_blocks = -(-nk // MOE_BLOCK) + N_EXPERTS
    buf = jnp.zeros((n_blocks * MOE_BLOCK, d), rows.dtype).at[dest].set(rows[tok_sorted])
    block_e = jnp.minimum(jnp.searchsorted(pend, jnp.arange(n_blocks) * MOE_BLOCK, side='right'), N_EXPERTS - 1)

    def expert_ffn(args):
        xb, e = args
        return (jax.nn.silu(xb @ w1[e]) * (xb @ w3[e])) @ w2[e]

    y_buf = lax.map(expert_ffn, (buf.reshape(n_blocks, MOE_BLOCK, d), block_e)).reshape(n_blocks * MOE_BLOCK, d)
    y = y_buf[dest] * gate_sorted[:, None].astype(rows.dtype)
    return jnp.zeros((n, d), rows.dtype).at[tok_sorted].add(y)


def setup_inputs(seed: int = 0) -> dict:
    key = jax.random.key(seed)
    ks = iter(jax.random.split(key, 64))
    f32 = jnp.float32
    D, DA = D_MODEL, RWKV_WIDTH

    def nrm(shape, scale):
        return jax.random.normal(next(ks), shape, f32) * scale

    def unif(shape):
        return jax.random.uniform(next(ks), shape, f32)

    decay_base = -6.0 + 5.0 * jnp.arange(DA, dtype=f32) / (DA - 1)
    return {
        "x": nrm((BATCH, SEQ, D), 1.0),
        "c": nrm((BATCH, D), 1.0),
        "ctx": nrm((BATCH, CTX_LEN, D), 1.0),
        "c_ctx": nrm((D,), 1.0),
        "ada_w": nrm((DEPTH, D, 6 * D), 0.5 * D ** -0.5),
        "ada_b": nrm((DEPTH, 6 * D), 0.1),
        "norm_mix": 1.0 + nrm((DEPTH, D), 0.05),
        "norm_ffn": 1.0 + nrm((DEPTH, D), 0.05),
        "w_in": nrm((N_EVEN, D, IN_WIDTH), D ** -0.5),
        "mu_x": unif((N_EVEN, 3, D)),
        "mu_p": unif((N_EVEN, 3, DA)),
        "decay_w0": decay_base + nrm((N_EVEN, 2, DA), 0.1),
        "decay_w1": nrm((N_EVEN, 2, D, R_DECAY), D ** -0.5),
        "decay_w2": nrm((N_EVEN, 2, R_DECAY, DA), 0.5 * R_DECAY ** -0.5),
        "lr_a0": nrm((N_EVEN, 2, DA), 0.1),
        "lr_a1": nrm((N_EVEN, 2, D, R_LR), D ** -0.5),
        "lr_a2": nrm((N_EVEN, 2, R_LR, DA), 0.5 * R_LR ** -0.5),
        "gate_g1": nrm((N_EVEN, D, R_GATE), D ** -0.5),
        "gate_g2": nrm((N_EVEN, R_GATE, DA), R_GATE ** -0.5),
        "k_k": 0.85 + nrm((N_EVEN, DA), 0.05),
        "k_a": 1.0 + nrm((N_EVEN, DA), 0.05),
        "r_k": nrm((N_EVEN, RWKV_HEADS, HEAD_DIM), 0.1),
        "gn_w": 1.0 + nrm((N_EVEN, DA), 0.05),
        "gn_b": nrm((N_EVEN, DA), 0.02),
        "pool_w": nrm((N_EVEN, len(POOL_WINDOWS), POOL_GROUP, POOL_GROUP), POOL_GROUP ** -0.5),
        "pool_scale": 1.0 + nrm((N_EVEN, POOL_WIDTH), 0.1),
        "w_out": nrm((N_EVEN, MIX_WIDTH, D), MIX_WIDTH ** -0.5),
        "w_fourier": nrm((N_ODD, D, D), D ** -0.5),
        "router_c": nrm((DEPTH, D, MOE_GROUPS), D ** -0.5),
        "router_c_b": nrm((DEPTH, MOE_GROUPS), 0.01),
        "router_f": nrm((DEPTH, D, N_EXPERTS), D ** -0.5),
        "router_f_b": nrm((DEPTH, N_EXPERTS), 0.01),
        "moe_w1": nrm((DEPTH, N_EXPERTS, D, D_EXPERT), D ** -0.5),
        "moe_w3": nrm((DEPTH, N_EXPERTS, D, D_EXPERT), D ** -0.5),
        "moe_w2": nrm((DEPTH, N_EXPERTS, D_EXPERT, D), D_EXPERT ** -0.5),
        "final_norm": 1.0 + nrm((D,), 0.05),
    }


def reference(x, c, ctx, c_ctx, ada_w, ada_b, norm_mix, norm_ffn, w_in, mu_x, mu_p,
              decay_w0, decay_w1, decay_w2, lr_a0, lr_a1, lr_a2, gate_g1, gate_g2,
              k_k, k_a, r_k, gn_w, gn_b, pool_w, pool_scale, w_out, w_fourier,
              router_c, router_c_b, router_f, router_f_b, moe_w1, moe_w3, moe_w2, final_norm):
    B = x.shape[0]
    D = D_MODEL
    s_lat = jax.nn.silu(c)
    s_ctx = jax.nn.silu(c_ctx)
    last_read = 2 * ((DEPTH - 1) // 2)
    lat, cx = x, ctx
    for i in range(DEPTH):
        ctx_in = i <= last_read
        ctx_out = i < last_read
        m = (s_lat @ ada_w[i] + ada_b[i]).reshape(B, 6, 1, D)
        if ctx_in:
            mc = (s_ctx @ ada_w[i] + ada_b[i]).reshape(6, 1, 1, D)
        h = modulate(lat, norm_mix[i], m[:, 0], m[:, 1])
        j = i // 2
        if i % 2 == 0:
            rw = (mu_x[j], mu_p[j], decay_w0[j], decay_w1[j], decay_w2[j], lr_a0[j], lr_a1[j], lr_a2[j],
                  gate_g1[j], gate_g2[j], k_k[j], k_a[j], r_k[j], gn_w[j], gn_b[j])
            hc = modulate(cx, norm_mix[i], mc[0], mc[1])
            yc, ctx_states = even_mixer(hc, seq_shift, None, ctx_out, w_in[j], w_out[j], pool_w[j], pool_scale[j], rw)
            yl, _ = even_mixer(h, grid_shift, ctx_states, True, w_in[j], w_out[j], pool_w[j], pool_scale[j], rw)
            lat = lat + m[:, 2] * yl
            if ctx_out:
                cx = cx + mc[2] * yc
        else:
            lat = lat + m[:, 2] * fourier_mix(h, w_fourier[j])
            if ctx_out:
                hc = modulate(cx, norm_mix[i], mc[0], mc[1])
                cx = cx + mc[2] * fourier_mix(hc, w_fourier[j])
        h2 = modulate(lat, norm_ffn[i], m[:, 3], m[:, 4])
        moe_p = (router_c[i], router_c_b[i], router_f[i], router_f_b[i], moe_w1[i], moe_w3[i], moe_w2[i])
        if ctx_out:
            hc2 = modulate(cx, norm_ffn[i], mc[3], mc[4])
            n_lat = h2.shape[0] * h2.shape[1]
            y = hier_moe(jnp.concatenate([h2.reshape(-1, D), hc2.reshape(-1, D)], axis=0), *moe_p)
            lat = lat + m[:, 5] * y[:n_lat].reshape(lat.shape)
            cx = cx + mc[5] * y[n_lat:].reshape(cx.shape)
        else:
            lat = lat + m[:, 5] * hier_moe(h2.reshape(-1, D), *moe_p).reshape(lat.shape)
    return rms_norm(lat, final_norm)
```

```python
import functools
import math

import jax
import jax.numpy as jnp
import numpy as np
from jax import lax
from jax.experimental import pallas as pl
from jax.experimental.pallas import tpu as pltpu

F32 = jnp.float32
BF16 = jnp.bfloat16

GRID_W = 64
HEAD_DIM = 64
CHUNK = 64
PAIR = 2 * HEAD_DIM
NORM_EPS = 1e-6
GN_EPS = 64e-5
POOL_WINDOWS = (2, 4, 8, 16)
MOE_GROUPS = 4
EXPERTS_PER_GROUP = 8
MOE_BLOCK = 256
VMEM_LIMIT = 56 * 1024 * 1024


def _cparams(*sem):
    return pltpu.CompilerParams(dimension_semantics=tuple(sem), vmem_limit_bytes=VMEM_LIMIT)


def _dot(a, b):
    return jnp.dot(a.astype(BF16), b.astype(BF16), preferred_element_type=F32)


def _dot_nt(a, b):
    return lax.dot_general(a.astype(BF16), b.astype(BF16), (((1,), (1,)), ((), ())),
                           preferred_element_type=F32)


def _dot_tn(a, b):
    return lax.dot_general(a.astype(BF16), b.astype(BF16), (((0,), (0,)), ((), ())),
                           preferred_element_type=F32)


def _split3(x):
    hi = x.astype(BF16)
    r1 = x - hi.astype(F32)
    mid = r1.astype(BF16)
    lo = (r1 - mid.astype(F32)).astype(BF16)
    return hi, mid, lo


def _sigmoid(x):
    return 1.0 / (1.0 + jnp.exp(-x))


def _modulate(x, g, shift, scale):
    ms = jnp.mean(x * x, axis=-1, keepdims=True)
    return x * lax.rsqrt(ms + NORM_EPS) * g * (1.0 + scale) + shift


def _ada_kernel(c_ref, w_ref, b_ref, o_ref):
    c = c_ref[...]
    s = c * _sigmoid(c)
    o_ref[...] = _dot(s, w_ref[...]) + b_ref[...]


def _ada(cond8, ada_w, ada_b):
    depth, d, n6 = ada_w.shape
    tn = 1536
    return pl.pallas_call(
        _ada_kernel,
        grid=(depth, n6 // tn),
        in_specs=[pl.BlockSpec((8, d), lambda l, j: (0, 0)),
                  pl.BlockSpec((None, d, tn), lambda l, j: (l, 0, j)),
                  pl.BlockSpec((None, 1, tn), lambda l, j: (l, 0, j))],
        out_specs=pl.BlockSpec((None, 8, tn), lambda l, j: (l, 0, j)),
        out_shape=jax.ShapeDtypeStruct((depth, 8, n6), F32),
        compiler_params=_cparams("parallel", "parallel"),
        name="ada_mod",
    )(cond8, ada_w, ada_b.reshape(depth, 1, n6))


def _proj_kernel(x_ref, g_ref, m_ref, w_ref, o_ref, *, shift_row, scale_row):
    h = _modulate(x_ref[...], g_ref[...], m_ref[shift_row:shift_row + 1, :], m_ref[scale_row:scale_row + 1, :])
    o_ref[...] = _dot(h, w_ref[...]).astype(o_ref.dtype)


def _proj(x, g, mod, w, rows_per_batch, mod_base, mod_stride, shift_row, tm, out_dtype=F32):
    n, d = x.shape
    nout = w.shape[1]
    tpb = rows_per_batch // tm
    return pl.pallas_call(
        functools.partial(_proj_kernel, shift_row=shift_row, scale_row=shift_row + 1),
        grid=(n // tm,),
        in_specs=[pl.BlockSpec((tm, d), lambda i: (i, 0)),
                  pl.BlockSpec((1, d), lambda i: (0, 0)),
                  pl.BlockSpec((None, 6, d), lambda i: (mod_base + (i // tpb) * mod_stride, 0, 0)),
                  pl.BlockSpec((d, nout), lambda i: (0, 0))],
        out_specs=pl.BlockSpec((tm, nout), lambda i: (i, 0)),
        out_shape=jax.ShapeDtypeStruct((n, nout), out_dtype),
        compiler_params=_cparams("parallel"),
        name="mod_proj",
    )(x, g.reshape(1, d), mod, w)


def _wkv_chunk_kernel(r_ref, v_ref, an_ref, lw_ref, kd_ref, bb_ref, ry_ref, pq_ref):
    d = pl.program_id(0)
    lw = lw_ref[...]
    width = lw.shape[1]
    row = lax.broadcasted_iota(jnp.int32, (CHUNK, CHUNK), 0)
    col = lax.broadcasted_iota(jnp.int32, (CHUNK, CHUNK), 1)
    sgn = 1 - 2 * d
    tri = jnp.where((row - col) * sgn >= 0, 1.0, 0.0).astype(BF16)
    hi, mid, lo = _split3(lw)
    csum = (jnp.dot(tri, hi, preferred_element_type=F32)
            + jnp.dot(tri, mid, preferred_element_type=F32)
            + jnp.dot(tri, lo, preferred_element_type=F32))
    total = jnp.sum(lw, axis=0, keepdims=True)
    e_pos = jnp.exp(csum)
    e_neg = jnp.exp(-csum)
    r_hat = r_ref[...] * e_pos
    k_hat = kd_ref[...] * e_neg
    b_hat = bb_ref[...] * e_neg
    a_hat = an_ref[...] * jnp.exp(csum - lw)
    g_end = jnp.exp(total)
    v = v_ref[...]

    lane = lax.broadcasted_iota(jnp.int32, (1, PAIR), 1)
    head0 = lane < HEAD_DIM
    i2 = lax.broadcasted_iota(jnp.int32, (PAIR, PAIR), 0)
    j2 = lax.broadcasted_iota(jnp.int32, (PAIR, PAIR), 1)
    ti = jnp.bitwise_and(i2, CHUNK - 1)
    tj = jnp.bitwise_and(j2, CHUNK - 1)
    strict = (ti - tj) * sgn > 0
    incl = (ti - tj) * sgn >= 0
    eye = (i2 == j2).astype(F32)

    def stack(x):
        return jnp.concatenate([jnp.where(head0, x, 0.0), jnp.where(head0, 0.0, x)], axis=0)

    for p in range(width // PAIR):
        sl = slice(p * PAIR, (p + 1) * PAIR)
        a_s, b_s, k_s, r_s, v_s = (stack(t[:, sl]) for t in (a_hat, b_hat, k_hat, r_hat, v))
        m1 = _dot_nt(jnp.concatenate([a_s, r_s], axis=0), jnp.concatenate([b_s, k_s], axis=0))
        n_ab = jnp.where(strict, m1[:PAIR, :PAIR], 0.0)
        a_ak = jnp.where(strict, m1[:PAIR, PAIR:], 0.0)
        a_rb = jnp.where(incl, m1[PAIR:, :PAIR], 0.0)
        a_rk = jnp.where(incl, m1[PAIR:, PAIR:], 0.0)
        x = jnp.concatenate([a_s, _dot(a_ak, v_s)], axis=1)
        npow = n_ab
        steps = int(math.log2(CHUNK))
        for it in range(steps):
            x = x + _dot(npow, x)
            if it + 1 < steps:
                npow = _dot(npow, npow)
        ry = jnp.concatenate([r_s, _dot(a_rk, v_s)], axis=1) + _dot(a_rb, x)
        ry_ref[p] = ry
        ge = g_end[:, sl]
        pt = (eye + _dot_tn(x[:, :PAIR], b_s)) * ge
        qt = (_dot_tn(x[:, PAIR:], b_s) + _dot_tn(v_s, k_s)) * ge
        pq_ref[p] = jnp.concatenate([pt, qt], axis=0)


def _wkv_chunks(r, v, an, lw, kd, bb, batch, t):
    n, w = r.shape
    nc = t // CHUNK
    npair = w // PAIR
    shared = pl.BlockSpec((CHUNK, w), lambda d, b, c: (b * nc + c, 0))
    perdir = pl.BlockSpec((None, CHUNK, w), lambda d, b, c: (d, b * nc + c, 0))
    return pl.pallas_call(
        _wkv_chunk_kernel,
        grid=(2, batch, nc),
        in_specs=[shared, shared, shared, perdir, perdir, perdir],
        out_specs=[pl.BlockSpec((None, None, None, npair, PAIR, 2 * PAIR), lambda d, b, c: (d, b, c, 0, 0, 0)),
                   pl.BlockSpec((None, None, None, npair, 2 * PAIR, PAIR), lambda d, b, c: (d, b, c, 0, 0, 0))],
        out_shape=[jax.ShapeDtypeStruct((2, batch, nc, npair, PAIR, 2 * PAIR), F32),
                   jax.ShapeDtypeStruct((2, batch, nc, npair, 2 * PAIR, PAIR), F32)],
        compiler_params=_cparams("parallel", "parallel", "parallel"),
        name="wkv_chunks",
    )(r, v, an, lw, kd, bb)


def _wkv_scan_kernel(ryf_ref, ryb_ref, pqf_ref, pqb_ref, g0_ref, yf_ref, yb_ref, gout_ref, g_ref):
    pos = pl.program_id(0)

    @pl.when(pos == 0)
    def _():
        g_ref[...] = g0_ref[...]

    batch, npair = g_ref.shape[1], g_ref.shape[2]
    for d, (ry_ref, pq_ref, y_ref) in enumerate(((ryf_ref, pqf_ref, yf_ref), (ryb_ref, pqb_ref, yb_ref))):
        for b in range(batch):
            for p in range(npair):
                g = g_ref[d, b, p]
                ry = ry_ref[b, p]
                pq = pq_ref[b, p]
                ys = _dot_nt(ry[:, :PAIR], g) + ry[:, PAIR:]
                y_ref[b, :, p * PAIR:(p + 1) * PAIR] = ys[:CHUNK] + ys[CHUNK:]
                g_ref[d, b, p] = _dot(g, pq[:PAIR]) + pq[PAIR:]

    @pl.when(pos == pl.num_programs(0) - 1)
    def _():
        gout_ref[...] = g_ref[...]


def _wkv_scan(ry, pq, g0):
    _, batch, nc, npair, _, _ = ry.shape
    w = npair * PAIR
    t = nc * CHUNK
    ry_f = pl.BlockSpec((None, batch, None, npair, PAIR, 2 * PAIR), lambda s: (0, 0, s, 0, 0, 0))
    ry_b = pl.BlockSpec((None, batch, None, npair, PAIR, 2 * PAIR), lambda s: (1, 0, nc - 1 - s, 0, 0, 0))
    pq_f = pl.BlockSpec((None, batch, None, npair, 2 * PAIR, PAIR), lambda s: (0, 0, s, 0, 0, 0))
    pq_b = pl.BlockSpec((None, batch, None, npair, 2 * PAIR, PAIR), lambda s: (1, 0, nc - 1 - s, 0, 0, 0))
    gspec = pl.BlockSpec((2, batch, npair, PAIR, PAIR), lambda s: (0, 0, 0, 0, 0))
    return pl.pallas_call(
        _wkv_scan_kernel,
        grid=(nc,),
        in_specs=[ry_f, ry_b, pq_f, pq_b, gspec],
        out_specs=[pl.BlockSpec((batch, CHUNK, w), lambda s: (0, s, 0)),
                   pl.BlockSpec((batch, CHUNK, w), lambda s: (0, nc - 1 - s, 0)),
                   gspec],
        out_shape=[jax.ShapeDtypeStruct((batch, t, w), F32),
                   jax.ShapeDtypeStruct((batch, t, w), F32),
                   jax.ShapeDtypeStruct((2, batch, npair, PAIR, PAIR), F32)],
        scratch_shapes=[pltpu.VMEM((2, batch, npair, PAIR, PAIR), F32)],
        compiler_params=_cparams("arbitrary"),
        name="wkv_scan",
    )(ry, ry, pq, pq, g0)


GRID_SHIFT = ((-1, "first_col"), (1, "last_col"), (-GRID_W, None), (GRID_W, None))
SEQ_SHIFT = ((-1, None), (1, None))


def _prep_kernel(xp_ref, xm_ref, xn_ref, pp_ref, pm_ref, pn_ref, g_ref, m_ref,
                 mux_ref, mup_ref, dw0_ref, dw1_ref, dw2_ref, la0_ref, la1_ref, la2_ref,
                 gg1_ref, gg2_ref, kk_ref, ka_ref, rk_ref, pw_ref, ps_ref, seg_ref,
                 r_out, v_out, an_out, lw_out, kd_out, bb_out, bonus_out, gate_out, yb_out,
                 hext_ref, pext_ref, *, parts, tm, pad, tiles_per_batch, seq_len):
    i = pl.program_id(0)
    tile = lax.rem(i, tiles_per_batch)
    keep_prev = jnp.where(tile == 0, 0.0, 1.0)
    keep_next = jnp.where(tile == tiles_per_batch - 1, 0.0, 1.0)
    g = g_ref[...]
    shift, scale = m_ref[0:1, :], m_ref[1:2, :]
    d_model = xm_ref.shape[1]
    wa = r_out.shape[1]

    hext_ref[0:pad, :] = _modulate(xp_ref[...], g, shift, scale) * keep_prev
    hext_ref[pad:pad + tm, :] = _modulate(xm_ref[...], g, shift, scale)
    hext_ref[pad + tm:, :] = _modulate(xn_ref[...], g, shift, scale) * keep_next
    pext_ref[0:pad, :] = pp_ref[...] * keep_prev
    pext_ref[pad:pad + tm, :] = pm_ref[...]
    pext_ref[pad + tm:, :] = pn_ref[...] * keep_next

    colidx = jnp.bitwise_and(lax.broadcasted_iota(jnp.int32, (tm, 1), 0), GRID_W - 1)

    def shifted(ext_ref, col0, width):
        pw = width // len(parts)
        outs = []
        for q, (off, mask) in enumerate(parts):
            blk = ext_ref[pad + off:pad + off + tm, col0 + q * pw:col0 + (q + 1) * pw]
            if mask == "first_col":
                blk = jnp.where(colidx == 0, 0.0, blk)
            elif mask == "last_col":
                blk = jnp.where(colidx == GRID_W - 1, 0.0, blk)
            outs.append(blk)
        return jnp.concatenate(outs, axis=1)

    h = hext_ref[pad:pad + tm, :]
    hx = shifted(hext_ref, 0, d_model) - h
    x_w = h + hx * mux_ref[0:1, :]
    x_a = h + hx * mux_ref[1:2, :]
    x_g = h + hx * mux_ref[2:3, :]
    zw_mid = jnp.tanh(_dot(x_w, dw1_ref[...]))
    xa_mid = _dot(x_a, la1_ref[...])
    gate_out[...] = _dot(_sigmoid(_dot(x_g, gg1_ref[...])), gg2_ref[...])

    def mixed(n):
        p_n = pext_ref[pad:pad + tm, n * wa:(n + 1) * wa]
        return p_n + (shifted(pext_ref, n * wa, wa) - p_n) * mup_ref[n:n + 1, :]

    r, k, v = mixed(0), mixed(1), mixed(2)
    seg = seg_ref[...]
    kk = k * kk_ref[...]
    kk = kk * lax.rsqrt(jnp.maximum(_dot(kk * kk, seg), 1e-12))
    r_out[...] = r
    v_out[...] = v
    an_out[...] = -kk
    ka = ka_ref[...]
    kd_sum = jnp.zeros_like(k)
    for d in range(2):
        zw = dw0_ref[d:d + 1, :] + _dot(zw_mid, dw2_ref[d])
        lw_out[d] = -math.exp(-0.5) * _sigmoid(zw)
        a_lr = _sigmoid(la0_ref[d:d + 1, :] + _dot(xa_mid, la2_ref[d]))
        kd = k * (1.0 + (a_lr - 1.0) * ka)
        kd_out[d] = kd
        bb_out[d] = kk * a_lr
        kd_sum = kd_sum + kd
    bonus_out[...] = _dot(r * kd_sum * rk_ref[...], seg) * v

    pos = tile * tm + lax.broadcasted_iota(jnp.int32, (tm, 1), 0)
    trow = lax.broadcasted_iota(jnp.int32, (tm, tm + 2 * pad), 0)
    srow = lax.broadcasted_iota(jnp.int32, (tm, tm + 2 * pad), 1) - pad
    gp = wa // len(POOL_WINDOWS)
    ybs = []
    for gi, win in enumerate(POOL_WINDOWS):
        half = win // 2
        c0 = 3 * wa + gi * gp
        band = jnp.where((srow >= trow - half) & (srow < trow + half), 1.0, 0.0).astype(BF16)
        sums = _dot(band, pext_ref[:, c0:c0 + gp])
        cnt = (jnp.minimum(pos + half, seq_len) - jnp.maximum(pos - half, 0)).astype(F32)
        diff = sums / cnt - pext_ref[pad:pad + tm, c0:c0 + gp]
        ybs.append(_dot(diff, pw_ref[gi]))
    yb_out[...] = jnp.concatenate(ybs, axis=1) * ps_ref[...]


def _rwkv_prep(x, proj, g, mod, mod_base, mod_stride, wts, batch, t, grid_mode):
    n, d = x.shape
    wa = d // 2
    pcols = proj.shape[1]
    tm = 256
    pad = GRID_W if grid_mode else 8
    parts = GRID_SHIFT if grid_mode else SEQ_SHIFT
    tpb = t // tm
    hb = tm // pad
    nhb = n // pad
    main = lambda i: (i, 0)
    prev = lambda i: (jnp.maximum(i * hb - 1, 0), 0)
    nxt = lambda i: (jnp.minimum((i + 1) * hb, nhb - 1), 0)
    full2 = lambda i: (0, 0)
    full3 = lambda i: (0, 0, 0)
    in_specs = [pl.BlockSpec((pad, d), prev), pl.BlockSpec((tm, d), main), pl.BlockSpec((pad, d), nxt),
                pl.BlockSpec((pad, pcols), prev), pl.BlockSpec((tm, pcols), main), pl.BlockSpec((pad, pcols), nxt),
                pl.BlockSpec((1, d), full2),
                pl.BlockSpec((None, 6, d), lambda i: (mod_base + (i // tpb) * mod_stride, 0, 0))]
    for a in wts:
        in_specs.append(pl.BlockSpec(a.shape, full2 if a.ndim == 2 else full3))
    one = pl.BlockSpec((tm, wa), main)
    two = pl.BlockSpec((2, tm, wa), lambda i: (0, i, 0))
    sd1 = jax.ShapeDtypeStruct((n, wa), F32)
    sd2 = jax.ShapeDtypeStruct((2, n, wa), F32)
    return pl.pallas_call(
        functools.partial(_prep_kernel, parts=parts, tm=tm, pad=pad, tiles_per_batch=tpb, seq_len=t),
        grid=(n // tm,),
        in_specs=in_specs,
        out_specs=[one, one, one, two, two, two, one, one, one],
        out_shape=[sd1, sd1, sd1, sd2, sd2, sd2, sd1, sd1, sd1],
        scratch_shapes=[pltpu.VMEM((tm + 2 * pad, d), F32), pltpu.VMEM((tm + 2 * pad, pcols), F32)],
        compiler_params=_cparams("parallel"),
        name="rwkv_prep",
    )(x, x, x, proj, proj, proj, g.reshape(1, d), mod, *wts)


def _mix_out_kernel(y0_ref, y1_ref, bonus_ref, gate_ref, yb_ref, x_ref, m_ref, gnw_ref, gnb_ref,
                    seg_ref, wout_ref, o_ref):
    y = y0_ref[...] + y1_ref[...]
    seg = seg_ref[...]
    inv = 1.0 / HEAD_DIM
    yh = y.astype(BF16)
    mu = (jnp.dot(yh, seg, preferred_element_type=F32) + _dot(y - yh.astype(F32), seg)) * inv
    dlt = y - mu
    var = _dot(dlt * dlt, seg) * inv
    yn = dlt * lax.rsqrt(var + GN_EPS) * gnw_ref[...] + gnb_ref[...]
    ya = (yn + bonus_ref[...]) * gate_ref[...]
    cat = jnp.concatenate([ya, yb_ref[...]], axis=1)
    o_ref[...] = x_ref[...] + m_ref[2:3, :] * _dot(cat, wout_ref[...])


def _mix_out(y0, y1, bonus, gate, yb, x, mod, mod_base, mod_stride, gnw, gnb, seg, wout, t, tm):
    n, d = x.shape
    wa = d // 2
    tpb = t // tm
    half = pl.BlockSpec((tm, wa), lambda i: (i, 0))
    full = pl.BlockSpec((tm, d), lambda i: (i, 0))
    const = lambda a: pl.BlockSpec(a.shape, lambda i: (0, 0))
    return pl.pallas_call(
        _mix_out_kernel,
        grid=(n // tm,),
        in_specs=[half, half, half, half, half, full,
                  pl.BlockSpec((None, 6, d), lambda i: (mod_base + (i // tpb) * mod_stride, 0, 0)),
                  const(gnw), const(gnb), const(seg), const(wout)],
        out_specs=full,
        out_shape=jax.ShapeDtypeStruct((n, d), F32),
        input_output_aliases={5: 0},
        compiler_params=_cparams("parallel"),
        name="mix_out",
    )(y0, y1, bonus, gate, yb, x, mod, gnw, gnb, seg, wout)


def _chan_dft_kernel(x_ref, g_ref, m_ref, cs_ref, xc_ref, xs_ref, *, groups):
    h = _modulate(x_ref[...], g_ref[...], m_ref[0:1, :], m_ref[1:2, :])
    gc = h.shape[1] // groups
    cs = cs_ref[...]
    for gi in range(groups):
        res = _dot(h[:, gi * gc:(gi + 1) * gc], cs)
        xc_ref[:, gi * gc:(gi + 1) * gc] = res[:, :gc].astype(xc_ref.dtype)
        xs_ref[:, gi * gc:(gi + 1) * gc] = res[:, gc:].astype(xs_ref.dtype)


def _chan_dft(x, g, mod, mod_base, mod_stride, cs, t, tm, groups):
    n, d = x.shape
    tpb = t // tm
    full = pl.BlockSpec((tm, d), lambda i: (i, 0))
    return pl.pallas_call(
        functools.partial(_chan_dft_kernel, groups=groups),
        grid=(n // tm,),
        in_specs=[full, pl.BlockSpec((1, d), lambda i: (0, 0)),
                  pl.BlockSpec((None, 6, d), lambda i: (mod_base + (i // tpb) * mod_stride, 0, 0)),
                  pl.BlockSpec(cs.shape, lambda i: (0, 0))],
        out_specs=[full, full],
        out_shape=[jax.ShapeDtypeStruct((n, d), BF16)] * 2,
        compiler_params=_cparams("parallel"),
        name="chan_dft",
    )(x, g.reshape(1, d), mod, cs)


def _time_dft_kernel(ct_ref, st_ref, xc_ref, xs_ref, x_ref, m_ref, wf_ref, o_ref, acc_ref):
    ki = pl.program_id(2)

    @pl.when(ki == 0)
    def _():
        acc_ref[...] = jnp.zeros_like(acc_ref)

    acc_ref[...] += (jnp.dot(ct_ref[...], xc_ref[...], preferred_element_type=F32)
                     - jnp.dot(st_ref[...], xs_ref[...], preferred_element_type=F32))

    @pl.when(ki == pl.num_programs(2) - 1)
    def _():
        o_ref[...] = x_ref[...] + m_ref[2:3, :] * _dot(acc_ref[...], wf_ref[...])


def _time_dft(ct, st, xc, xs, x, mod, mod_base, mod_stride, wf, batch, t, tf, tk):
    n, d = x.shape
    nf, nk = t // tf, t // tk
    return pl.pallas_call(
        _time_dft_kernel,
        grid=(batch, nf, nk),
        in_specs=[pl.BlockSpec((tf, tk), lambda b, f, k: (f, k)),
                  pl.BlockSpec((tf, tk), lambda b, f, k: (f, k)),
                  pl.BlockSpec((tk, d), lambda b, f, k: (b * nk + k, 0)),
                  pl.BlockSpec((tk, d), lambda b, f, k: (b * nk + k, 0)),
                  pl.BlockSpec((tf, d), lambda b, f, k: (b * nf + f, 0)),
                  pl.BlockSpec((None, 6, d), lambda b, f, k: (mod_base + b * mod_stride, 0, 0)),
                  pl.BlockSpec((d, d), lambda b, f, k: (0, 0))],
        out_specs=pl.BlockSpec((tf, d), lambda b, f, k: (b * nf + f, 0)),
        out_shape=jax.ShapeDtypeStruct((n, d), F32),
        scratch_shapes=[pltpu.VMEM((tf, d), F32)],
        input_output_aliases={4: 0},
        compiler_params=_cparams("parallel", "parallel", "arbitrary"),
        name="time_dft",
    )(ct, st, xc, xs, x, mod, wf)


def _dft_tables(n, scale):
    def small(rows, cols, period):
        prod = jnp.bitwise_and(jnp.arange(rows, dtype=jnp.int32)[:, None] * jnp.arange(cols, dtype=jnp.int32)[None, :],
                               period - 1)
        ang = prod.astype(F32) * (2.0 * math.pi / period)
        return jnp.cos(ang), jnp.sin(ang)

    if n <= 256:
        c, s = small(n, n, n)
        return c * scale, s * scale
    lo_n = 128
    hi_n = n // lo_n
    ca, sa = small(n, hi_n, hi_n)
    cb, sb = small(n, lo_n, n)
    ca, sa = ca[:, :, None] * scale, sa[:, :, None] * scale
    cb, sb = cb[:, None, :], sb[:, None, :]
    return (ca * cb - sa * sb).reshape(n, n), (sa * cb + ca * sb).reshape(n, n)


def _router_kernel(x_ref, g_ref, m_ref, wr_ref, br_ref, tri_ref, h_out, ids_out, gates_out, cnt_out, base_ref):
    i = pl.program_id(0)

    @pl.when(i == 0)
    def _():
        base_ref[...] = jnp.zeros_like(base_ref)

    h2 = _modulate(x_ref[...], g_ref[...], m_ref[3:4, :], m_ref[4:5, :])
    h_out[...] = h2
    hs = _split3(h2)
    logits = br_ref[...]
    for ia, ib in ((0, 0), (0, 1), (1, 0), (0, 2), (2, 0), (1, 1)):
        logits = logits + jnp.dot(hs[ia], wr_ref[ib], preferred_element_type=F32)

    tm = logits.shape[0]
    lane_i = lax.broadcasted_iota(jnp.int32, (tm, 128), 1)
    lane = lane_i.astype(F32)
    neg = -jnp.inf
    gmask = lane_i < MOE_GROUPS
    lc = jnp.where(gmask, logits, neg)
    mc = jnp.max(lc, axis=1, keepdims=True)
    sc = jnp.sum(jnp.where(gmask, jnp.exp(logits - mc), 0.0), axis=1, keepdims=True)
    g_val = 1.0 / sc
    g_idx = jnp.min(jnp.where(lc == mc, lane, 128.0), axis=1, keepdims=True)
    lgroup = jnp.where(lane_i >= MOE_GROUPS, jnp.right_shift(lane_i - MOE_GROUPS, 3), -1).astype(F32)
    sel = lgroup == g_idx
    lf = jnp.where(sel, logits, neg)
    m1 = jnp.max(lf, axis=1, keepdims=True)
    ef = jnp.where(sel, jnp.exp(logits - m1), 0.0)
    p = ef / jnp.sum(ef, axis=1, keepdims=True)
    p1 = jnp.where(sel, p, -1.0)
    v1 = jnp.max(p1, axis=1, keepdims=True)
    i1 = jnp.min(jnp.where(p1 == v1, lane, 128.0), axis=1, keepdims=True)
    p2 = jnp.where(lane == i1, -1.0, p1)
    v2 = jnp.max(p2, axis=1, keepdims=True)
    i2 = jnp.min(jnp.where(p2 == v2, lane, 128.0), axis=1, keepdims=True)
    denom = v1 + v2
    gate1 = g_val * v1 / denom
    gate2 = g_val * v2 / denom
    e1 = i1 - MOE_GROUPS
    e2 = i2 - MOE_GROUPS

    tri = tri_ref[...]
    oh1 = jnp.where(lane == e1, 1.0, 0.0)
    oh2 = jnp.where(lane == e2, 1.0, 0.0)
    base = base_ref[...]
    tot1 = jnp.sum(oh1, axis=0, keepdims=True)
    tot2 = jnp.sum(oh2, axis=0, keepdims=True)
    c1 = jnp.dot(tri, oh1.astype(BF16), preferred_element_type=F32)
    c2 = jnp.dot(tri, oh2.astype(BF16), preferred_element_type=F32)
    rank1 = jnp.sum(oh1 * (base + c1), axis=1, keepdims=True)
    rank2 = jnp.sum(oh2 * (base + tot1 + c2), axis=1, keepdims=True)
    base = base + tot1 + tot2
    base_ref[...] = base
    cnt_out[...] = base
    ids_out[...] = jnp.where(lane_i == 0, e1, jnp.where(lane_i == 1, e2, jnp.where(
        lane_i == 2, rank1, jnp.where(lane_i == 3, rank2, 0.0)))).astype(jnp.int32)
    gates_out[...] = jnp.where(lane_i == 0, gate1, jnp.where(lane_i == 1, gate2, 0.0))


def _router(x, g, mod, mod_base, mod_stride, wr3, br, tri, t, tm):
    n, d = x.shape
    tpb = t // tm
    full = pl.BlockSpec((tm, d), lambda i: (i, 0))
    lanes = pl.BlockSpec((tm, 128), lambda i: (i, 0))
    return pl.pallas_call(
        _router_kernel,
        grid=(n // tm,),
        in_specs=[full, pl.BlockSpec((1, d), lambda i: (0, 0)),
                  pl.BlockSpec((None, 6, d), lambda i: (mod_base + (i // tpb) * mod_stride, 0, 0)),
                  pl.BlockSpec(wr3.shape, lambda i: (0, 0, 0)),
                  pl.BlockSpec((1, 128), lambda i: (0, 0)),
                  pl.BlockSpec((tm, tm), lambda i: (0, 0))],
        out_specs=[full, lanes, lanes, pl.BlockSpec((1, 128), lambda i: (0, 0))],
        out_shape=[jax.ShapeDtypeStruct((n, d), F32), jax.ShapeDtypeStruct((n, 128), jnp.int32),
                   jax.ShapeDtypeStruct((n, 128), F32), jax.ShapeDtypeStruct((1, 128), F32)],
        scratch_shapes=[pltpu.VMEM((1, 128), F32)],
        compiler_params=_cparams("arbitrary"),
        name="moe_router",
    )(x, g.reshape(1, d), mod, wr3, br, tri)


def _row_copy(src, src_row, dst, dst_row, sem):
    return pltpu.make_async_copy(src.at[pl.ds(src_row, 1)], dst.at[pl.ds(dst_row, 1)], sem)


def _dispatch_kernel(dest_ref, h_ref, xin_ref, xbuf_ref, sem, *, tm):
    del xin_ref
    base = pl.program_id(0) * tm

    def body(r, carry):
        tok = base + r
        for j in range(2):
            _row_copy(h_ref, tok, xbuf_ref, dest_ref[2 * tok + j], sem).start()
        return carry

    lax.fori_loop(0, tm, body, 0, unroll=8)

    def drain(r, carry):
        for j in range(2):
            _row_copy(h_ref, 0, xbuf_ref, 0, sem).wait()
        return carry

    lax.fori_loop(0, tm, drain, 0, unroll=8)


def _dispatch(dest, h2, xbuf0, tm):
    n, d = h2.shape
    return pl.pallas_call(
        functools.partial(_dispatch_kernel, tm=tm),
        grid_spec=pltpu.PrefetchScalarGridSpec(
            num_scalar_prefetch=1,
            grid=(n // tm,),
            in_specs=[pl.BlockSpec(memory_space=pl.ANY), pl.BlockSpec(memory_space=pl.ANY)],
            out_specs=pl.BlockSpec(memory_space=pl.ANY),
            scratch_shapes=[pltpu.SemaphoreType.DMA],
        ),
        out_shape=jax.ShapeDtypeStruct(xbuf0.shape, F32),
        input_output_aliases={2: 0},
        compiler_params=_cparams("arbitrary"),
        name="moe_dispatch",
    )(dest, h2, xbuf0)


def _expert_kernel(be_ref, na_ref, x_ref, w1_ref, w3_ref, w2_ref, o_ref):
    active = pl.program_id(0) < na_ref[0]

    @pl.when(active)
    def _():
        xb = x_ref[...].astype(BF16)
        h1 = jnp.dot(xb, w1_ref[...].astype(BF16), preferred_element_type=F32)
        h3 = jnp.dot(xb, w3_ref[...].astype(BF16), preferred_element_type=F32)
        act = h1 * _sigmoid(h1) * h3
        o_ref[...] = _dot(act, w2_ref[...])

    @pl.when(jnp.logical_not(active))
    def _():
        o_ref[...] = jnp.zeros_like(o_ref)


def _experts(block_e, nact, xbuf, w1, w3, w2):
    rows, d = xbuf.shape
    nb = rows // MOE_BLOCK
    de = w1.shape[2]
    blk = lambda i, be, na: (jnp.minimum(i, na[0] - 1), 0)
    wmap = lambda i, be, na: (be[jnp.minimum(i, na[0] - 1)], 0, 0)
    return pl.pallas_call(
        _expert_kernel,
        grid_spec=pltpu.PrefetchScalarGridSpec(
            num_scalar_prefetch=2,
            grid=(nb,),
            in_specs=[pl.BlockSpec((MOE_BLOCK, d), blk),
                      pl.BlockSpec((None, d, de), wmap),
                      pl.BlockSpec((None, d, de), wmap),
                      pl.BlockSpec((None, de, d), wmap)],
            out_specs=pl.BlockSpec((MOE_BLOCK, d), lambda i, be, na: (i, 0)),
        ),
        out_shape=jax.ShapeDtypeStruct((rows, d), F32),
        compiler_params=_cparams("arbitrary"),
        name="moe_experts",
    )(block_e, nact, xbuf, w1, w3, w2)


def _combine_kernel(dest_ref, y_ref, gates_ref, x_ref, m_ref, fn_ref, o_ref, rows_ref, sem, *, tm, final):
    base = pl.program_id(0) * tm

    def body(r, carry):
        tok = base + r
        for j in range(2):
            _row_copy(y_ref, dest_ref[2 * tok + j], rows_ref.at[j], r, sem).start()
        return carry

    lax.fori_loop(0, tm, body, 0, unroll=8)

    def drain(r, carry):
        for j in range(2):
            _row_copy(y_ref, 0, rows_ref.at[j], 0, sem).wait()
        return carry

    lax.fori_loop(0, tm, drain, 0, unroll=8)

    gates = gates_ref[...]
    y = gates[:, 0:1] * rows_ref[0] + gates[:, 1:2] * rows_ref[1]
    out = x_ref[...] + m_ref[5:6, :] * y
    if final:
        ms = jnp.mean(out * out, axis=-1, keepdims=True)
        out = out * lax.rsqrt(ms + NORM_EPS) * fn_ref[...]
    o_ref[...] = out


def _combine(dest, ybuf, gates, x, mod, mod_base, mod_stride, fnorm, t, tm, final):
    n, d = x.shape
    tpb = t // tm
    full = pl.BlockSpec((tm, d), lambda i, dr: (i, 0))
    return pl.pallas_call(
        functools.partial(_combine_kernel, tm=tm, final=final),
        grid_spec=pltpu.PrefetchScalarGridSpec(
            num_scalar_prefetch=1,
            grid=(n // tm,),
            in_specs=[pl.BlockSpec(memory_space=pl.ANY),
                      pl.BlockSpec((tm, 128), lambda i, dr: (i, 0)),
                      full,
                      pl.BlockSpec((None, 6, d), lambda i, dr: (mod_base + (i // tpb) * mod_stride, 0, 0)),
                      pl.BlockSpec((1, d), lambda i, dr: (0, 0))],
            out_specs=full,
            scratch_shapes=[pltpu.VMEM((2, tm, d), F32), pltpu.SemaphoreType.DMA],
        ),
        out_shape=jax.ShapeDtypeStruct((n, d), F32),
        input_output_aliases={3: 0},
        compiler_params=_cparams("arbitrary"),
        name="moe_combine",
    )(dest, ybuf, gates, x, mod, fnorm)


def _moe(x, g, mod, mod_base, mod_stride, rt, w1, w3, w2, fnorm, t, tm, final):
    n, d = x.shape
    n_exp = w1.shape[0]
    wr3, br, tri = rt
    h2, ids, gates, counts = _router(x, g, mod, mod_base, mod_stride, wr3, br, tri, t, tm)
    counts = counts[0, :n_exp].astype(jnp.int32)
    padded = (counts + MOE_BLOCK - 1) // MOE_BLOCK * MOE_BLOCK
    pend = jnp.cumsum(padded)
    pstart = pend - padded
    dest = (pstart[ids[:, 0:2]] + ids[:, 2:4]).reshape(2 * n)
    nb = (2 * n) // MOE_BLOCK + n_exp
    block_e = jnp.minimum(jnp.searchsorted(pend, jnp.arange(nb, dtype=jnp.int32) * MOE_BLOCK, side="right"),
                          n_exp - 1).astype(jnp.int32)
    nact = (pend[-1:] // MOE_BLOCK).astype(jnp.int32)
    xbuf = _dispatch(dest, h2, jnp.zeros((nb * MOE_BLOCK, d), F32), tm)
    ybuf = _experts(block_e, nact, xbuf, w1, w3, w2)
    return _combine(dest, ybuf, gates, x, mod, mod_base, mod_stride, fnorm, t, tm, final)


def _seg_ones(width):
    idx = np.arange(width) // HEAD_DIM
    return jnp.asarray((idx[:, None] == idx[None, :]).astype(np.float32), dtype=BF16)


def _even_weights(j, mu_x, mu_p, decay_w0, decay_w1, decay_w2, lr_a0, lr_a1, lr_a2, gate_g1, gate_g2,
                  k_k, k_a, r_k, pool_w, pool_scale, seg):
    wa = mu_p.shape[-1]

    def cat1(w):
        return jnp.concatenate([w[0], w[1]], axis=1).astype(BF16)

    def pad2(w):
        z = jnp.zeros_like(w[0])
        return jnp.stack([jnp.concatenate([w[0], z], axis=0), jnp.concatenate([z, w[1]], axis=0)]).astype(BF16)

    return (mu_x[j], mu_p[j], decay_w0[j], cat1(decay_w1[j]), pad2(decay_w2[j]),
            lr_a0[j], cat1(lr_a1[j]), pad2(lr_a2[j]),
            gate_g1[j].astype(BF16), gate_g2[j].astype(BF16),
            k_k[j].reshape(1, wa), k_a[j].reshape(1, wa), r_k[j].reshape(1, wa),
            pool_w[j].astype(BF16), pool_scale[j].reshape(1, wa), seg)


def kernel(x, c, ctx, c_ctx, ada_w, ada_b, norm_mix, norm_ffn, w_in, mu_x, mu_p, decay_w0, decay_w1, decay_w2, lr_a0, lr_a1, lr_a2, gate_g1, gate_g2, k_k, k_a, r_k, gn_w, gn_b, pool_w, pool_scale, w_out, w_fourier, router_c, router_c_b, router_f, router_f_b, moe_w1, moe_w3, moe_w2, final_norm):
    batch, t, d = x.shape
    tc = ctx.shape[1]
    depth = ada_w.shape[0]
    wa = d // 2
    n, ncx = batch * t, batch * tc
    assert batch <= 4 and t % 512 == 0 and tc % 256 == 0 and d % 512 == 0

    cond8 = jnp.zeros((8, d), F32).at[:batch].set(c).at[4].set(c_ctx)
    mod_all = _ada(cond8, ada_w, ada_b).reshape(depth, 8, 6, d)
    seg = _seg_ones(wa)
    fnorm = final_norm.reshape(1, d)

    fgroups = 4
    gc = d // fgroups
    cc, sc = _dft_tables(gc, gc ** -0.5)
    cs = jnp.concatenate([cc, sc], axis=1).astype(BF16)
    ct_lat, st_lat = (a.astype(BF16) for a in _dft_tables(t, t ** -0.5))
    ct_ctx, st_ctx = (a.astype(BF16) for a in _dft_tables(tc, tc ** -0.5))

    n_exp = moe_w1.shape[1]
    tri = {tm: jnp.asarray(np.tril(np.ones((tm, tm), np.float32), -1), dtype=BF16) for tm in (512, 256)}

    lat = x.reshape(n, d)
    cx = ctx.reshape(ncx, d)
    last_read = 2 * ((depth - 1) // 2)
    npair = wa // PAIR

    for i in range(depth):
        ctx_in = i <= last_read
        ctx_out = i < last_read
        mod = mod_all[i]
        j = i // 2
        if i % 2 == 0:
            wts = _even_weights(j, mu_x, mu_p, decay_w0, decay_w1, decay_w2, lr_a0, lr_a1, lr_a2,
                                gate_g1, gate_g2, k_k, k_a, r_k, pool_w, pool_scale, seg)
            w_in_b = w_in[j].astype(BF16)
            w_out_b = w_out[j].astype(BF16)
            gnw, gnb = gn_w[j].reshape(1, wa), gn_b[j].reshape(1, wa)
            pc = _proj(cx, norm_mix[i], mod, w_in_b, tc, 4, 0, 0, 256)
            fc = _rwkv_prep(cx, pc, norm_mix[i], mod, 4, 0, wts, batch, tc, False)
            ryc, pqc = _wkv_chunks(*fc[:6], batch, tc)
            g0 = jnp.zeros((2, batch, npair, PAIR, PAIR), F32)
            yc0, yc1, gctx = _wkv_scan(ryc, pqc, g0)
            pl_ = _proj(lat, norm_mix[i], mod, w_in_b, t, 0, 1, 0, 512)
            fl = _rwkv_prep(lat, pl_, norm_mix[i], mod, 0, 1, wts, batch, t, True)
            ryl, pql = _wkv_chunks(*fl[:6], batch, t)
            yl0, yl1, _ = _wkv_scan(ryl, pql, gctx)
            lat = _mix_out(yl0.reshape(n, wa), yl1.reshape(n, wa), fl[6], fl[7], fl[8], lat, mod, 0, 1,
                           gnw, gnb, seg, w_out_b, t, 512)
            if ctx_out:
                cx = _mix_out(yc0.reshape(ncx, wa), yc1.reshape(ncx, wa), fc[6], fc[7], fc[8], cx, mod, 4, 0,
                              gnw, gnb, seg, w_out_b, tc, 256)
        else:
            wf_b = w_fourier[j].astype(BF16)
            xc, xs = _chan_dft(lat, norm_mix[i], mod, 0, 1, cs, t, 512, fgroups)
            lat = _time_dft(ct_lat, st_lat, xc, xs, lat, mod, 0, 1, wf_b, batch, t, 1024, 1024)
            if ctx_out:
                xcc, xsc = _chan_dft(cx, norm_mix[i], mod, 4, 0, cs, tc, 256, fgroups)
                cx = _time_dft(ct_ctx, st_ctx, xcc, xsc, cx, mod, 4, 0, wf_b, batch, tc, 256, 256)
        wr = jnp.zeros((d, 128), F32).at[:, :MOE_GROUPS].set(router_c[i]).at[:, MOE_GROUPS:MOE_GROUPS + n_exp].set(router_f[i])
        br = jnp.zeros((1, 128), F32).at[0, :MOE_GROUPS].set(router_c_b[i]).at[0, MOE_GROUPS:MOE_GROUPS + n_exp].set(router_f_b[i])
        wr3 = jnp.stack(_split3(wr))
        final = i == depth - 1
        lat = _moe(lat, norm_ffn[i], mod, 0, 1, (wr3, br, tri[512]), moe_w1[i], moe_w3[i], moe_w2[i], fnorm,
                   t, 512, final)
        if ctx_out:
            cx = _moe(cx, norm_ffn[i], mod, 4, 0, (wr3, br, tri[256]), moe_w1[i], moe_w3[i], moe_w2[i], fnorm,
                      tc, 256, False)
    return lat.reshape(batch, t, d)
```

```python
import functools
import math

import jax
import jax.numpy as jnp
import numpy as np
from jax import lax
from jax.experimental import pallas as pl
from jax.experimental.pallas import tpu as pltpu

F32 = jnp.float32
BF16 = jnp.bfloat16

GRID_W = 64
HEAD_DIM = 64
CHUNK = 64
PAIR = 2 * HEAD_DIM
NORM_EPS = 1e-6
GN_EPS = 64e-5
POOL_WINDOWS = (2, 4, 8, 16)
MOE_GROUPS = 4
EXPERTS_PER_GROUP = 8
MOE_BLOCK = 256
VMEM_LIMIT = 56 * 1024 * 1024


def _cparams(*sem):
    return pltpu.CompilerParams(dimension_semantics=tuple(sem), vmem_limit_bytes=VMEM_LIMIT)


def _dot(a, b):
    return jnp.dot(a.astype(BF16), b.astype(BF16), preferred_element_type=F32)


def _dot_nt(a, b):
    return lax.dot_general(a.astype(BF16), b.astype(BF16), (((1,), (1,)), ((), ())),
                           preferred_element_type=F32)


def _dot_tn(a, b):
    return lax.dot_general(a.astype(BF16), b.astype(BF16), (((0,), (0,)), ((), ())),
                           preferred_element_type=F32)


def _split3(x):
    hi = x.astype(BF16)
    r1 = x - hi.astype(F32)
    mid = r1.astype(BF16)
    lo = (r1 - mid.astype(F32)).astype(BF16)
    return hi, mid, lo


def _sigmoid(x):
    return 1.0 / (1.0 + jnp.exp(-x))


def _modulate(x, g, shift, scale):
    ms = jnp.mean(x * x, axis=-1, keepdims=True)
    return x * lax.rsqrt(ms + NORM_EPS) * g * (1.0 + scale) + shift


def _ada_kernel(c_ref, w_ref, b_ref, o_ref):
    c = c_ref[...]
    s = c * _sigmoid(c)
    o_ref[...] = _dot(s, w_ref[...]) + b_ref[...]


def _ada(cond8, ada_w, ada_b):
    depth, d, n6 = ada_w.shape
    tn = 1536
    return pl.pallas_call(
        _ada_kernel,
        grid=(depth, n6 // tn),
        in_specs=[pl.BlockSpec((8, d), lambda l, j: (0, 0)),
                  pl.BlockSpec((None, d, tn), lambda l, j: (l, 0, j)),
                  pl.BlockSpec((None, 1, tn), lambda l, j: (l, 0, j))],
        out_specs=pl.BlockSpec((None, 8, tn), lambda l, j: (l, 0, j)),
        out_shape=jax.ShapeDtypeStruct((depth, 8, n6), F32),
        compiler_params=_cparams("parallel", "parallel"),
        name="ada_mod",
    )(cond8, ada_w, ada_b.reshape(depth, 1, n6))


def _proj_kernel(x_ref, g_ref, m_ref, w_ref, o_ref, *, shift_row, scale_row):
    h = _modulate(x_ref[...], g_ref[...], m_ref[shift_row:shift_row + 1, :], m_ref[scale_row:scale_row + 1, :])
    o_ref[...] = _dot(h, w_ref[...]).astype(o_ref.dtype)


def _proj(x, g, mod, w, rows_per_batch, mod_base, mod_stride, shift_row, tm, out_dtype=F32):
    n, d = x.shape
    nout = w.shape[1]
    tpb = rows_per_batch // tm
    return pl.pallas_call(
        functools.partial(_proj_kernel, shift_row=shift_row, scale_row=shift_row + 1),
        grid=(n // tm,),
        in_specs=[pl.BlockSpec((tm, d), lambda i: (i, 0)),
                  pl.BlockSpec((1, d), lambda i: (0, 0)),
                  pl.BlockSpec((None, 6, d), lambda i: (mod_base + (i // tpb) * mod_stride, 0, 0)),
                  pl.BlockSpec((d, nout), lambda i: (0, 0))],
        out_specs=pl.BlockSpec((tm, nout), lambda i: (i, 0)),
        out_shape=jax.ShapeDtypeStruct((n, nout), out_dtype),
        compiler_params=_cparams("parallel"),
        name="mod_proj",
    )(x, g.reshape(1, d), mod, w)


def _wkv_chunk_kernel(r_ref, v_ref, an_ref, lw_ref, kd_ref, bb_ref, ry_ref, pq_ref):
    d = pl.program_id(0)
    lw = lw_ref[...]
    width = lw.shape[1]
    row = lax.broadcasted_iota(jnp.int32, (CHUNK, CHUNK), 0)
    col = lax.broadcasted_iota(jnp.int32, (CHUNK, CHUNK), 1)
    sgn = 1 - 2 * d
    tri = jnp.where((row - col) * sgn >= 0, 1.0, 0.0).astype(BF16)
    hi, mid, lo = _split3(lw)
    csum = (jnp.dot(tri, hi, preferred_element_type=F32)
            + jnp.dot(tri, mid, preferred_element_type=F32)
            + jnp.dot(tri, lo, preferred_element_type=F32))
    total = jnp.sum(lw, axis=0, keepdims=True)
    e_pos = jnp.exp(csum)
    e_neg = jnp.exp(-csum)
    r_hat = r_ref[...] * e_pos
    k_hat = kd_ref[...] * e_neg
    b_hat = bb_ref[...] * e_neg
    a_hat = an_ref[...] * jnp.exp(csum - lw)
    g_end = jnp.exp(total)
    v = v_ref[...]

    lane = lax.broadcasted_iota(jnp.int32, (1, PAIR), 1)
    head0 = lane < HEAD_DIM
    i2 = lax.broadcasted_iota(jnp.int32, (PAIR, PAIR), 0)
    j2 = lax.broadcasted_iota(jnp.int32, (PAIR, PAIR), 1)
    ti = jnp.bitwise_and(i2, CHUNK - 1)
    tj = jnp.bitwise_and(j2, CHUNK - 1)
    strict = (ti - tj) * sgn > 0
    incl = (ti - tj) * sgn >= 0
    eye = (i2 == j2).astype(F32)

    def stack(x):
        return jnp.concatenate([jnp.where(head0, x, 0.0), jnp.where(head0, 0.0, x)], axis=0)

    pairs = range(width // PAIR)
    sls = [slice(p * PAIR, (p + 1) * PAIR) for p in pairs]
    a_s, b_s, k_s, r_s, v_s = ([stack(t[:, sl]) for sl in sls] for t in (a_hat, b_hat, k_hat, r_hat, v))
    m1 = [_dot_nt(jnp.concatenate([a_s[p], r_s[p]], axis=0), jnp.concatenate([b_s[p], k_s[p]], axis=0))
          for p in pairs]
    npow = [jnp.where(strict, m[:PAIR, :PAIR], 0.0) for m in m1]
    a_ak = [jnp.where(strict, m[:PAIR, PAIR:], 0.0) for m in m1]
    a_rb = [jnp.where(incl, m[PAIR:, :PAIR], 0.0) for m in m1]
    a_rk = [jnp.where(incl, m[PAIR:, PAIR:], 0.0) for m in m1]
    x = [jnp.concatenate([a_s[p], _dot(a_ak[p], v_s[p])], axis=1) for p in pairs]
    steps = int(math.log2(CHUNK))
    for it in range(steps):
        x = [x[p] + _dot(npow[p], x[p]) for p in pairs]
        if it + 1 < steps:
            npow = [_dot(npow[p], npow[p]) for p in pairs]
    for p in pairs:
        ry_ref[p] = jnp.concatenate([r_s[p], _dot(a_rk[p], v_s[p])], axis=1) + _dot(a_rb[p], x[p])
    for p in pairs:
        ge = g_end[:, sls[p]]
        pt = (eye + _dot_tn(x[p][:, :PAIR], b_s[p])) * ge
        qt = (_dot_tn(x[p][:, PAIR:], b_s[p]) + _dot_tn(v_s[p], k_s[p])) * ge
        pq_ref[p] = jnp.concatenate([pt, qt], axis=0)


def _wkv_chunks(r, v, an, lw, kd, bb, batch, t):
    n, w = r.shape
    nc = t // CHUNK
    npair = w // PAIR
    shared = pl.BlockSpec((CHUNK, w), lambda d, b, c: (b * nc + c, 0))
    perdir = pl.BlockSpec((None, CHUNK, w), lambda d, b, c: (d, b * nc + c, 0))
    return pl.pallas_call(
        _wkv_chunk_kernel,
        grid=(2, batch, nc),
        in_specs=[shared, shared, shared, perdir, perdir, perdir],
        out_specs=[pl.BlockSpec((None, None, None, npair, PAIR, 2 * PAIR), lambda d, b, c: (d, b, c, 0, 0, 0)),
                   pl.BlockSpec((None, None, None, npair, 2 * PAIR, PAIR), lambda d, b, c: (d, b, c, 0, 0, 0))],
        out_shape=[jax.ShapeDtypeStruct((2, batch, nc, npair, PAIR, 2 * PAIR), F32),
                   jax.ShapeDtypeStruct((2, batch, nc, npair, 2 * PAIR, PAIR), F32)],
        compiler_params=_cparams("parallel", "parallel", "parallel"),
        name="wkv_chunks",
    )(r, v, an, lw, kd, bb)


def _wkv_scan_kernel(ryf_ref, ryb_ref, pqf_ref, pqb_ref, g0_ref, yf_ref, yb_ref, gout_ref, g_ref):
    pos = pl.program_id(0)

    @pl.when(pos == 0)
    def _():
        g_ref[...] = g0_ref[...]

    batch, npair = g_ref.shape[1], g_ref.shape[2]
    for d, (ry_ref, pq_ref, y_ref) in enumerate(((ryf_ref, pqf_ref, yf_ref), (ryb_ref, pqb_ref, yb_ref))):
        for b in range(batch):
            for p in range(npair):
                g = g_ref[d, b, p]
                ry = ry_ref[b, p]
                pq = pq_ref[b, p]
                ys = _dot_nt(ry[:, :PAIR], g) + ry[:, PAIR:]
                y_ref[b, :, p * PAIR:(p + 1) * PAIR] = ys[:CHUNK] + ys[CHUNK:]
                g_ref[d, b, p] = _dot(g, pq[:PAIR]) + pq[PAIR:]

    @pl.when(pos == pl.num_programs(0) - 1)
    def _():
        gout_ref[...] = g_ref[...]


def _wkv_scan(ry, pq, g0):
    _, batch, nc, npair, _, _ = ry.shape
    w = npair * PAIR
    t = nc * CHUNK
    ry_f = pl.BlockSpec((None, batch, None, npair, PAIR, 2 * PAIR), lambda s: (0, 0, s, 0, 0, 0))
    ry_b = pl.BlockSpec((None, batch, None, npair, PAIR, 2 * PAIR), lambda s: (1, 0, nc - 1 - s, 0, 0, 0))
    pq_f = pl.BlockSpec((None, batch, None, npair, 2 * PAIR, PAIR), lambda s: (0, 0, s, 0, 0, 0))
    pq_b = pl.BlockSpec((None, batch, None, npair, 2 * PAIR, PAIR), lambda s: (1, 0, nc - 1 - s, 0, 0, 0))
    gspec = pl.BlockSpec((2, batch, npair, PAIR, PAIR), lambda s: (0, 0, 0, 0, 0))
    return pl.pallas_call(
        _wkv_scan_kernel,
        grid=(nc,),
        in_specs=[ry_f, ry_b, pq_f, pq_b, gspec],
        out_specs=[pl.BlockSpec((batch, CHUNK, w), lambda s: (0, s, 0)),
                   pl.BlockSpec((batch, CHUNK, w), lambda s: (0, nc - 1 - s, 0)),
                   gspec],
        out_shape=[jax.ShapeDtypeStruct((batch, t, w), F32),
                   jax.ShapeDtypeStruct((batch, t, w), F32),
                   jax.ShapeDtypeStruct((2, batch, npair, PAIR, PAIR), F32)],
        scratch_shapes=[pltpu.VMEM((2, batch, npair, PAIR, PAIR), F32)],
        compiler_params=_cparams("arbitrary"),
        name="wkv_scan",
    )(ry, ry, pq, pq, g0)


GRID_SHIFT = ((-1, "first_col"), (1, "last_col"), (-GRID_W, None), (GRID_W, None))
SEQ_SHIFT = ((-1, None), (1, None))


def _prep_kernel(xp_ref, xm_ref, xn_ref, pp_ref, pm_ref, pn_ref, g_ref, m_ref,
                 mux_ref, mup_ref, dw0_ref, dw1_ref, dw2_ref, la0_ref, la1_ref, la2_ref,
                 gg1_ref, gg2_ref, kk_ref, ka_ref, rk_ref, pw_ref, ps_ref, seg_ref,
                 r_out, v_out, an_out, lw_out, kd_out, bb_out, bonus_out, gate_out, yb_out,
                 hext_ref, pext_ref, *, parts, tm, pad, tiles_per_batch, seq_len):
    i = pl.program_id(0)
    tile = lax.rem(i, tiles_per_batch)
    keep_prev = jnp.where(tile == 0, 0.0, 1.0)
    keep_next = jnp.where(tile == tiles_per_batch - 1, 0.0, 1.0)
    g = g_ref[...]
    shift, scale = m_ref[0:1, :], m_ref[1:2, :]
    d_model = xm_ref.shape[1]
    wa = r_out.shape[1]

    hext_ref[0:pad, :] = _modulate(xp_ref[...], g, shift, scale) * keep_prev
    hext_ref[pad:pad + tm, :] = _modulate(xm_ref[...], g, shift, scale)
    hext_ref[pad + tm:, :] = _modulate(xn_ref[...], g, shift, scale) * keep_next
    pext_ref[0:pad, :] = pp_ref[...] * keep_prev
    pext_ref[pad:pad + tm, :] = pm_ref[...]
    pext_ref[pad + tm:, :] = pn_ref[...] * keep_next

    colidx = jnp.bitwise_and(lax.broadcasted_iota(jnp.int32, (tm, 1), 0), GRID_W - 1)

    def shifted(ext_ref, col0, width):
        pw = width // len(parts)
        outs = []
        for q, (off, mask) in enumerate(parts):
            blk = ext_ref[pad + off:pad + off + tm, col0 + q * pw:col0 + (q + 1) * pw]
            if mask == "first_col":
                blk = jnp.where(colidx == 0, 0.0, blk)
            elif mask == "last_col":
                blk = jnp.where(colidx == GRID_W - 1, 0.0, blk)
            outs.append(blk)
        return jnp.concatenate(outs, axis=1)

    h = hext_ref[pad:pad + tm, :]
    hx = shifted(hext_ref, 0, d_model) - h
    x_w = h + hx * mux_ref[0:1, :]
    x_a = h + hx * mux_ref[1:2, :]
    x_g = h + hx * mux_ref[2:3, :]
    zw_mid = jnp.tanh(_dot(x_w, dw1_ref[...]))
    xa_mid = _dot(x_a, la1_ref[...])
    gate_out[...] = _dot(_sigmoid(_dot(x_g, gg1_ref[...])), gg2_ref[...])

    def mixed(n):
        p_n = pext_ref[pad:pad + tm, n * wa:(n + 1) * wa]
        return p_n + (shifted(pext_ref, n * wa, wa) - p_n) * mup_ref[n:n + 1, :]

    r, k, v = mixed(0), mixed(1), mixed(2)
    seg = seg_ref[...]
    kk = k * kk_ref[...]
    kk = kk * lax.rsqrt(jnp.maximum(_dot(kk * kk, seg), 1e-12))
    r_out[...] = r
    v_out[...] = v
    an_out[...] = -kk
    ka = ka_ref[...]
    kd_sum = jnp.zeros_like(k)
    for d in range(2):
        zw = dw0_ref[d:d + 1, :] + _dot(zw_mid, dw2_ref[d])
        lw_out[d] = -math.exp(-0.5) * _sigmoid(zw)
        a_lr = _sigmoid(la0_ref[d:d + 1, :] + _dot(xa_mid, la2_ref[d]))
        kd = k * (1.0 + (a_lr - 1.0) * ka)
        kd_out[d] = kd
        bb_out[d] = kk * a_lr
        kd_sum = kd_sum + kd
    bonus_out[...] = _dot(r * kd_sum * rk_ref[...], seg) * v

    pos = tile * tm + lax.broadcasted_iota(jnp.int32, (tm, 1), 0)
    trow = lax.broadcasted_iota(jnp.int32, (tm, tm + 2 * pad), 0)
    srow = lax.broadcasted_iota(jnp.int32, (tm, tm + 2 * pad), 1) - pad
    gp = wa // len(POOL_WINDOWS)
    ybs = []
    for gi, win in enumerate(POOL_WINDOWS):
        half = win // 2
        c0 = 3 * wa + gi * gp
        band = jnp.where((srow >= trow - half) & (srow < trow + half), 1.0, 0.0).astype(BF16)
        sums = _dot(band, pext_ref[:, c0:c0 + gp])
        cnt = (jnp.minimum(pos + half, seq_len) - jnp.maximum(pos - half, 0)).astype(F32)
        diff = sums / cnt - pext_ref[pad:pad + tm, c0:c0 + gp]
        ybs.append(_dot(diff, pw_ref[gi]))
    yb_out[...] = jnp.concatenate(ybs, axis=1) * ps_ref[...]


def _rwkv_prep(x, proj, g, mod, mod_base, mod_stride, wts, batch, t, grid_mode):
    n, d = x.shape
    wa = d // 2
    pcols = proj.shape[1]
    tm = 256
    pad = GRID_W if grid_mode else 8
    parts = GRID_SHIFT if grid_mode else SEQ_SHIFT
    tpb = t // tm
    hb = tm // pad
    nhb = n // pad
    main = lambda i: (i, 0)
    prev = lambda i: (jnp.maximum(i * hb - 1, 0), 0)
    nxt = lambda i: (jnp.minimum((i + 1) * hb, nhb - 1), 0)
    full2 = lambda i: (0, 0)
    full3 = lambda i: (0, 0, 0)
    in_specs = [pl.BlockSpec((pad, d), prev), pl.BlockSpec((tm, d), main), pl.BlockSpec((pad, d), nxt),
                pl.BlockSpec((pad, pcols), prev), pl.BlockSpec((tm, pcols), main), pl.BlockSpec((pad, pcols), nxt),
                pl.BlockSpec((1, d), full2),
                pl.BlockSpec((None, 6, d), lambda i: (mod_base + (i // tpb) * mod_stride, 0, 0))]
    for a in wts:
        in_specs.append(pl.BlockSpec(a.shape, full2 if a.ndim == 2 else full3))
    one = pl.BlockSpec((tm, wa), main)
    two = pl.BlockSpec((2, tm, wa), lambda i: (0, i, 0))
    sd1 = jax.ShapeDtypeStruct((n, wa), F32)
    sd2 = jax.ShapeDtypeStruct((2, n, wa), F32)
    return pl.pallas_call(
        functools.partial(_prep_kernel, parts=parts, tm=tm, pad=pad, tiles_per_batch=tpb, seq_len=t),
        grid=(n // tm,),
        in_specs=in_specs,
        out_specs=[one, one, one, two, two, two, one, one, one],
        out_shape=[sd1, sd1, sd1, sd2, sd2, sd2, sd1, sd1, sd1],
        scratch_shapes=[pltpu.VMEM((tm + 2 * pad, d), F32), pltpu.VMEM((tm + 2 * pad, pcols), F32)],
        compiler_params=_cparams("parallel"),
        name="rwkv_prep",
    )(x, x, x, proj, proj, proj, g.reshape(1, d), mod, *wts)


def _mix_out_kernel(y0_ref, y1_ref, bonus_ref, gate_ref, yb_ref, x_ref, m_ref, gnw_ref, gnb_ref,
                    seg_ref, wout_ref, o_ref):
    y = y0_ref[...] + y1_ref[...]
    seg = seg_ref[...]
    inv = 1.0 / HEAD_DIM
    yh = y.astype(BF16)
    mu = (jnp.dot(yh, seg, preferred_element_type=F32) + _dot(y - yh.astype(F32), seg)) * inv
    dlt = y - mu
    var = _dot(dlt * dlt, seg) * inv
    yn = dlt * lax.rsqrt(var + GN_EPS) * gnw_ref[...] + gnb_ref[...]
    ya = (yn + bonus_ref[...]) * gate_ref[...]
    cat = jnp.concatenate([ya, yb_ref[...]], axis=1)
    o_ref[...] = x_ref[...] + m_ref[2:3, :] * _dot(cat, wout_ref[...])


def _mix_out(y0, y1, bonus, gate, yb, x, mod, mod_base, mod_stride, gnw, gnb, seg, wout, t, tm):
    n, d = x.shape
    wa = d // 2
    tpb = t // tm
    half = pl.BlockSpec((tm, wa), lambda i: (i, 0))
    full = pl.BlockSpec((tm, d), lambda i: (i, 0))
    const = lambda a: pl.BlockSpec(a.shape, lambda i: (0, 0))
    return pl.pallas_call(
        _mix_out_kernel,
        grid=(n // tm,),
        in_specs=[half, half, half, half, half, full,
                  pl.BlockSpec((None, 6, d), lambda i: (mod_base + (i // tpb) * mod_stride, 0, 0)),
                  const(gnw), const(gnb), const(seg), const(wout)],
        out_specs=full,
        out_shape=jax.ShapeDtypeStruct((n, d), F32),
        input_output_aliases={5: 0},
        compiler_params=_cparams("parallel"),
        name="mix_out",
    )(y0, y1, bonus, gate, yb, x, mod, gnw, gnb, seg, wout)


def _chan_dft_kernel(x_ref, g_ref, m_ref, cs_ref, xc_ref, xs_ref, *, groups):
    h = _modulate(x_ref[...], g_ref[...], m_ref[0:1, :], m_ref[1:2, :])
    gc = h.shape[1] // groups
    cs = cs_ref[...]
    for gi in range(groups):
        res = _dot(h[:, gi * gc:(gi + 1) * gc], cs)
        xc_ref[:, gi * gc:(gi + 1) * gc] = res[:, :gc].astype(xc_ref.dtype)
        xs_ref[:, gi * gc:(gi + 1) * gc] = res[:, gc:].astype(xs_ref.dtype)


def _chan_dft(x, g, mod, mod_base, mod_stride, cs, t, tm, groups):
    n, d = x.shape
    tpb = t // tm
    full = pl.BlockSpec((tm, d), lambda i: (i, 0))
    return pl.pallas_call(
        functools.partial(_chan_dft_kernel, groups=groups),
        grid=(n // tm,),
        in_specs=[full, pl.BlockSpec((1, d), lambda i: (0, 0)),
                  pl.BlockSpec((None, 6, d), lambda i: (mod_base + (i // tpb) * mod_stride, 0, 0)),
                  pl.BlockSpec(cs.shape, lambda i: (0, 0))],
        out_specs=[full, full],
        out_shape=[jax.ShapeDtypeStruct((n, d), BF16)] * 2,
        compiler_params=_cparams("parallel"),
        name="chan_dft",
    )(x, g.reshape(1, d), mod, cs)


def _time_dft_kernel(ct_ref, st_ref, xc_ref, xs_ref, x_ref, m_ref, wf_ref, o_ref, acc_ref):
    ki = pl.program_id(2)

    @pl.when(ki == 0)
    def _():
        acc_ref[...] = jnp.zeros_like(acc_ref)

    acc_ref[...] += (jnp.dot(ct_ref[...], xc_ref[...], preferred_element_type=F32)
                     - jnp.dot(st_ref[...], xs_ref[...], preferred_element_type=F32))

    @pl.when(ki == pl.num_programs(2) - 1)
    def _():
        o_ref[...] = x_ref[...] + m_ref[2:3, :] * _dot(acc_ref[...], wf_ref[...])


def _time_dft(ct, st, xc, xs, x, mod, mod_base, mod_stride, wf, batch, t, tf, tk):
    n, d = x.shape
    nf, nk = t // tf, t // tk
    return pl.pallas_call(
        _time_dft_kernel,
        grid=(batch, nf, nk),
        in_specs=[pl.BlockSpec((tf, tk), lambda b, f, k: (f, k)),
                  pl.BlockSpec((tf, tk), lambda b, f, k: (f, k)),
                  pl.BlockSpec((tk, d), lambda b, f, k: (b * nk + k, 0)),
                  pl.BlockSpec((tk, d), lambda b, f, k: (b * nk + k, 0)),
                  pl.BlockSpec((tf, d), lambda b, f, k: (b * nf + f, 0)),
                  pl.BlockSpec((None, 6, d), lambda b, f, k: (mod_base + b * mod_stride, 0, 0)),
                  pl.BlockSpec((d, d), lambda b, f, k: (0, 0))],
        out_specs=pl.BlockSpec((tf, d), lambda b, f, k: (b * nf + f, 0)),
        out_shape=jax.ShapeDtypeStruct((n, d), F32),
        scratch_shapes=[pltpu.VMEM((tf, d), F32)],
        input_output_aliases={4: 0},
        compiler_params=_cparams("parallel", "parallel", "arbitrary"),
        name="time_dft",
    )(ct, st, xc, xs, x, mod, wf)


def _dft_tables(n, scale):
    def small(rows, cols, period):
        prod = jnp.bitwise_and(jnp.arange(rows, dtype=jnp.int32)[:, None] * jnp.arange(cols, dtype=jnp.int32)[None, :],
                               period - 1)
        ang = prod.astype(F32) * (2.0 * math.pi / period)
        return jnp.cos(ang), jnp.sin(ang)

    if n <= 256:
        c, s = small(n, n, n)
        return c * scale, s * scale
    lo_n = 128
    hi_n = n // lo_n
    ca, sa = small(n, hi_n, hi_n)
    cb, sb = small(n, lo_n, n)
    ca, sa = ca[:, :, None] * scale, sa[:, :, None] * scale
    cb, sb = cb[:, None, :], sb[:, None, :]
    return (ca * cb - sa * sb).reshape(n, n), (sa * cb + ca * sb).reshape(n, n)


def _router_kernel(x_ref, g_ref, m_ref, wr_ref, br_ref, tri_ref, ids_out, gates_out, cnt_out, base_ref):
    i = pl.program_id(0)

    @pl.when(i == 0)
    def _():
        base_ref[...] = jnp.zeros_like(base_ref)

    h2 = _modulate(x_ref[...], g_ref[...], m_ref[3:4, :], m_ref[4:5, :])
    hs = _split3(h2)
    logits = br_ref[...]
    for ia, ib in ((0, 0), (0, 1), (1, 0), (0, 2), (2, 0), (1, 1)):
        logits = logits + jnp.dot(hs[ia], wr_ref[ib], preferred_element_type=F32)

    tm = logits.shape[0]
    lane_i = lax.broadcasted_iota(jnp.int32, (tm, 128), 1)
    lane = lane_i.astype(F32)
    neg = -jnp.inf
    gmask = lane_i < MOE_GROUPS
    lc = jnp.where(gmask, logits, neg)
    mc = jnp.max(lc, axis=1, keepdims=True)
    sc = jnp.sum(jnp.where(gmask, jnp.exp(logits - mc), 0.0), axis=1, keepdims=True)
    g_val = 1.0 / sc
    g_idx = jnp.min(jnp.where(lc == mc, lane, 128.0), axis=1, keepdims=True)
    lgroup = jnp.where(lane_i >= MOE_GROUPS, jnp.right_shift(lane_i - MOE_GROUPS, 3), -1).astype(F32)
    sel = lgroup == g_idx
    lf = jnp.where(sel, logits, neg)
    m1 = jnp.max(lf, axis=1, keepdims=True)
    ef = jnp.where(sel, jnp.exp(logits - m1), 0.0)
    p = ef / jnp.sum(ef, axis=1, keepdims=True)
    p1 = jnp.where(sel, p, -1.0)
    v1 = jnp.max(p1, axis=1, keepdims=True)
    i1 = jnp.min(jnp.where(p1 == v1, lane, 128.0), axis=1, keepdims=True)
    p2 = jnp.where(lane == i1, -1.0, p1)
    v2 = jnp.max(p2, axis=1, keepdims=True)
    i2 = jnp.min(jnp.where(p2 == v2, lane, 128.0), axis=1, keepdims=True)
    denom = v1 + v2
    gate1 = g_val * v1 / denom
    gate2 = g_val * v2 / denom
    e1 = i1 - MOE_GROUPS
    e2 = i2 - MOE_GROUPS

    tri = tri_ref[...]
    oh1 = jnp.where(lane == e1, 1.0, 0.0)
    oh2 = jnp.where(lane == e2, 1.0, 0.0)
    base = base_ref[...]
    tot1 = jnp.sum(oh1, axis=0, keepdims=True)
    tot2 = jnp.sum(oh2, axis=0, keepdims=True)
    c1 = jnp.dot(tri, oh1.astype(BF16), preferred_element_type=F32)
    c2 = jnp.dot(tri, oh2.astype(BF16), preferred_element_type=F32)
    rank1 = jnp.sum(oh1 * (base + c1), axis=1, keepdims=True)
    rank2 = jnp.sum(oh2 * (base + tot1 + c2), axis=1, keepdims=True)
    base = base + tot1 + tot2
    base_ref[...] = base
    cnt_out[...] = base
    ids_out[...] = jnp.where(lane_i == 0, e1, jnp.where(lane_i == 1, e2, jnp.where(
        lane_i == 2, rank1, jnp.where(lane_i == 3, rank2, 0.0)))).astype(jnp.int32)
    gates_out[...] = jnp.where(lane_i == 0, gate1, jnp.where(lane_i == 1, gate2, 0.0))


def _router(x, g, mod, mod_base, mod_stride, wr3, br, tri, t, tm):
    n, d = x.shape
    tpb = t // tm
    full = pl.BlockSpec((tm, d), lambda i: (i, 0))
    lanes = pl.BlockSpec((tm, 128), lambda i: (i, 0))
    return pl.pallas_call(
        _router_kernel,
        grid=(n // tm,),
        in_specs=[full, pl.BlockSpec((1, d), lambda i: (0, 0)),
                  pl.BlockSpec((None, 6, d), lambda i: (mod_base + (i // tpb) * mod_stride, 0, 0)),
                  pl.BlockSpec(wr3.shape, lambda i: (0, 0, 0)),
                  pl.BlockSpec((1, 128), lambda i: (0, 0)),
                  pl.BlockSpec((tm, tm), lambda i: (0, 0))],
        out_specs=[lanes, lanes, pl.BlockSpec((1, 128), lambda i: (0, 0))],
        out_shape=[jax.ShapeDtypeStruct((n, 128), jnp.int32),
                   jax.ShapeDtypeStruct((n, 128), F32), jax.ShapeDtypeStruct((1, 128), F32)],
        scratch_shapes=[pltpu.VMEM((1, 128), F32)],
        compiler_params=_cparams("arbitrary"),
        name="moe_router",
    )(x, g.reshape(1, d), mod, wr3, br, tri)


def _row_copy(src, src_row, dst, dst_row, sem):
    return pltpu.make_async_copy(src.at[pl.ds(src_row, 1)], dst.at[pl.ds(dst_row, 1)], sem)


def _dispatch_kernel(dest_ref, zb_ref, x_ref, g_ref, m_ref, xbuf_ref, h_ref, zero_ref, sem, zsem, *, tm):
    step = pl.program_id(0)

    def zero_copy(s):
        start = pl.multiple_of(zb_ref[s] * MOE_BLOCK, MOE_BLOCK)
        return pltpu.make_async_copy(zero_ref, xbuf_ref.at[pl.ds(start, MOE_BLOCK)], zsem)

    @pl.when(step == 0)
    def _():
        zero_ref[...] = jnp.zeros_like(zero_ref)
        for s in range(zb_ref.shape[0]):
            @pl.when(zb_ref[s] >= 0)
            def _():
                zero_copy(s).start()
        for s in range(zb_ref.shape[0]):
            @pl.when(zb_ref[s] >= 0)
            def _():
                zero_copy(s).wait()

    h_ref[...] = _modulate(x_ref[...], g_ref[...], m_ref[3:4, :], m_ref[4:5, :])
    base = step * tm

    def body(r, carry):
        tok = base + r
        for j in range(2):
            _row_copy(h_ref, r, xbuf_ref, dest_ref[2 * tok + j], sem).start()
        return carry

    lax.fori_loop(0, tm, body, 0, unroll=8)

    def drain(r, carry):
        for j in range(2):
            _row_copy(h_ref, 0, xbuf_ref, 0, sem).wait()
        return carry

    lax.fori_loop(0, tm, drain, 0, unroll=8)


def _dispatch(dest, zblocks, x, g, mod, mod_base, mod_stride, rows, t, tm):
    n, d = x.shape
    tpb = t // tm
    return pl.pallas_call(
        functools.partial(_dispatch_kernel, tm=tm),
        grid_spec=pltpu.PrefetchScalarGridSpec(
            num_scalar_prefetch=2,
            grid=(n // tm,),
            in_specs=[pl.BlockSpec((tm, d), lambda i, dr, zb: (i, 0)),
                      pl.BlockSpec((1, d), lambda i, dr, zb: (0, 0)),
                      pl.BlockSpec((None, 6, d), lambda i, dr, zb: (mod_base + (i // tpb) * mod_stride, 0, 0))],
            out_specs=pl.BlockSpec(memory_space=pl.ANY),
            scratch_shapes=[pltpu.VMEM((tm, d), F32), pltpu.VMEM((MOE_BLOCK, d), F32),
                            pltpu.SemaphoreType.DMA, pltpu.SemaphoreType.DMA],
        ),
        out_shape=jax.ShapeDtypeStruct((rows, d), F32),
        compiler_params=_cparams("arbitrary"),
        name="moe_dispatch",
    )(dest, zblocks, x, g.reshape(1, d), mod)


def _expert_kernel(be_ref, na_ref, x_ref, w1_ref, w3_ref, w2_ref, o_ref):
    active = pl.program_id(0) < na_ref[0]

    @pl.when(active)
    def _():
        xb = x_ref[...].astype(BF16)
        h1 = jnp.dot(xb, w1_ref[...].astype(BF16), preferred_element_type=F32)
        h3 = jnp.dot(xb, w3_ref[...].astype(BF16), preferred_element_type=F32)
        act = h1 * _sigmoid(h1) * h3
        o_ref[...] = _dot(act, w2_ref[...])

    @pl.when(jnp.logical_not(active))
    def _():
        o_ref[...] = jnp.zeros_like(o_ref)


def _experts(block_e, nact, xbuf, w1, w3, w2, layer):
    rows, d = xbuf.shape
    nb = rows // MOE_BLOCK
    de = w1.shape[3]
    blk = lambda i, be, na: (jnp.minimum(i, na[0] - 1), 0)
    wmap = lambda i, be, na: (layer, be[jnp.minimum(i, na[0] - 1)], 0, 0)
    return pl.pallas_call(
        _expert_kernel,
        grid_spec=pltpu.PrefetchScalarGridSpec(
            num_scalar_prefetch=2,
            grid=(nb,),
            in_specs=[pl.BlockSpec((MOE_BLOCK, d), blk),
                      pl.BlockSpec((None, None, d, de), wmap),
                      pl.BlockSpec((None, None, d, de), wmap),
                      pl.BlockSpec((None, None, de, d), wmap)],
            out_specs=pl.BlockSpec((MOE_BLOCK, d), lambda i, be, na: (i, 0)),
        ),
        out_shape=jax.ShapeDtypeStruct((rows, d), F32),
        compiler_params=_cparams("arbitrary"),
        name="moe_experts",
    )(block_e, nact, xbuf, w1, w3, w2)


def _combine_kernel(dest_ref, y_ref, gates_ref, x_ref, m_ref, fn_ref, o_ref, rows_ref, sem, *, tm, final):
    base = pl.program_id(0) * tm

    def body(r, carry):
        tok = base + r
        for j in range(2):
            _row_copy(y_ref, dest_ref[2 * tok + j], rows_ref.at[j], r, sem).start()
        return carry

    lax.fori_loop(0, tm, body, 0, unroll=8)

    def drain(r, carry):
        for j in range(2):
            _row_copy(y_ref, 0, rows_ref.at[j], 0, sem).wait()
        return carry

    lax.fori_loop(0, tm, drain, 0, unroll=8)

    gates = gates_ref[...]
    y = gates[:, 0:1] * rows_ref[0] + gates[:, 1:2] * rows_ref[1]
    out = x_ref[...] + m_ref[5:6, :] * y
    if final:
        ms = jnp.mean(out * out, axis=-1, keepdims=True)
        out = out * lax.rsqrt(ms + NORM_EPS) * fn_ref[...]
    o_ref[...] = out


def _combine(dest, ybuf, gates, x, mod, mod_base, mod_stride, fnorm, t, tm, final):
    n, d = x.shape
    tpb = t // tm
    full = pl.BlockSpec((tm, d), lambda i, dr: (i, 0))
    return pl.pallas_call(
        functools.partial(_combine_kernel, tm=tm, final=final),
        grid_spec=pltpu.PrefetchScalarGridSpec(
            num_scalar_prefetch=1,
            grid=(n // tm,),
            in_specs=[pl.BlockSpec(memory_space=pl.ANY),
                      pl.BlockSpec((tm, 128), lambda i, dr: (i, 0)),
                      full,
                      pl.BlockSpec((None, 6, d), lambda i, dr: (mod_base + (i // tpb) * mod_stride, 0, 0)),
                      pl.BlockSpec((1, d), lambda i, dr: (0, 0))],
            out_specs=full,
            scratch_shapes=[pltpu.VMEM((2, tm, d), F32), pltpu.SemaphoreType.DMA],
        ),
        out_shape=jax.ShapeDtypeStruct((n, d), F32),
        input_output_aliases={3: 0},
        compiler_params=_cparams("arbitrary"),
        name="moe_combine",
    )(dest, ybuf, gates, x, mod, fnorm)


def _moe(x, g, mod, mod_base, mod_stride, rt, w1, w3, w2, layer, fnorm, t, tm, final):
    n, d = x.shape
    n_exp = w1.shape[1]
    wr3, br, tri = rt
    ids, gates, counts = _router(x, g, mod, mod_base, mod_stride, wr3, br, tri, t, tm)
    counts = counts[0, :n_exp].astype(jnp.int32)
    padded = (counts + MOE_BLOCK - 1) // MOE_BLOCK * MOE_BLOCK
    pend = jnp.cumsum(padded)
    pstart = pend - padded
    dest = (pstart[ids[:, 0:2]] + ids[:, 2:4]).reshape(2 * n)
    nb = (2 * n) // MOE_BLOCK + n_exp
    blk_start = jnp.arange(nb, dtype=jnp.int32) * MOE_BLOCK
    block_e = jnp.minimum(jnp.sum((pend[None, :] <= blk_start[:, None]).astype(jnp.int32), axis=1), n_exp - 1)
    nact = pend[-1:] // MOE_BLOCK
    partial = jnp.where(padded > counts, pend // MOE_BLOCK - 1, -1)
    tail = nact + jnp.arange(n_exp, dtype=jnp.int32)
    zblocks = jnp.concatenate([partial, jnp.where(tail < nb, tail, -1)]).astype(jnp.int32)
    xbuf = _dispatch(dest, zblocks, x, g, mod, mod_base, mod_stride, nb * MOE_BLOCK, t, tm)
    ybuf = _experts(block_e, nact, xbuf, w1, w3, w2, layer)
    return _combine(dest, ybuf, gates, x, mod, mod_base, mod_stride, fnorm, t, tm, final)


def _seg_ones(width):
    idx = np.arange(width) // HEAD_DIM
    return jnp.asarray((idx[:, None] == idx[None, :]).astype(np.float32), dtype=BF16)


def _even_weights(j, mu_x, mu_p, decay_w0, decay_w1, decay_w2, lr_a0, lr_a1, lr_a2, gate_g1, gate_g2,
                  k_k, k_a, r_k, pool_w, pool_scale, seg):
    wa = mu_p.shape[-1]

    def cat1(w):
        return jnp.concatenate([w[0], w[1]], axis=1).astype(BF16)

    def pad2(w):
        z = jnp.zeros_like(w[0])
        return jnp.stack([jnp.concatenate([w[0], z], axis=0), jnp.concatenate([z, w[1]], axis=0)]).astype(BF16)

    return (mu_x[j], mu_p[j], decay_w0[j], cat1(decay_w1[j]), pad2(decay_w2[j]),
            lr_a0[j], cat1(lr_a1[j]), pad2(lr_a2[j]),
            gate_g1[j].astype(BF16), gate_g2[j].astype(BF16),
            k_k[j].reshape(1, wa), k_a[j].reshape(1, wa), r_k[j].reshape(1, wa),
            pool_w[j].astype(BF16), pool_scale[j].reshape(1, wa), seg)


def kernel(x, c, ctx, c_ctx, ada_w, ada_b, norm_mix, norm_ffn, w_in, mu_x, mu_p, decay_w0, decay_w1, decay_w2, lr_a0, lr_a1, lr_a2, gate_g1, gate_g2, k_k, k_a, r_k, gn_w, gn_b, pool_w, pool_scale, w_out, w_fourier, router_c, router_c_b, router_f, router_f_b, moe_w1, moe_w3, moe_w2, final_norm):
    batch, t, d = x.shape
    tc = ctx.shape[1]
    depth = ada_w.shape[0]
    wa = d // 2
    n, ncx = batch * t, batch * tc
    assert batch <= 4 and t % 512 == 0 and tc % 256 == 0 and d % 512 == 0

    cond8 = jnp.zeros((8, d), F32).at[:batch].set(c).at[4].set(c_ctx)
    mod_all = _ada(cond8, ada_w, ada_b).reshape(depth, 8, 6, d)
    seg = _seg_ones(wa)
    fnorm = final_norm.reshape(1, d)

    fgroups = 4
    gc = d // fgroups
    cc, sc = _dft_tables(gc, gc ** -0.5)
    cs = jnp.concatenate([cc, sc], axis=1).astype(BF16)
    ct_lat, st_lat = (a.astype(BF16) for a in _dft_tables(t, t ** -0.5))
    ct_ctx, st_ctx = (a.astype(BF16) for a in _dft_tables(tc, tc ** -0.5))

    n_exp = moe_w1.shape[1]
    tri = {tm: jnp.asarray(np.tril(np.ones((tm, tm), np.float32), -1), dtype=BF16) for tm in (512, 256)}

    lat = x.reshape(n, d)
    cx = ctx.reshape(ncx, d)
    last_read = 2 * ((depth - 1) // 2)
    npair = wa // PAIR

    for i in range(depth):
        ctx_in = i <= last_read
        ctx_out = i < last_read
        mod = mod_all[i]
        j = i // 2
        if i % 2 == 0:
            wts = _even_weights(j, mu_x, mu_p, decay_w0, decay_w1, decay_w2, lr_a0, lr_a1, lr_a2,
                                gate_g1, gate_g2, k_k, k_a, r_k, pool_w, pool_scale, seg)
            w_in_b = w_in[j].astype(BF16)
            w_out_b = w_out[j].astype(BF16)
            gnw, gnb = gn_w[j].reshape(1, wa), gn_b[j].reshape(1, wa)
            pc = _proj(cx, norm_mix[i], mod, w_in_b, tc, 4, 0, 0, 256)
            fc = _rwkv_prep(cx, pc, norm_mix[i], mod, 4, 0, wts, batch, tc, False)
            ryc, pqc = _wkv_chunks(*fc[:6], batch, tc)
            g0 = jnp.zeros((2, batch, npair, PAIR, PAIR), F32)
            yc0, yc1, gctx = _wkv_scan(ryc, pqc, g0)
            pl_ = _proj(lat, norm_mix[i], mod, w_in_b, t, 0, 1, 0, 512)
            fl = _rwkv_prep(lat, pl_, norm_mix[i], mod, 0, 1, wts, batch, t, True)
            ryl, pql = _wkv_chunks(*fl[:6], batch, t)
            yl0, yl1, _ = _wkv_scan(ryl, pql, gctx)
            lat = _mix_out(yl0.reshape(n, wa), yl1.reshape(n, wa), fl[6], fl[7], fl[8], lat, mod, 0, 1,
                           gnw, gnb, seg, w_out_b, t, 512)
            if ctx_out:
                cx = _mix_out(yc0.reshape(ncx, wa), yc1.reshape(ncx, wa), fc[6], fc[7], fc[8], cx, mod, 4, 0,
                              gnw, gnb, seg, w_out_b, tc, 256)
        else:
            wf_b = w_fourier[j].astype(BF16)
            xc, xs = _chan_dft(lat, norm_mix[i], mod, 0, 1, cs, t, 512, fgroups)
            lat = _time_dft(ct_lat, st_lat, xc, xs, lat, mod, 0, 1, wf_b, batch, t, 1024, 1024)
            if ctx_out:
                xcc, xsc = _chan_dft(cx, norm_mix[i], mod, 4, 0, cs, tc, 256, fgroups)
                cx = _time_dft(ct_ctx, st_ctx, xcc, xsc, cx, mod, 4, 0, wf_b, batch, tc, 256, 256)
        wr = jnp.zeros((d, 128), F32).at[:, :MOE_GROUPS].set(router_c[i]).at[:, MOE_GROUPS:MOE_GROUPS + n_exp].set(router_f[i])
        br = jnp.zeros((1, 128), F32).at[0, :MOE_GROUPS].set(router_c_b[i]).at[0, MOE_GROUPS:MOE_GROUPS + n_exp].set(router_f_b[i])
        wr3 = jnp.stack(_split3(wr))
        final = i == depth - 1
        lat = _moe(lat, norm_ffn[i], mod, 0, 1, (wr3, br, tri[512]), moe_w1, moe_w3, moe_w2, i, fnorm,
                   t, 512, final)
        if ctx_out:
            cx = _moe(cx, norm_ffn[i], mod, 4, 0, (wr3, br, tri[256]), moe_w1, moe_w3, moe_w2, i, fnorm,
                      tc, 256, False)
    return lat.reshape(batch, t, d)
```

```python
import functools
import math

import jax
import jax.numpy as jnp
import numpy as np
from jax import lax
from jax.experimental import pallas as pl
from jax.experimental.pallas import tpu as pltpu

F32 = jnp.float32
BF16 = jnp.bfloat16

GRID_W = 64
HEAD_DIM = 64
CHUNK = 64
PAIR = 2 * HEAD_DIM
NORM_EPS = 1e-6
GN_EPS = 64e-5
POOL_WINDOWS = (2, 4, 8, 16)
MOE_GROUPS = 4
EXPERTS_PER_GROUP = 8
MOE_BLOCK = 512
VMEM_LIMIT = 56 * 1024 * 1024


def _cparams(*sem):
    return pltpu.CompilerParams(dimension_semantics=tuple(sem), vmem_limit_bytes=VMEM_LIMIT)


def _dot(a, b):
    return jnp.dot(a.astype(BF16), b.astype(BF16), preferred_element_type=F32)


def _dot_nt(a, b):
    return lax.dot_general(a.astype(BF16), b.astype(BF16), (((1,), (1,)), ((), ())),
                           preferred_element_type=F32)


def _dot_tn(a, b):
    return lax.dot_general(a.astype(BF16), b.astype(BF16), (((0,), (0,)), ((), ())),
                           preferred_element_type=F32)


def _split3(x):
    hi = x.astype(BF16)
    r1 = x - hi.astype(F32)
    mid = r1.astype(BF16)
    lo = (r1 - mid.astype(F32)).astype(BF16)
    return hi, mid, lo


def _sigmoid(x):
    return 1.0 / (1.0 + jnp.exp(-x))


def _modulate(x, g, shift, scale):
    ms = jnp.mean(x * x, axis=-1, keepdims=True)
    return x * lax.rsqrt(ms + NORM_EPS) * g * (1.0 + scale) + shift


def _ada_kernel(c_ref, w_ref, b_ref, o_ref):
    c = c_ref[...]
    s = c * _sigmoid(c)
    o_ref[...] = _dot(s, w_ref[...]) + b_ref[...]


def _ada(cond8, ada_w, ada_b):
    depth, d, n6 = ada_w.shape
    tn = 1536
    return pl.pallas_call(
        _ada_kernel,
        grid=(depth, n6 // tn),
        in_specs=[pl.BlockSpec((8, d), lambda l, j: (0, 0)),
                  pl.BlockSpec((None, d, tn), lambda l, j: (l, 0, j)),
                  pl.BlockSpec((None, 1, tn), lambda l, j: (l, 0, j))],
        out_specs=pl.BlockSpec((None, 8, tn), lambda l, j: (l, 0, j)),
        out_shape=jax.ShapeDtypeStruct((depth, 8, n6), F32),
        compiler_params=_cparams("parallel", "parallel"),
        name="ada_mod",
    )(cond8, ada_w, ada_b.reshape(depth, 1, n6))


def _proj_kernel(x_ref, g_ref, m_ref, w_ref, o_ref, *, shift_row, scale_row):
    h = _modulate(x_ref[...], g_ref[...], m_ref[shift_row:shift_row + 1, :], m_ref[scale_row:scale_row + 1, :])
    o_ref[...] = _dot(h, w_ref[...]).astype(o_ref.dtype)


def _proj(x, g, mod, w, rows_per_batch, mod_base, mod_stride, shift_row, tm, out_dtype=F32):
    n, d = x.shape
    nout = w.shape[1]
    tpb = rows_per_batch // tm
    return pl.pallas_call(
        functools.partial(_proj_kernel, shift_row=shift_row, scale_row=shift_row + 1),
        grid=(n // tm,),
        in_specs=[pl.BlockSpec((tm, d), lambda i: (i, 0)),
                  pl.BlockSpec((1, d), lambda i: (0, 0)),
                  pl.BlockSpec((None, 6, d), lambda i: (mod_base + (i // tpb) * mod_stride, 0, 0)),
                  pl.BlockSpec((d, nout), lambda i: (0, 0))],
        out_specs=pl.BlockSpec((tm, nout), lambda i: (i, 0)),
        out_shape=jax.ShapeDtypeStruct((n, nout), out_dtype),
        compiler_params=_cparams("parallel"),
        name="mod_proj",
    )(x, g.reshape(1, d), mod, w)


def _wkv_chunk_kernel(r_ref, v_ref, an_ref, lw_ref, kd_ref, bb_ref, ry_ref, pq_ref):
    d = pl.program_id(0)
    width = lw_ref.shape[1]
    nsub = lw_ref.shape[0] // CHUNK
    row = lax.broadcasted_iota(jnp.int32, (CHUNK, CHUNK), 0)
    col = lax.broadcasted_iota(jnp.int32, (CHUNK, CHUNK), 1)
    sgn = 1 - 2 * d
    tri = jnp.where((row - col) * sgn >= 0, 1.0, 0.0).astype(BF16)

    def chunk_feats(sc):
        rows = slice(sc * CHUNK, (sc + 1) * CHUNK)
        lw = lw_ref[rows, :]
        hi, mid, lo = _split3(lw)
        csum = (jnp.dot(tri, hi, preferred_element_type=F32)
                + jnp.dot(tri, mid, preferred_element_type=F32)
                + jnp.dot(tri, lo, preferred_element_type=F32))
        e_neg = jnp.exp(-csum)
        return (an_ref[rows, :] * jnp.exp(csum - lw), bb_ref[rows, :] * e_neg, kd_ref[rows, :] * e_neg,
                r_ref[rows, :] * jnp.exp(csum), v_ref[rows, :], jnp.exp(jnp.sum(lw, axis=0, keepdims=True)))

    feats = [chunk_feats(sc) for sc in range(nsub)]

    lane = lax.broadcasted_iota(jnp.int32, (1, PAIR), 1)
    head0 = lane < HEAD_DIM
    i2 = lax.broadcasted_iota(jnp.int32, (PAIR, PAIR), 0)
    j2 = lax.broadcasted_iota(jnp.int32, (PAIR, PAIR), 1)
    ti = jnp.bitwise_and(i2, CHUNK - 1)
    tj = jnp.bitwise_and(j2, CHUNK - 1)
    strict = (ti - tj) * sgn > 0
    incl = (ti - tj) * sgn >= 0
    eye = (i2 == j2).astype(F32)

    def stack(x):
        return jnp.concatenate([jnp.where(head0, x, 0.0), jnp.where(head0, 0.0, x)], axis=0)

    units = [(sc, p) for sc in range(nsub) for p in range(width // PAIR)]
    us = range(len(units))
    sls = [slice(p * PAIR, (p + 1) * PAIR) for _, p in units]
    a_s, b_s, k_s, r_s, v_s = ([stack(feats[sc][f][:, sls[u]]) for u, (sc, _) in enumerate(units)]
                               for f in range(5))
    m1 = [_dot_nt(jnp.concatenate([a_s[u], r_s[u]], axis=0), jnp.concatenate([b_s[u], k_s[u]], axis=0))
          for u in us]
    npow = [jnp.where(strict, m[:PAIR, :PAIR], 0.0) for m in m1]
    a_ak = [jnp.where(strict, m[:PAIR, PAIR:], 0.0) for m in m1]
    a_rb = [jnp.where(incl, m[PAIR:, :PAIR], 0.0) for m in m1]
    a_rk = [jnp.where(incl, m[PAIR:, PAIR:], 0.0) for m in m1]
    x = [jnp.concatenate([a_s[u], _dot(a_ak[u], v_s[u])], axis=1) for u in us]
    steps = int(math.log2(CHUNK))
    for it in range(steps):
        x = [x[u] + _dot(npow[u], x[u]) for u in us]
        if it + 1 < steps:
            npow = [_dot(npow[u], npow[u]) for u in us]
    for u, (sc, p) in enumerate(units):
        ry = jnp.concatenate([r_s[u], _dot(a_rk[u], v_s[u])], axis=1) + _dot(a_rb[u], x[u])
        ry_ref[sc, p] = ry.astype(ry_ref.dtype)
    for u, (sc, p) in enumerate(units):
        ge = feats[sc][5][:, sls[u]]
        pt = (eye + _dot_tn(x[u][:, :PAIR], b_s[u])) * ge
        qt = (_dot_tn(x[u][:, PAIR:], b_s[u]) + _dot_tn(v_s[u], k_s[u])) * ge
        pq_ref[sc, p] = jnp.concatenate([pt, qt], axis=0).astype(pq_ref.dtype)


def _wkv_chunks(r, v, an, lw, kd, bb, batch, t):
    n, w = r.shape
    nc = t // CHUNK
    npair = w // PAIR
    nsub = 2
    ns = nc // nsub
    shared = pl.BlockSpec((nsub * CHUNK, w), lambda d, b, c: (b * ns + c, 0))
    perdir = pl.BlockSpec((None, nsub * CHUNK, w), lambda d, b, c: (d, b * ns + c, 0))
    return pl.pallas_call(
        _wkv_chunk_kernel,
        grid=(2, batch, ns),
        in_specs=[shared, shared, shared, perdir, perdir, perdir],
        out_specs=[pl.BlockSpec((None, None, nsub, npair, PAIR, 2 * PAIR), lambda d, b, c: (d, b, c, 0, 0, 0)),
                   pl.BlockSpec((None, None, nsub, npair, 2 * PAIR, PAIR), lambda d, b, c: (d, b, c, 0, 0, 0))],
        out_shape=[jax.ShapeDtypeStruct((2, batch, nc, npair, PAIR, 2 * PAIR), BF16),
                   jax.ShapeDtypeStruct((2, batch, nc, npair, 2 * PAIR, PAIR), BF16)],
        compiler_params=_cparams("parallel", "parallel", "parallel"),
        name="wkv_chunks",
    )(r, v, an, lw, kd, bb)


def _wkv_scan_kernel(ryf_ref, ryb_ref, pqf_ref, pqb_ref, g0_ref, yf_ref, yb_ref, gout_ref, g_ref):
    pos = pl.program_id(0)

    @pl.when(pos == 0)
    def _():
        g_ref[...] = g0_ref[...]

    batch, npair = g_ref.shape[1], g_ref.shape[2]
    for d, (ry_ref, pq_ref, y_ref) in enumerate(((ryf_ref, pqf_ref, yf_ref), (ryb_ref, pqb_ref, yb_ref))):
        for b in range(batch):
            for p in range(npair):
                g = g_ref[d, b, p]
                ry = ry_ref[b, p]
                pq = pq_ref[b, p]
                ys = _dot_nt(ry[:, :PAIR], g) + ry[:, PAIR:].astype(F32)
                y_ref[b, :, p * PAIR:(p + 1) * PAIR] = ys[:CHUNK] + ys[CHUNK:]
                g_ref[d, b, p] = _dot(g, pq[:PAIR]) + pq[PAIR:].astype(F32)

    @pl.when(pos == pl.num_programs(0) - 1)
    def _():
        gout_ref[...] = g_ref[...]


def _wkv_scan(ry, pq, g0):
    _, batch, nc, npair, _, _ = ry.shape
    w = npair * PAIR
    t = nc * CHUNK
    ry_f = pl.BlockSpec((None, batch, None, npair, PAIR, 2 * PAIR), lambda s: (0, 0, s, 0, 0, 0))
    ry_b = pl.BlockSpec((None, batch, None, npair, PAIR, 2 * PAIR), lambda s: (1, 0, nc - 1 - s, 0, 0, 0))
    pq_f = pl.BlockSpec((None, batch, None, npair, 2 * PAIR, PAIR), lambda s: (0, 0, s, 0, 0, 0))
    pq_b = pl.BlockSpec((None, batch, None, npair, 2 * PAIR, PAIR), lambda s: (1, 0, nc - 1 - s, 0, 0, 0))
    gspec = pl.BlockSpec((2, batch, npair, PAIR, PAIR), lambda s: (0, 0, 0, 0, 0))
    return pl.pallas_call(
        _wkv_scan_kernel,
        grid=(nc,),
        in_specs=[ry_f, ry_b, pq_f, pq_b, gspec],
        out_specs=[pl.BlockSpec((batch, CHUNK, w), lambda s: (0, s, 0)),
                   pl.BlockSpec((batch, CHUNK, w), lambda s: (0, nc - 1 - s, 0)),
                   gspec],
        out_shape=[jax.ShapeDtypeStruct((batch, t, w), F32),
                   jax.ShapeDtypeStruct((batch, t, w), F32),
                   jax.ShapeDtypeStruct((2, batch, npair, PAIR, PAIR), F32)],
        scratch_shapes=[pltpu.VMEM((2, batch, npair, PAIR, PAIR), F32)],
        compiler_params=_cparams("arbitrary"),
        name="wkv_scan",
    )(ry, ry, pq, pq, g0)


GRID_SHIFT = ((-1, "first_col"), (1, "last_col"), (-GRID_W, None), (GRID_W, None))
SEQ_SHIFT = ((-1, None), (1, None))


def _prep_kernel(xp_ref, xm_ref, xn_ref, pp_ref, pm_ref, pn_ref, g_ref, m_ref,
                 mux_ref, mup_ref, dw0_ref, dw1_ref, dw2_ref, la0_ref, la1_ref, la2_ref,
                 gg1_ref, gg2_ref, kk_ref, ka_ref, rk_ref, pw_ref, ps_ref, seg_ref,
                 r_out, v_out, an_out, lw_out, kd_out, bb_out, bonus_out, gate_out, yb_out,
                 hext_ref, pext_ref, *, parts, tm, pad, tiles_per_batch, seq_len):
    i = pl.program_id(0)
    tile = lax.rem(i, tiles_per_batch)
    keep_prev = jnp.where(tile == 0, 0.0, 1.0)
    keep_next = jnp.where(tile == tiles_per_batch - 1, 0.0, 1.0)
    g = g_ref[...]
    shift, scale = m_ref[0:1, :], m_ref[1:2, :]
    d_model = xm_ref.shape[1]
    wa = r_out.shape[1]

    hext_ref[0:pad, :] = _modulate(xp_ref[...], g, shift, scale) * keep_prev
    hext_ref[pad:pad + tm, :] = _modulate(xm_ref[...], g, shift, scale)
    hext_ref[pad + tm:, :] = _modulate(xn_ref[...], g, shift, scale) * keep_next
    pext_ref[0:pad, :] = pp_ref[...] * keep_prev
    pext_ref[pad:pad + tm, :] = pm_ref[...]
    pext_ref[pad + tm:, :] = pn_ref[...] * keep_next

    colidx = jnp.bitwise_and(lax.broadcasted_iota(jnp.int32, (tm, 1), 0), GRID_W - 1)

    def shifted(ext_ref, col0, width):
        pw = width // len(parts)
        outs = []
        for q, (off, mask) in enumerate(parts):
            blk = ext_ref[pad + off:pad + off + tm, col0 + q * pw:col0 + (q + 1) * pw]
            if mask == "first_col":
                blk = jnp.where(colidx == 0, 0.0, blk)
            elif mask == "last_col":
                blk = jnp.where(colidx == GRID_W - 1, 0.0, blk)
            outs.append(blk)
        return jnp.concatenate(outs, axis=1)

    h = hext_ref[pad:pad + tm, :]
    hx = shifted(hext_ref, 0, d_model) - h
    x_w = h + hx * mux_ref[0:1, :]
    x_a = h + hx * mux_ref[1:2, :]
    x_g = h + hx * mux_ref[2:3, :]
    zw_mid = jnp.tanh(_dot(x_w, dw1_ref[...]))
    xa_mid = _dot(x_a, la1_ref[...])
    gate_out[...] = _dot(_sigmoid(_dot(x_g, gg1_ref[...])), gg2_ref[...])

    def mixed(n):
        p_n = pext_ref[pad:pad + tm, n * wa:(n + 1) * wa]
        return p_n + (shifted(pext_ref, n * wa, wa) - p_n) * mup_ref[n:n + 1, :]

    r, k, v = mixed(0), mixed(1), mixed(2)
    seg = seg_ref[...]
    kk = k * kk_ref[...]
    kk = kk * lax.rsqrt(jnp.maximum(_dot(kk * kk, seg), 1e-12))
    r_out[...] = r
    v_out[...] = v
    an_out[...] = -kk
    ka = ka_ref[...]
    kd_sum = jnp.zeros_like(k)
    for d in range(2):
        zw = dw0_ref[d:d + 1, :] + _dot(zw_mid, dw2_ref[d])
        lw_out[d] = -math.exp(-0.5) * _sigmoid(zw)
        a_lr = _sigmoid(la0_ref[d:d + 1, :] + _dot(xa_mid, la2_ref[d]))
        kd = k * (1.0 + (a_lr - 1.0) * ka)
        kd_out[d] = kd
        bb_out[d] = kk * a_lr
        kd_sum = kd_sum + kd
    bonus_out[...] = _dot(r * kd_sum * rk_ref[...], seg) * v

    pos = tile * tm + lax.broadcasted_iota(jnp.int32, (tm, 1), 0)
    trow = lax.broadcasted_iota(jnp.int32, (tm, tm + 2 * pad), 0)
    srow = lax.broadcasted_iota(jnp.int32, (tm, tm + 2 * pad), 1) - pad
    gp = wa // len(POOL_WINDOWS)
    ybs = []
    for gi, win in enumerate(POOL_WINDOWS):
        half = win // 2
        c0 = 3 * wa + gi * gp
        band = jnp.where((srow >= trow - half) & (srow < trow + half), 1.0, 0.0).astype(BF16)
        sums = _dot(band, pext_ref[:, c0:c0 + gp])
        cnt = (jnp.minimum(pos + half, seq_len) - jnp.maximum(pos - half, 0)).astype(F32)
        diff = sums / cnt - pext_ref[pad:pad + tm, c0:c0 + gp]
        ybs.append(_dot(diff, pw_ref[gi]))
    yb_out[...] = jnp.concatenate(ybs, axis=1) * ps_ref[...]


def _rwkv_prep(x, proj, g, mod, mod_base, mod_stride, wts, batch, t, grid_mode):
    n, d = x.shape
    wa = d // 2
    pcols = proj.shape[1]
    tm = 256
    pad = GRID_W if grid_mode else 8
    parts = GRID_SHIFT if grid_mode else SEQ_SHIFT
    tpb = t // tm
    hb = tm // pad
    nhb = n // pad
    main = lambda i: (i, 0)
    prev = lambda i: (jnp.maximum(i * hb - 1, 0), 0)
    nxt = lambda i: (jnp.minimum((i + 1) * hb, nhb - 1), 0)
    full2 = lambda i: (0, 0)
    full3 = lambda i: (0, 0, 0)
    in_specs = [pl.BlockSpec((pad, d), prev), pl.BlockSpec((tm, d), main), pl.BlockSpec((pad, d), nxt),
                pl.BlockSpec((pad, pcols), prev), pl.BlockSpec((tm, pcols), main), pl.BlockSpec((pad, pcols), nxt),
                pl.BlockSpec((1, d), full2),
                pl.BlockSpec((None, 6, d), lambda i: (mod_base + (i // tpb) * mod_stride, 0, 0))]
    for a in wts:
        in_specs.append(pl.BlockSpec(a.shape, full2 if a.ndim == 2 else full3))
    one = pl.BlockSpec((tm, wa), main)
    two = pl.BlockSpec((2, tm, wa), lambda i: (0, i, 0))
    sd1 = jax.ShapeDtypeStruct((n, wa), F32)
    sd2 = jax.ShapeDtypeStruct((2, n, wa), F32)
    return pl.pallas_call(
        functools.partial(_prep_kernel, parts=parts, tm=tm, pad=pad, tiles_per_batch=tpb, seq_len=t),
        grid=(n // tm,),
        in_specs=in_specs,
        out_specs=[one, one, one, two, two, two, one, one, one],
        out_shape=[sd1, sd1, sd1, sd2, sd2, sd2, sd1, sd1, sd1],
        scratch_shapes=[pltpu.VMEM((tm + 2 * pad, d), F32), pltpu.VMEM((tm + 2 * pad, pcols), F32)],
        compiler_params=_cparams("parallel"),
        name="rwkv_prep",
    )(x, x, x, proj, proj, proj, g.reshape(1, d), mod, *wts)


def _mix_out_kernel(y0_ref, y1_ref, bonus_ref, gate_ref, yb_ref, x_ref, m_ref, gnw_ref, gnb_ref,
                    seg_ref, wout_ref, o_ref):
    y = y0_ref[...] + y1_ref[...]
    seg = seg_ref[...]
    inv = 1.0 / HEAD_DIM
    yh = y.astype(BF16)
    mu = (jnp.dot(yh, seg, preferred_element_type=F32) + _dot(y - yh.astype(F32), seg)) * inv
    dlt = y - mu
    var = _dot(dlt * dlt, seg) * inv
    yn = dlt * lax.rsqrt(var + GN_EPS) * gnw_ref[...] + gnb_ref[...]
    ya = (yn + bonus_ref[...]) * gate_ref[...]
    cat = jnp.concatenate([ya, yb_ref[...]], axis=1)
    o_ref[...] = x_ref[...] + m_ref[2:3, :] * _dot(cat, wout_ref[...])


def _mix_out(y0, y1, bonus, gate, yb, x, mod, mod_base, mod_stride, gnw, gnb, seg, wout, t, tm):
    n, d = x.shape
    wa = d // 2
    tpb = t // tm
    half = pl.BlockSpec((tm, wa), lambda i: (i, 0))
    full = pl.BlockSpec((tm, d), lambda i: (i, 0))
    const = lambda a: pl.BlockSpec(a.shape, lambda i: (0, 0))
    return pl.pallas_call(
        _mix_out_kernel,
        grid=(n // tm,),
        in_specs=[half, half, half, half, half, full,
                  pl.BlockSpec((None, 6, d), lambda i: (mod_base + (i // tpb) * mod_stride, 0, 0)),
                  const(gnw), const(gnb), const(seg), const(wout)],
        out_specs=full,
        out_shape=jax.ShapeDtypeStruct((n, d), F32),
        input_output_aliases={5: 0},
        compiler_params=_cparams("parallel"),
        name="mix_out",
    )(y0, y1, bonus, gate, yb, x, mod, gnw, gnb, seg, wout)


def _chan_dft_kernel(x_ref, g_ref, m_ref, cs_ref, xc_ref, xs_ref, *, groups):
    h = _modulate(x_ref[...], g_ref[...], m_ref[0:1, :], m_ref[1:2, :])
    gc = h.shape[1] // groups
    cs = cs_ref[...]
    for gi in range(groups):
        res = _dot(h[:, gi * gc:(gi + 1) * gc], cs)
        xc_ref[:, gi * gc:(gi + 1) * gc] = res[:, :gc].astype(xc_ref.dtype)
        xs_ref[:, gi * gc:(gi + 1) * gc] = res[:, gc:].astype(xs_ref.dtype)


def _chan_dft(x, g, mod, mod_base, mod_stride, cs, t, tm, groups):
    n, d = x.shape
    tpb = t // tm
    full = pl.BlockSpec((tm, d), lambda i: (i, 0))
    return pl.pallas_call(
        functools.partial(_chan_dft_kernel, groups=groups),
        grid=(n // tm,),
        in_specs=[full, pl.BlockSpec((1, d), lambda i: (0, 0)),
                  pl.BlockSpec((None, 6, d), lambda i: (mod_base + (i // tpb) * mod_stride, 0, 0)),
                  pl.BlockSpec(cs.shape, lambda i: (0, 0))],
        out_specs=[full, full],
        out_shape=[jax.ShapeDtypeStruct((n, d), BF16)] * 2,
        compiler_params=_cparams("parallel"),
        name="chan_dft",
    )(x, g.reshape(1, d), mod, cs)


LANES = 128


def _time_dft_kernel(ct_ref, st_ref, xc_ref, xs_ref, x_ref, m_ref, wf_ref, o_ref, acc_ref):
    ki = pl.program_id(2)

    @pl.when(ki == 0)
    def _():
        acc_ref[...] = jnp.zeros_like(acc_ref)

    acc_ref[...] += (jnp.dot(ct_ref[...], xc_ref[...], preferred_element_type=F32)
                     - jnp.dot(st_ref[...], xs_ref[...], preferred_element_type=F32))

    @pl.when(ki == pl.num_programs(2) - 1)
    def _():
        o_ref[...] = x_ref[...] + m_ref[2:3, :] * _dot(acc_ref[...], wf_ref[...])


def _time_dft(tabs, xc, xs, x, mod, mod_base, mod_stride, wf, batch, t, tf, tk):
    ct, st = tabs
    n, d = x.shape
    nf, nk = t // tf, t // tk
    return pl.pallas_call(
        _time_dft_kernel,
        grid=(batch, nf, nk),
        in_specs=[pl.BlockSpec((tf, tk), lambda b, f, k: (f, k)),
                  pl.BlockSpec((tf, tk), lambda b, f, k: (f, k)),
                  pl.BlockSpec((tk, d), lambda b, f, k: (b * nk + k, 0)),
                  pl.BlockSpec((tk, d), lambda b, f, k: (b * nk + k, 0)),
                  pl.BlockSpec((tf, d), lambda b, f, k: (b * nf + f, 0)),
                  pl.BlockSpec((None, 6, d), lambda b, f, k: (mod_base + b * mod_stride, 0, 0)),
                  pl.BlockSpec((d, d), lambda b, f, k: (0, 0))],
        out_specs=pl.BlockSpec((tf, d), lambda b, f, k: (b * nf + f, 0)),
        out_shape=jax.ShapeDtypeStruct((n, d), F32),
        scratch_shapes=[pltpu.VMEM((tf, d), F32)],
        input_output_aliases={4: 0},
        compiler_params=_cparams("parallel", "parallel", "arbitrary"),
        name="time_dft",
    )(ct, st, xc, xs, x, mod, wf)


def _trig(rows, cols, period):
    prod = jnp.bitwise_and(jnp.arange(rows, dtype=jnp.int32)[:, None] * jnp.arange(cols, dtype=jnp.int32)[None, :],
                           period - 1)
    ang = prod.astype(F32) * (2.0 * math.pi / period)
    return jnp.cos(ang), jnp.sin(ang)


def _dft_table_kernel(ca_ref, sa_ref, cb_ref, sb_ref, ct_ref, st_ref):
    ca, sa = ca_ref[...], sa_ref[...]
    cb, sb = cb_ref[...], sb_ref[...]
    for j in range(ca.shape[1]):
        a, s = ca[:, j:j + 1], sa[:, j:j + 1]
        ct_ref[:, j * LANES:(j + 1) * LANES] = (a * cb - s * sb).astype(ct_ref.dtype)
        st_ref[:, j * LANES:(j + 1) * LANES] = (s * cb + a * sb).astype(st_ref.dtype)


def _time_tables(t, scale):
    hi_n = t // LANES
    ca, sa = _trig(t, hi_n, hi_n)
    cb, sb = _trig(t, LANES, t)
    tf = min(t, 256)
    small = pl.BlockSpec((tf, hi_n), lambda f: (f, 0))
    lanes = pl.BlockSpec((tf, LANES), lambda f: (f, 0))
    wide = pl.BlockSpec((tf, t), lambda f: (f, 0))
    return pl.pallas_call(
        _dft_table_kernel,
        grid=(t // tf,),
        in_specs=[small, small, lanes, lanes],
        out_specs=[wide, wide],
        out_shape=[jax.ShapeDtypeStruct((t, t), BF16)] * 2,
        compiler_params=_cparams("parallel"),
        name="dft_table",
    )(ca * scale, sa * scale, cb, sb)


def _router_kernel(x_ref, g_ref, m_ref, wr_ref, br_ref, tri_ref, ids_out, gates_out, cnt_out, base_ref):
    i = pl.program_id(0)

    @pl.when(i == 0)
    def _():
        base_ref[...] = jnp.zeros_like(base_ref)

    h2 = _modulate(x_ref[...], g_ref[...], m_ref[3:4, :], m_ref[4:5, :])
    hs = _split3(h2)
    logits = br_ref[...]
    for ia, ib in ((0, 0), (0, 1), (1, 0), (0, 2), (2, 0), (1, 1)):
        logits = logits + jnp.dot(hs[ia], wr_ref[ib], preferred_element_type=F32)

    tm = logits.shape[0]
    lane_i = lax.broadcasted_iota(jnp.int32, (tm, 128), 1)
    lane = lane_i.astype(F32)
    neg = -jnp.inf
    gmask = lane_i < MOE_GROUPS
    lc = jnp.where(gmask, logits, neg)
    mc = jnp.max(lc, axis=1, keepdims=True)
    sc = jnp.sum(jnp.where(gmask, jnp.exp(logits - mc), 0.0), axis=1, keepdims=True)
    g_val = 1.0 / sc
    g_idx = jnp.min(jnp.where(lc == mc, lane, 128.0), axis=1, keepdims=True)
    lgroup = jnp.where(lane_i >= MOE_GROUPS, jnp.right_shift(lane_i - MOE_GROUPS, 3), -1).astype(F32)
    sel = lgroup == g_idx
    lf = jnp.where(sel, logits, neg)
    m1 = jnp.max(lf, axis=1, keepdims=True)
    ef = jnp.where(sel, jnp.exp(logits - m1), 0.0)
    p = ef / jnp.sum(ef, axis=1, keepdims=True)
    p1 = jnp.where(sel, p, -1.0)
    v1 = jnp.max(p1, axis=1, keepdims=True)
    i1 = jnp.min(jnp.where(p1 == v1, lane, 128.0), axis=1, keepdims=True)
    p2 = jnp.where(lane == i1, -1.0, p1)
    v2 = jnp.max(p2, axis=1, keepdims=True)
    i2 = jnp.min(jnp.where(p2 == v2, lane, 128.0), axis=1, keepdims=True)
    denom = v1 + v2
    gate1 = g_val * v1 / denom
    gate2 = g_val * v2 / denom
    e1 = i1 - MOE_GROUPS
    e2 = i2 - MOE_GROUPS

    tri = tri_ref[...]
    oh1 = jnp.where(lane == e1, 1.0, 0.0)
    oh2 = jnp.where(lane == e2, 1.0, 0.0)
    base = base_ref[...]
    tot1 = jnp.sum(oh1, axis=0, keepdims=True)
    tot2 = jnp.sum(oh2, axis=0, keepdims=True)
    c1 = jnp.dot(tri, oh1.astype(BF16), preferred_element_type=F32)
    c2 = jnp.dot(tri, oh2.astype(BF16), preferred_element_type=F32)
    rank1 = jnp.sum(oh1 * (base + c1), axis=1, keepdims=True)
    rank2 = jnp.sum(oh2 * (base + tot1 + c2), axis=1, keepdims=True)
    base = base + tot1 + tot2
    base_ref[...] = base
    cnt_out[...] = base
    ids_out[...] = jnp.where(lane_i == 0, e1, jnp.where(lane_i == 1, e2, jnp.where(
        lane_i == 2, rank1, jnp.where(lane_i == 3, rank2, 0.0)))).astype(jnp.int32)
    gates_out[...] = jnp.where(lane_i == 0, gate1, jnp.where(lane_i == 1, gate2, 0.0))


def _router(x, g, mod, mod_base, mod_stride, wr3, br, tri, t, tm):
    n, d = x.shape
    tpb = t // tm
    full = pl.BlockSpec((tm, d), lambda i: (i, 0))
    lanes = pl.BlockSpec((tm, 128), lambda i: (i, 0))
    return pl.pallas_call(
        _router_kernel,
        grid=(n // tm,),
        in_specs=[full, pl.BlockSpec((1, d), lambda i: (0, 0)),
                  pl.BlockSpec((None, 6, d), lambda i: (mod_base + (i // tpb) * mod_stride, 0, 0)),
                  pl.BlockSpec(wr3.shape, lambda i: (0, 0, 0)),
                  pl.BlockSpec((1, 128), lambda i: (0, 0)),
                  pl.BlockSpec((tm, tm), lambda i: (0, 0))],
        out_specs=[lanes, lanes, pl.BlockSpec((1, 128), lambda i: (0, 0))],
        out_shape=[jax.ShapeDtypeStruct((n, 128), jnp.int32),
                   jax.ShapeDtypeStruct((n, 128), F32), jax.ShapeDtypeStruct((1, 128), F32)],
        scratch_shapes=[pltpu.VMEM((1, 128), F32)],
        compiler_params=_cparams("arbitrary"),
        name="moe_router",
    )(x, g.reshape(1, d), mod, wr3, br, tri)


def _row_copy(src, src_row, dst, dst_row, sem):
    return pltpu.make_async_copy(src.at[pl.ds(src_row, 1)], dst.at[pl.ds(dst_row, 1)], sem)


def _dispatch_kernel(dest_ref, zb_ref, x_ref, g_ref, m_ref, xbuf_ref, h_ref, zero_ref, sem, zsem, *, tm):
    step = pl.program_id(0)

    def zero_copy(s):
        start = pl.multiple_of(zb_ref[s] * MOE_BLOCK, MOE_BLOCK)
        return pltpu.make_async_copy(zero_ref, xbuf_ref.at[pl.ds(start, MOE_BLOCK)], zsem)

    @pl.when(step == 0)
    def _():
        zero_ref[...] = jnp.zeros_like(zero_ref)
        for s in range(zb_ref.shape[0]):
            @pl.when(zb_ref[s] >= 0)
            def _():
                zero_copy(s).start()
        for s in range(zb_ref.shape[0]):
            @pl.when(zb_ref[s] >= 0)
            def _():
                zero_copy(s).wait()

    h_ref[...] = _modulate(x_ref[...], g_ref[...], m_ref[3:4, :], m_ref[4:5, :])
    base = step * tm

    def body(r, carry):
        tok = base + r
        for j in range(2):
            _row_copy(h_ref, r, xbuf_ref, dest_ref[2 * tok + j], sem).start(priority=j)
        return carry

    lax.fori_loop(0, tm, body, 0, unroll=8)

    def drain(r, carry):
        for j in range(2):
            _row_copy(h_ref, 0, xbuf_ref, 0, sem).wait()
        return carry

    lax.fori_loop(0, tm, drain, 0, unroll=8)


def _dispatch(dest, zblocks, x, g, mod, mod_base, mod_stride, rows, t, tm):
    n, d = x.shape
    tpb = t // tm
    return pl.pallas_call(
        functools.partial(_dispatch_kernel, tm=tm),
        grid_spec=pltpu.PrefetchScalarGridSpec(
            num_scalar_prefetch=2,
            grid=(n // tm,),
            in_specs=[pl.BlockSpec((tm, d), lambda i, dr, zb: (i, 0)),
                      pl.BlockSpec((1, d), lambda i, dr, zb: (0, 0)),
                      pl.BlockSpec((None, 6, d), lambda i, dr, zb: (mod_base + (i // tpb) * mod_stride, 0, 0))],
            out_specs=pl.BlockSpec(memory_space=pl.ANY),
            scratch_shapes=[pltpu.VMEM((tm, d), F32), pltpu.VMEM((MOE_BLOCK, d), F32),
                            pltpu.SemaphoreType.DMA, pltpu.SemaphoreType.DMA],
        ),
        out_shape=jax.ShapeDtypeStruct((rows, d), F32),
        compiler_params=_cparams("arbitrary"),
        name="moe_dispatch",
    )(dest, zblocks, x, g.reshape(1, d), mod)


def _expert_kernel(be_ref, na_ref, x_ref, w1_ref, w3_ref, w2_ref, o_ref):
    active = pl.program_id(0) < na_ref[0]

    @pl.when(active)
    def _():
        xb = x_ref[...].astype(BF16)
        h1 = jnp.dot(xb, w1_ref[...].astype(BF16), preferred_element_type=F32)
        h3 = jnp.dot(xb, w3_ref[...].astype(BF16), preferred_element_type=F32)
        act = h1 * _sigmoid(h1) * h3
        o_ref[...] = _dot(act, w2_ref[...])

    @pl.when(jnp.logical_not(active))
    def _():
        o_ref[...] = jnp.zeros_like(o_ref)


def _experts(block_e, nact, xbuf, w1, w3, w2, layer):
    rows, d = xbuf.shape
    nb = rows // MOE_BLOCK
    de = w1.shape[3]
    blk = lambda i, be, na: (jnp.minimum(i, na[0] - 1), 0)
    wmap = lambda i, be, na: (layer, be[jnp.minimum(i, na[0] - 1)], 0, 0)
    return pl.pallas_call(
        _expert_kernel,
        grid_spec=pltpu.PrefetchScalarGridSpec(
            num_scalar_prefetch=2,
            grid=(nb,),
            in_specs=[pl.BlockSpec((MOE_BLOCK, d), blk),
                      pl.BlockSpec((None, None, d, de), wmap),
                      pl.BlockSpec((None, None, d, de), wmap),
                      pl.BlockSpec((None, None, de, d), wmap)],
            out_specs=pl.BlockSpec((MOE_BLOCK, d), lambda i, be, na: (i, 0)),
        ),
        out_shape=jax.ShapeDtypeStruct((rows, d), F32),
        compiler_params=_cparams("arbitrary"),
        name="moe_experts",
    )(block_e, nact, xbuf, w1, w3, w2)


def _combine_kernel(dest_ref, y_ref, gates_ref, x_ref, m_ref, fn_ref, o_ref, rows_ref, sem, *, tm, final):
    base = pl.program_id(0) * tm

    def body(r, carry):
        tok = base + r
        for j in range(2):
            _row_copy(y_ref, dest_ref[2 * tok + j], rows_ref.at[j], r, sem).start(priority=j)
        return carry

    lax.fori_loop(0, tm, body, 0, unroll=8)

    def drain(r, carry):
        for j in range(2):
            _row_copy(y_ref, 0, rows_ref.at[j], 0, sem).wait()
        return carry

    lax.fori_loop(0, tm, drain, 0, unroll=8)

    gates = gates_ref[...]
    y = gates[:, 0:1] * rows_ref[0] + gates[:, 1:2] * rows_ref[1]
    out = x_ref[...] + m_ref[5:6, :] * y
    if final:
        ms = jnp.mean(out * out, axis=-1, keepdims=True)
        out = out * lax.rsqrt(ms + NORM_EPS) * fn_ref[...]
    o_ref[...] = out


def _combine(dest, ybuf, gates, x, mod, mod_base, mod_stride, fnorm, t, tm, final):
    n, d = x.shape
    tpb = t // tm
    full = pl.BlockSpec((tm, d), lambda i, dr: (i, 0))
    return pl.pallas_call(
        functools.partial(_combine_kernel, tm=tm, final=final),
        grid_spec=pltpu.PrefetchScalarGridSpec(
            num_scalar_prefetch=1,
            grid=(n // tm,),
            in_specs=[pl.BlockSpec(memory_space=pl.ANY),
                      pl.BlockSpec((tm, 128), lambda i, dr: (i, 0)),
                      full,
                      pl.BlockSpec((None, 6, d), lambda i, dr: (mod_base + (i // tpb) * mod_stride, 0, 0)),
                      pl.BlockSpec((1, d), lambda i, dr: (0, 0))],
            out_specs=full,
            scratch_shapes=[pltpu.VMEM((2, tm, d), F32), pltpu.SemaphoreType.DMA],
        ),
        out_shape=jax.ShapeDtypeStruct((n, d), F32),
        input_output_aliases={3: 0},
        compiler_params=_cparams("arbitrary"),
        name="moe_combine",
    )(dest, ybuf, gates, x, mod, fnorm)


def _moe(x, g, mod, mod_base, mod_stride, rt, w1, w3, w2, layer, fnorm, t, tm, final):
    n, d = x.shape
    n_exp = w1.shape[1]
    wr3, br, tri = rt
    ids, gates, counts = _router(x, g, mod, mod_base, mod_stride, wr3, br, tri, t, tm)
    counts = counts[0, :n_exp].astype(jnp.int32)
    padded = (counts + MOE_BLOCK - 1) // MOE_BLOCK * MOE_BLOCK
    pend = jnp.cumsum(padded)
    pstart = pend - padded
    onehot = ids[:, 0:2, None] == jnp.arange(n_exp, dtype=jnp.int32)
    dest = (jnp.sum(jnp.where(onehot, pstart, 0), axis=-1) + ids[:, 2:4]).reshape(2 * n)
    nb = (2 * n) // MOE_BLOCK + n_exp
    blk_start = jnp.arange(nb, dtype=jnp.int32) * MOE_BLOCK
    block_e = jnp.minimum(jnp.sum((pend[None, :] <= blk_start[:, None]).astype(jnp.int32), axis=1), n_exp - 1)
    nact = pend[-1:] // MOE_BLOCK
    partial = jnp.where(padded > counts, pend // MOE_BLOCK - 1, -1)
    tail = nact + jnp.arange(n_exp, dtype=jnp.int32)
    zblocks = jnp.concatenate([partial, jnp.where(tail < nb, tail, -1)]).astype(jnp.int32)
    xbuf = _dispatch(dest, zblocks, x, g, mod, mod_base, mod_stride, nb * MOE_BLOCK, t, tm)
    ybuf = _experts(block_e, nact, xbuf, w1, w3, w2, layer)
    return _combine(dest, ybuf, gates, x, mod, mod_base, mod_stride, fnorm, t, tm, final)


def _seg_ones(width):
    idx = np.arange(width) // HEAD_DIM
    return jnp.asarray((idx[:, None] == idx[None, :]).astype(np.float32), dtype=BF16)


def _even_weights(j, mu_x, mu_p, decay_w0, decay_w1, decay_w2, lr_a0, lr_a1, lr_a2, gate_g1, gate_g2,
                  k_k, k_a, r_k, pool_w, pool_scale, seg):
    wa = mu_p.shape[-1]

    def cat1(w):
        return jnp.concatenate([w[0], w[1]], axis=1).astype(BF16)

    def pad2(w):
        z = jnp.zeros_like(w[0])
        return jnp.stack([jnp.concatenate([w[0], z], axis=0), jnp.concatenate([z, w[1]], axis=0)]).astype(BF16)

    return (mu_x[j], mu_p[j], decay_w0[j], cat1(decay_w1[j]), pad2(decay_w2[j]),
            lr_a0[j], cat1(lr_a1[j]), pad2(lr_a2[j]),
            gate_g1[j].astype(BF16), gate_g2[j].astype(BF16),
            k_k[j].reshape(1, wa), k_a[j].reshape(1, wa), r_k[j].reshape(1, wa),
            pool_w[j].astype(BF16), pool_scale[j].reshape(1, wa), seg)


def kernel(x, c, ctx, c_ctx, ada_w, ada_b, norm_mix, norm_ffn, w_in, mu_x, mu_p, decay_w0, decay_w1, decay_w2, lr_a0, lr_a1, lr_a2, gate_g1, gate_g2, k_k, k_a, r_k, gn_w, gn_b, pool_w, pool_scale, w_out, w_fourier, router_c, router_c_b, router_f, router_f_b, moe_w1, moe_w3, moe_w2, final_norm):
    batch, t, d = x.shape
    tc = ctx.shape[1]
    depth = ada_w.shape[0]
    wa = d // 2
    n, ncx = batch * t, batch * tc
    assert batch <= 4 and t % 512 == 0 and tc % 256 == 0 and d % 512 == 0

    cond8 = jnp.zeros((8, d), F32).at[:batch].set(c).at[4].set(c_ctx)
    mod_all = _ada(cond8, ada_w, ada_b).reshape(depth, 8, 6, d)
    seg = _seg_ones(wa)
    fnorm = final_norm.reshape(1, d)

    fgroups = 4
    gc = d // fgroups
    cc, sc = _trig(gc, gc, gc)
    cs = (jnp.concatenate([cc, sc], axis=1) * gc ** -0.5).astype(BF16)
    tf_lat, tk_lat = 1024, 1024
    tab_lat = _time_tables(t, t ** -0.5)
    tab_ctx = _time_tables(tc, tc ** -0.5)

    n_exp = moe_w1.shape[1]
    tri = {tm: jnp.asarray(np.tril(np.ones((tm, tm), np.float32), -1), dtype=BF16) for tm in (512, 256)}

    lat = x.reshape(n, d)
    cx = ctx.reshape(ncx, d)
    last_read = 2 * ((depth - 1) // 2)
    npair = wa // PAIR

    for i in range(depth):
        ctx_in = i <= last_read
        ctx_out = i < last_read
        mod = mod_all[i]
        j = i // 2
        if i % 2 == 0:
            wts = _even_weights(j, mu_x, mu_p, decay_w0, decay_w1, decay_w2, lr_a0, lr_a1, lr_a2,
                                gate_g1, gate_g2, k_k, k_a, r_k, pool_w, pool_scale, seg)
            w_in_b = w_in[j].astype(BF16)
            w_out_b = w_out[j].astype(BF16)
            gnw, gnb = gn_w[j].reshape(1, wa), gn_b[j].reshape(1, wa)
            pc = _proj(cx, norm_mix[i], mod, w_in_b, tc, 4, 0, 0, 256)
            fc = _rwkv_prep(cx, pc, norm_mix[i], mod, 4, 0, wts, batch, tc, False)
            ryc, pqc = _wkv_chunks(*fc[:6], batch, tc)
            g0 = jnp.zeros((2, batch, npair, PAIR, PAIR), F32)
            yc0, yc1, gctx = _wkv_scan(ryc, pqc, g0)
            pl_ = _proj(lat, norm_mix[i], mod, w_in_b, t, 0, 1, 0, 512)
            fl = _rwkv_prep(lat, pl_, norm_mix[i], mod, 0, 1, wts, batch, t, True)
            ryl, pql = _wkv_chunks(*fl[:6], batch, t)
            yl0, yl1, _ = _wkv_scan(ryl, pql, gctx)
            lat = _mix_out(yl0.reshape(n, wa), yl1.reshape(n, wa), fl[6], fl[7], fl[8], lat, mod, 0, 1,
                           gnw, gnb, seg, w_out_b, t, 512)
            if ctx_out:
                cx = _mix_out(yc0.reshape(ncx, wa), yc1.reshape(ncx, wa), fc[6], fc[7], fc[8], cx, mod, 4, 0,
                              gnw, gnb, seg, w_out_b, tc, 256)
        else:
            wf_b = w_fourier[j].astype(BF16)
            xc, xs = _chan_dft(lat, norm_mix[i], mod, 0, 1, cs, t, 512, fgroups)
            lat = _time_dft(tab_lat, xc, xs, lat, mod, 0, 1, wf_b, batch, t, tf_lat, tk_lat)
            if ctx_out:
                xcc, xsc = _chan_dft(cx, norm_mix[i], mod, 4, 0, cs, tc, 256, fgroups)
                cx = _time_dft(tab_ctx, xcc, xsc, cx, mod, 4, 0, wf_b, batch, tc, tc, tc)
        wr = jnp.zeros((d, 128), F32).at[:, :MOE_GROUPS].set(router_c[i]).at[:, MOE_GROUPS:MOE_GROUPS + n_exp].set(router_f[i])
        br = jnp.zeros((1, 128), F32).at[0, :MOE_GROUPS].set(router_c_b[i]).at[0, MOE_GROUPS:MOE_GROUPS + n_exp].set(router_f_b[i])
        wr3 = jnp.stack(_split3(wr))
        final = i == depth - 1
        lat = _moe(lat, norm_ffn[i], mod, 0, 1, (wr3, br, tri[512]), moe_w1, moe_w3, moe_w2, i, fnorm,
                   t, 512, final)
        if ctx_out:
            cx = _moe(cx, norm_ffn[i], mod, 4, 0, (wr3, br, tri[256]), moe_w1, moe_w3, moe_w2, i, fnorm,
                      tc, 256, False)
    return lat.reshape(batch, t, d)
```

```python
import functools
import math

import jax
import jax.numpy as jnp
import numpy as np
from jax import lax
from jax.experimental import pallas as pl
from jax.experimental.pallas import tpu as pltpu

F32 = jnp.float32
BF16 = jnp.bfloat16

GRID_W = 64
HEAD_DIM = 64
CHUNK = 64
PAIR = 2 * HEAD_DIM
NORM_EPS = 1e-6
GN_EPS = 64e-5
POOL_WINDOWS = (2, 4, 8, 16)
MOE_GROUPS = 4
ROUTE_ROWS = 40
EXPERTS_PER_GROUP = 8
MOE_BLOCK = 512
VMEM_LIMIT = 56 * 1024 * 1024


def _cparams(*sem):
    return pltpu.CompilerParams(dimension_semantics=tuple(sem), vmem_limit_bytes=VMEM_LIMIT)


def _dot(a, b):
    return jnp.dot(a.astype(BF16), b.astype(BF16), preferred_element_type=F32)


def _dot_nt(a, b):
    return lax.dot_general(a.astype(BF16), b.astype(BF16), (((1,), (1,)), ((), ())),
                           preferred_element_type=F32)


def _dot_tn(a, b):
    return lax.dot_general(a.astype(BF16), b.astype(BF16), (((0,), (0,)), ((), ())),
                           preferred_element_type=F32)


def _split3(x):
    hi = x.astype(BF16)
    r1 = x - hi.astype(F32)
    mid = r1.astype(BF16)
    lo = (r1 - mid.astype(F32)).astype(BF16)
    return hi, mid, lo


def _sigmoid(x):
    return 1.0 / (1.0 + jnp.exp(-x))


def _modulate(x, g, shift, scale):
    ms = jnp.mean(x * x, axis=-1, keepdims=True)
    return x * lax.rsqrt(ms + NORM_EPS) * g * (1.0 + scale) + shift


def _ada_kernel(c_ref, w_ref, b_ref, o_ref):
    c = c_ref[...]
    s = c * _sigmoid(c)
    o_ref[...] = _dot(s, w_ref[...]) + b_ref[...]


def _ada(cond8, ada_w, ada_b):
    depth, d, n6 = ada_w.shape
    tn = 1536
    return pl.pallas_call(
        _ada_kernel,
        grid=(depth, n6 // tn),
        in_specs=[pl.BlockSpec((8, d), lambda l, j: (0, 0)),
                  pl.BlockSpec((None, d, tn), lambda l, j: (l, 0, j)),
                  pl.BlockSpec((None, 1, tn), lambda l, j: (l, 0, j))],
        out_specs=pl.BlockSpec((None, 8, tn), lambda l, j: (l, 0, j)),
        out_shape=jax.ShapeDtypeStruct((depth, 8, n6), F32),
        compiler_params=_cparams("parallel", "parallel"),
        name="ada_mod",
    )(cond8, ada_w, ada_b.reshape(depth, 1, n6))


def _proj_kernel(x_ref, g_ref, m_ref, w_ref, o_ref, *, shift_row, scale_row):
    h = _modulate(x_ref[...], g_ref[...], m_ref[shift_row:shift_row + 1, :], m_ref[scale_row:scale_row + 1, :])
    o_ref[...] = _dot(h, w_ref[...]).astype(o_ref.dtype)


def _proj(x, g, mod, w, rows_per_batch, mod_base, mod_stride, shift_row, tm, out_dtype=F32):
    n, d = x.shape
    nout = w.shape[1]
    tpb = rows_per_batch // tm
    return pl.pallas_call(
        functools.partial(_proj_kernel, shift_row=shift_row, scale_row=shift_row + 1),
        grid=(n // tm,),
        in_specs=[pl.BlockSpec((tm, d), lambda i: (i, 0)),
                  pl.BlockSpec((1, d), lambda i: (0, 0)),
                  pl.BlockSpec((None, 6, d), lambda i: (mod_base + (i // tpb) * mod_stride, 0, 0)),
                  pl.BlockSpec((d, nout), lambda i: (0, 0))],
        out_specs=pl.BlockSpec((tm, nout), lambda i: (i, 0)),
        out_shape=jax.ShapeDtypeStruct((n, nout), out_dtype),
        compiler_params=_cparams("parallel"),
        name="mod_proj",
    )(x, g.reshape(1, d), mod, w)


def _wkv_chunk_kernel(r_ref, v_ref, an_ref, lw_ref, kd_ref, bb_ref, ry_ref, pq_ref):
    d = pl.program_id(0)
    width = lw_ref.shape[1]
    nsub = lw_ref.shape[0] // CHUNK
    row = lax.broadcasted_iota(jnp.int32, (CHUNK, CHUNK), 0)
    col = lax.broadcasted_iota(jnp.int32, (CHUNK, CHUNK), 1)
    sgn = 1 - 2 * d
    tri = jnp.where((row - col) * sgn >= 0, 1.0, 0.0).astype(BF16)

    def chunk_feats(sc):
        rows = slice(sc * CHUNK, (sc + 1) * CHUNK)
        lw = lw_ref[rows, :]
        hi, mid, lo = _split3(lw)
        csum = (jnp.dot(tri, hi, preferred_element_type=F32)
                + jnp.dot(tri, mid, preferred_element_type=F32)
                + jnp.dot(tri, lo, preferred_element_type=F32))
        e_neg = jnp.exp(-csum)
        return (an_ref[rows, :] * jnp.exp(csum - lw), bb_ref[rows, :] * e_neg, kd_ref[rows, :] * e_neg,
                r_ref[rows, :] * jnp.exp(csum), v_ref[rows, :], jnp.exp(jnp.sum(lw, axis=0, keepdims=True)))

    feats = [chunk_feats(sc) for sc in range(nsub)]

    lane = lax.broadcasted_iota(jnp.int32, (1, PAIR), 1)
    head0 = lane < HEAD_DIM
    i2 = lax.broadcasted_iota(jnp.int32, (PAIR, PAIR), 0)
    j2 = lax.broadcasted_iota(jnp.int32, (PAIR, PAIR), 1)
    ti = jnp.bitwise_and(i2, CHUNK - 1)
    tj = jnp.bitwise_and(j2, CHUNK - 1)
    strict = (ti - tj) * sgn > 0
    incl = (ti - tj) * sgn >= 0
    eye = (i2 == j2).astype(F32)

    def stack(x):
        return jnp.concatenate([jnp.where(head0, x, 0.0), jnp.where(head0, 0.0, x)], axis=0)

    units = [(sc, p) for sc in range(nsub) for p in range(width // PAIR)]
    us = range(len(units))
    sls = [slice(p * PAIR, (p + 1) * PAIR) for _, p in units]
    a_s, b_s, k_s, r_s, v_s = ([stack(feats[sc][f][:, sls[u]]) for u, (sc, _) in enumerate(units)]
                               for f in range(5))
    m1 = [_dot_nt(jnp.concatenate([a_s[u], r_s[u]], axis=0), jnp.concatenate([b_s[u], k_s[u]], axis=0))
          for u in us]
    npow = [jnp.where(strict, m[:PAIR, :PAIR], 0.0) for m in m1]
    a_ak = [jnp.where(strict, m[:PAIR, PAIR:], 0.0) for m in m1]
    a_rb = [jnp.where(incl, m[PAIR:, :PAIR], 0.0) for m in m1]
    a_rk = [jnp.where(incl, m[PAIR:, PAIR:], 0.0) for m in m1]
    x = [jnp.concatenate([a_s[u], _dot(a_ak[u], v_s[u])], axis=1) for u in us]
    steps = int(math.log2(CHUNK))
    for it in range(steps):
        x = [x[u] + _dot(npow[u], x[u]) for u in us]
        if it + 1 < steps:
            npow = [_dot(npow[u], npow[u]) for u in us]
    for u, (sc, p) in enumerate(units):
        ry = jnp.concatenate([r_s[u], _dot(a_rk[u], v_s[u])], axis=1) + _dot(a_rb[u], x[u])
        ry_ref[sc, p] = ry.astype(ry_ref.dtype)
    for u, (sc, p) in enumerate(units):
        ge = feats[sc][5][:, sls[u]]
        pt = (eye + _dot_tn(x[u][:, :PAIR], b_s[u])) * ge
        qt = (_dot_tn(x[u][:, PAIR:], b_s[u]) + _dot_tn(v_s[u], k_s[u])) * ge
        pq_ref[sc, p] = jnp.concatenate([pt, qt], axis=0).astype(pq_ref.dtype)


def _wkv_chunks(r, v, an, lw, kd, bb, batch, t):
    n, w = r.shape
    nc = t // CHUNK
    npair = w // PAIR
    nsub = 2
    ns = nc // nsub
    shared = pl.BlockSpec((nsub * CHUNK, w), lambda d, b, c: (b * ns + c, 0))
    perdir = pl.BlockSpec((None, nsub * CHUNK, w), lambda d, b, c: (d, b * ns + c, 0))
    return pl.pallas_call(
        _wkv_chunk_kernel,
        grid=(2, batch, ns),
        in_specs=[shared, shared, shared, perdir, perdir, perdir],
        out_specs=[pl.BlockSpec((None, None, nsub, npair, PAIR, 2 * PAIR), lambda d, b, c: (d, b, c, 0, 0, 0)),
                   pl.BlockSpec((None, None, nsub, npair, 2 * PAIR, PAIR), lambda d, b, c: (d, b, c, 0, 0, 0))],
        out_shape=[jax.ShapeDtypeStruct((2, batch, nc, npair, PAIR, 2 * PAIR), BF16),
                   jax.ShapeDtypeStruct((2, batch, nc, npair, 2 * PAIR, PAIR), BF16)],
        compiler_params=_cparams("parallel", "parallel", "parallel"),
        name="wkv_chunks",
    )(r, v, an, lw, kd, bb)


def _wkv_scan_kernel(ryf_ref, ryb_ref, pqf_ref, pqb_ref, g0_ref, yf_ref, yb_ref, gout_ref, g_ref):
    pos = pl.program_id(0)

    @pl.when(pos == 0)
    def _():
        g_ref[...] = g0_ref[...]

    batch, npair = g_ref.shape[1], g_ref.shape[2]
    for d, (ry_ref, pq_ref, y_ref) in enumerate(((ryf_ref, pqf_ref, yf_ref), (ryb_ref, pqb_ref, yb_ref))):
        for b in range(batch):
            for p in range(npair):
                g = g_ref[d, b, p]
                ry = ry_ref[b, p]
                pq = pq_ref[b, p]
                ys = _dot_nt(ry[:, :PAIR], g) + ry[:, PAIR:].astype(F32)
                y_ref[b, :, p * PAIR:(p + 1) * PAIR] = ys[:CHUNK] + ys[CHUNK:]
                g_ref[d, b, p] = _dot(g, pq[:PAIR]) + pq[PAIR:].astype(F32)

    @pl.when(pos == pl.num_programs(0) - 1)
    def _():
        gout_ref[...] = g_ref[...]


def _wkv_scan(ry, pq, g0):
    _, batch, nc, npair, _, _ = ry.shape
    w = npair * PAIR
    t = nc * CHUNK
    ry_f = pl.BlockSpec((None, batch, None, npair, PAIR, 2 * PAIR), lambda s: (0, 0, s, 0, 0, 0))
    ry_b = pl.BlockSpec((None, batch, None, npair, PAIR, 2 * PAIR), lambda s: (1, 0, nc - 1 - s, 0, 0, 0))
    pq_f = pl.BlockSpec((None, batch, None, npair, 2 * PAIR, PAIR), lambda s: (0, 0, s, 0, 0, 0))
    pq_b = pl.BlockSpec((None, batch, None, npair, 2 * PAIR, PAIR), lambda s: (1, 0, nc - 1 - s, 0, 0, 0))
    gspec = pl.BlockSpec((2, batch, npair, PAIR, PAIR), lambda s: (0, 0, 0, 0, 0))
    return pl.pallas_call(
        _wkv_scan_kernel,
        grid=(nc,),
        in_specs=[ry_f, ry_b, pq_f, pq_b, gspec],
        out_specs=[pl.BlockSpec((batch, CHUNK, w), lambda s: (0, s, 0)),
                   pl.BlockSpec((batch, CHUNK, w), lambda s: (0, nc - 1 - s, 0)),
                   gspec],
        out_shape=[jax.ShapeDtypeStruct((batch, t, w), F32),
                   jax.ShapeDtypeStruct((batch, t, w), F32),
                   jax.ShapeDtypeStruct((2, batch, npair, PAIR, PAIR), F32)],
        scratch_shapes=[pltpu.VMEM((2, batch, npair, PAIR, PAIR), F32)],
        compiler_params=_cparams("arbitrary"),
        name="wkv_scan",
    )(ry, ry, pq, pq, g0)


GRID_SHIFT = ((-1, "first_col"), (1, "last_col"), (-GRID_W, None), (GRID_W, None))
SEQ_SHIFT = ((-1, None), (1, None))


def _prep_kernel(xp_ref, xm_ref, xn_ref, pp_ref, pm_ref, pn_ref, g_ref, m_ref,
                 mux_ref, mup_ref, dw0_ref, dw1_ref, dw2_ref, la0_ref, la1_ref, la2_ref,
                 gg1_ref, gg2_ref, kk_ref, ka_ref, rk_ref, pw_ref, ps_ref, seg_ref,
                 r_out, v_out, an_out, lw_out, kd_out, bb_out, bonus_out, gate_out, yb_out,
                 hext_ref, pext_ref, *, parts, tm, pad, tiles_per_batch, seq_len):
    i = pl.program_id(0)
    tile = lax.rem(i, tiles_per_batch)
    keep_prev = jnp.where(tile == 0, 0.0, 1.0)
    keep_next = jnp.where(tile == tiles_per_batch - 1, 0.0, 1.0)
    g = g_ref[...]
    shift, scale = m_ref[0:1, :], m_ref[1:2, :]
    d_model = xm_ref.shape[1]
    wa = r_out.shape[1]

    hext_ref[0:pad, :] = _modulate(xp_ref[...], g, shift, scale) * keep_prev
    hext_ref[pad:pad + tm, :] = _modulate(xm_ref[...], g, shift, scale)
    hext_ref[pad + tm:, :] = _modulate(xn_ref[...], g, shift, scale) * keep_next
    pext_ref[0:pad, :] = pp_ref[...] * keep_prev
    pext_ref[pad:pad + tm, :] = pm_ref[...]
    pext_ref[pad + tm:, :] = pn_ref[...] * keep_next

    colidx = jnp.bitwise_and(lax.broadcasted_iota(jnp.int32, (tm, 1), 0), GRID_W - 1)

    def shifted(ext_ref, col0, width):
        pw = width // len(parts)
        outs = []
        for q, (off, mask) in enumerate(parts):
            blk = ext_ref[pad + off:pad + off + tm, col0 + q * pw:col0 + (q + 1) * pw]
            if mask == "first_col":
                blk = jnp.where(colidx == 0, 0.0, blk)
            elif mask == "last_col":
                blk = jnp.where(colidx == GRID_W - 1, 0.0, blk)
            outs.append(blk)
        return jnp.concatenate(outs, axis=1)

    h = hext_ref[pad:pad + tm, :]
    hx = shifted(hext_ref, 0, d_model) - h
    x_w = h + hx * mux_ref[0:1, :]
    x_a = h + hx * mux_ref[1:2, :]
    x_g = h + hx * mux_ref[2:3, :]
    zw_mid = jnp.tanh(_dot(x_w, dw1_ref[...]))
    xa_mid = _dot(x_a, la1_ref[...])
    gate_out[...] = _dot(_sigmoid(_dot(x_g, gg1_ref[...])), gg2_ref[...])

    def mixed(n):
        p_n = pext_ref[pad:pad + tm, n * wa:(n + 1) * wa]
        return p_n + (shifted(pext_ref, n * wa, wa) - p_n) * mup_ref[n:n + 1, :]

    r, k, v = mixed(0), mixed(1), mixed(2)
    seg = seg_ref[...]
    kk = k * kk_ref[...]
    kk = kk * lax.rsqrt(jnp.maximum(_dot(kk * kk, seg), 1e-12))
    r_out[...] = r
    v_out[...] = v
    an_out[...] = -kk
    ka = ka_ref[...]
    kd_sum = jnp.zeros_like(k)
    for d in range(2):
        zw = dw0_ref[d:d + 1, :] + _dot(zw_mid, dw2_ref[d])
        lw_out[d] = -math.exp(-0.5) * _sigmoid(zw)
        a_lr = _sigmoid(la0_ref[d:d + 1, :] + _dot(xa_mid, la2_ref[d]))
        kd = k * (1.0 + (a_lr - 1.0) * ka)
        kd_out[d] = kd
        bb_out[d] = kk * a_lr
        kd_sum = kd_sum + kd
    bonus_out[...] = _dot(r * kd_sum * rk_ref[...], seg) * v

    pos = tile * tm + lax.broadcasted_iota(jnp.int32, (tm, 1), 0)
    trow = lax.broadcasted_iota(jnp.int32, (tm, tm + 2 * pad), 0)
    srow = lax.broadcasted_iota(jnp.int32, (tm, tm + 2 * pad), 1) - pad
    gp = wa // len(POOL_WINDOWS)
    ybs = []
    for gi, win in enumerate(POOL_WINDOWS):
        half = win // 2
        c0 = 3 * wa + gi * gp
        band = jnp.where((srow >= trow - half) & (srow < trow + half), 1.0, 0.0).astype(BF16)
        sums = _dot(band, pext_ref[:, c0:c0 + gp])
        cnt = (jnp.minimum(pos + half, seq_len) - jnp.maximum(pos - half, 0)).astype(F32)
        diff = sums / cnt - pext_ref[pad:pad + tm, c0:c0 + gp]
        ybs.append(_dot(diff, pw_ref[gi]))
    yb_out[...] = jnp.concatenate(ybs, axis=1) * ps_ref[...]


def _rwkv_prep(x, proj, g, mod, mod_base, mod_stride, wts, batch, t, grid_mode):
    n, d = x.shape
    wa = d // 2
    pcols = proj.shape[1]
    tm = 256
    pad = GRID_W if grid_mode else 8
    parts = GRID_SHIFT if grid_mode else SEQ_SHIFT
    tpb = t // tm
    hb = tm // pad
    nhb = n // pad
    main = lambda i: (i, 0)
    prev = lambda i: (jnp.maximum(i * hb - 1, 0), 0)
    nxt = lambda i: (jnp.minimum((i + 1) * hb, nhb - 1), 0)
    full2 = lambda i: (0, 0)
    full3 = lambda i: (0, 0, 0)
    in_specs = [pl.BlockSpec((pad, d), prev), pl.BlockSpec((tm, d), main), pl.BlockSpec((pad, d), nxt),
                pl.BlockSpec((pad, pcols), prev), pl.BlockSpec((tm, pcols), main), pl.BlockSpec((pad, pcols), nxt),
                pl.BlockSpec((1, d), full2),
                pl.BlockSpec((None, 6, d), lambda i: (mod_base + (i // tpb) * mod_stride, 0, 0))]
    for a in wts:
        in_specs.append(pl.BlockSpec(a.shape, full2 if a.ndim == 2 else full3))
    one = pl.BlockSpec((tm, wa), main)
    two = pl.BlockSpec((2, tm, wa), lambda i: (0, i, 0))
    sd1 = jax.ShapeDtypeStruct((n, wa), F32)
    sd2 = jax.ShapeDtypeStruct((2, n, wa), F32)
    return pl.pallas_call(
        functools.partial(_prep_kernel, parts=parts, tm=tm, pad=pad, tiles_per_batch=tpb, seq_len=t),
        grid=(n // tm,),
        in_specs=in_specs,
        out_specs=[one, one, one, two, two, two, one, one, one],
        out_shape=[sd1, sd1, sd1, sd2, sd2, sd2, sd1, sd1, sd1],
        scratch_shapes=[pltpu.VMEM((tm + 2 * pad, d), F32), pltpu.VMEM((tm + 2 * pad, pcols), F32)],
        compiler_params=_cparams("parallel"),
        name="rwkv_prep",
    )(x, x, x, proj, proj, proj, g.reshape(1, d), mod, *wts)


def _mix_out_kernel(y0_ref, y1_ref, bonus_ref, gate_ref, yb_ref, x_ref, m_ref, gnw_ref, gnb_ref,
                    seg_ref, wout_ref, o_ref):
    y = y0_ref[...] + y1_ref[...]
    seg = seg_ref[...]
    inv = 1.0 / HEAD_DIM
    yh = y.astype(BF16)
    mu = (jnp.dot(yh, seg, preferred_element_type=F32) + _dot(y - yh.astype(F32), seg)) * inv
    dlt = y - mu
    var = _dot(dlt * dlt, seg) * inv
    yn = dlt * lax.rsqrt(var + GN_EPS) * gnw_ref[...] + gnb_ref[...]
    ya = (yn + bonus_ref[...]) * gate_ref[...]
    cat = jnp.concatenate([ya, yb_ref[...]], axis=1)
    o_ref[...] = x_ref[...] + m_ref[2:3, :] * _dot(cat, wout_ref[...])


def _mix_out(y0, y1, bonus, gate, yb, x, mod, mod_base, mod_stride, gnw, gnb, seg, wout, t, tm):
    n, d = x.shape
    wa = d // 2
    tpb = t // tm
    half = pl.BlockSpec((tm, wa), lambda i: (i, 0))
    full = pl.BlockSpec((tm, d), lambda i: (i, 0))
    const = lambda a: pl.BlockSpec(a.shape, lambda i: (0, 0))
    return pl.pallas_call(
        _mix_out_kernel,
        grid=(n // tm,),
        in_specs=[half, half, half, half, half, full,
                  pl.BlockSpec((None, 6, d), lambda i: (mod_base + (i // tpb) * mod_stride, 0, 0)),
                  const(gnw), const(gnb), const(seg), const(wout)],
        out_specs=full,
        out_shape=jax.ShapeDtypeStruct((n, d), F32),
        input_output_aliases={5: 0},
        compiler_params=_cparams("parallel"),
        name="mix_out",
    )(y0, y1, bonus, gate, yb, x, mod, gnw, gnb, seg, wout)


def _chan_dft_kernel(x_ref, g_ref, m_ref, cs_ref, xc_ref, xs_ref, *, groups):
    h = _modulate(x_ref[...], g_ref[...], m_ref[0:1, :], m_ref[1:2, :])
    gc = h.shape[1] // groups
    cs = cs_ref[...]
    for gi in range(groups):
        res = _dot(h[:, gi * gc:(gi + 1) * gc], cs)
        xc_ref[:, gi * gc:(gi + 1) * gc] = res[:, :gc].astype(xc_ref.dtype)
        xs_ref[:, gi * gc:(gi + 1) * gc] = res[:, gc:].astype(xs_ref.dtype)


def _chan_dft(x, g, mod, mod_base, mod_stride, cs, t, tm, groups, out_dtype):
    n, d = x.shape
    tpb = t // tm
    full = pl.BlockSpec((tm, d), lambda i: (i, 0))
    return pl.pallas_call(
        functools.partial(_chan_dft_kernel, groups=groups),
        grid=(n // tm,),
        in_specs=[full, pl.BlockSpec((1, d), lambda i: (0, 0)),
                  pl.BlockSpec((None, 6, d), lambda i: (mod_base + (i // tpb) * mod_stride, 0, 0)),
                  pl.BlockSpec(cs.shape, lambda i: (0, 0))],
        out_specs=[full, full],
        out_shape=[jax.ShapeDtypeStruct((n, d), out_dtype)] * 2,
        compiler_params=_cparams("parallel"),
        name="chan_dft",
    )(x, g.reshape(1, d), mod, cs)


LANES = 128
SUBLANES = 8


def _dft_stage_a_kernel(m_ref, xc_ref, xs_ref, ar_ref, ai_ref):
    m = m_ref[...]
    n1 = ar_ref.shape[0]
    for v in range(SUBLANES):
        rhs = jnp.concatenate([xc_ref[:, v, :], xs_ref[:, v, :]], axis=0).astype(BF16)
        res = jnp.dot(m, rhs, preferred_element_type=F32)
        ar_ref[:, v, :] = res[:n1]
        ai_ref[:, v, :] = res[n1:]


def _dft_stage_a(m, xc, xs, batch, t):
    n, d = xc.shape
    n1 = t // LANES
    nu = LANES // SUBLANES
    view = (batch, n1, nu, SUBLANES, d)
    blk = pl.BlockSpec((None, n1, None, SUBLANES, d), lambda b, u: (b, 0, u, 0, 0))
    return pl.pallas_call(
        _dft_stage_a_kernel,
        grid=(batch, nu),
        in_specs=[pl.BlockSpec(m.shape, lambda b, u: (0, 0)), blk, blk],
        out_specs=[blk, blk],
        out_shape=[jax.ShapeDtypeStruct(view, F32)] * 2,
        compiler_params=_cparams("parallel", "parallel"),
        name="dft_stage_a",
    )(m, xc.reshape(view), xs.reshape(view))


def _dft_stage_c_kernel(cs_ref, twc_ref, tws_ref, ar_ref, ai_ref, y_ref):
    cs = cs_ref[...]
    for w in range(SUBLANES):
        ar, ai = ar_ref[w], ai_ref[w]
        cw, sw = twc_ref[w], tws_ref[w]
        b = jnp.concatenate([ar * cw + ai * sw, ai * cw - ar * sw], axis=0)
        y_ref[:, w, :] = _dot(cs, b)


def _dft_stage_c(cs, twc, tws, ar, ai, batch, t):
    d = ar.shape[-1]
    n1 = t // LANES
    ng = n1 // SUBLANES
    a_view = (batch, n1, LANES, d)
    a_blk = pl.BlockSpec((None, SUBLANES, LANES, d), lambda b, g: (b, g, 0, 0))
    tw_blk = pl.BlockSpec((SUBLANES, LANES, 1), lambda b, g: (g, 0, 0))
    return pl.pallas_call(
        _dft_stage_c_kernel,
        grid=(batch, ng),
        in_specs=[pl.BlockSpec(cs.shape, lambda b, g: (0, 0)), tw_blk, tw_blk, a_blk, a_blk],
        out_specs=pl.BlockSpec((None, LANES, None, SUBLANES, d), lambda b, g: (b, 0, g, 0, 0)),
        out_shape=jax.ShapeDtypeStruct((batch, LANES, ng, SUBLANES, d), F32),
        compiler_params=_cparams("parallel", "parallel"),
        name="dft_stage_c",
    )(cs, twc, tws, ar.reshape(a_view), ai.reshape(a_view)).reshape(batch * t, d)


def _resid_matmul_kernel(y_ref, x_ref, m_ref, w_ref, o_ref):
    o_ref[...] = x_ref[...] + m_ref[2:3, :] * _dot(y_ref[...], w_ref[...])


def _resid_matmul(y, x, mod, mod_base, mod_stride, w, t, tm):
    n, d = x.shape
    tpb = t // tm
    full = pl.BlockSpec((tm, d), lambda i: (i, 0))
    return pl.pallas_call(
        _resid_matmul_kernel,
        grid=(n // tm,),
        in_specs=[full, full,
                  pl.BlockSpec((None, 6, d), lambda i: (mod_base + (i // tpb) * mod_stride, 0, 0)),
                  pl.BlockSpec(w.shape, lambda i: (0, 0))],
        out_specs=full,
        out_shape=jax.ShapeDtypeStruct((n, d), F32),
        input_output_aliases={1: 0},
        compiler_params=_cparams("parallel"),
        name="resid_matmul",
    )(y, x, mod, w)


def _two_stage_tables(t):
    n1 = t // LANES
    c1, s1 = _trig(n1, n1, n1)
    m = jnp.concatenate([jnp.concatenate([c1, -s1], axis=1), jnp.concatenate([-s1, -c1], axis=1)], axis=0)
    c2, s2 = _trig(LANES, LANES, LANES)
    twc, tws = _trig(n1, LANES, t)
    return ((m * t ** -0.5).astype(BF16), jnp.concatenate([c2, s2], axis=1).astype(BF16),
            twc[:, :, None], tws[:, :, None])


def _time_dft_kernel(ct_ref, st_ref, xc_ref, xs_ref, x_ref, m_ref, wf_ref, o_ref, acc_ref):
    ki = pl.program_id(2)

    @pl.when(ki == 0)
    def _():
        acc_ref[...] = jnp.zeros_like(acc_ref)

    acc_ref[...] += (jnp.dot(ct_ref[...], xc_ref[...], preferred_element_type=F32)
                     - jnp.dot(st_ref[...], xs_ref[...], preferred_element_type=F32))

    @pl.when(ki == pl.num_programs(2) - 1)
    def _():
        o_ref[...] = x_ref[...] + m_ref[2:3, :] * _dot(acc_ref[...], wf_ref[...])


def _time_dft(tabs, xc, xs, x, mod, mod_base, mod_stride, wf, batch, t, tf, tk):
    ct, st = tabs
    n, d = x.shape
    nf, nk = t // tf, t // tk
    return pl.pallas_call(
        _time_dft_kernel,
        grid=(batch, nf, nk),
        in_specs=[pl.BlockSpec((tf, tk), lambda b, f, k: (f, k)),
                  pl.BlockSpec((tf, tk), lambda b, f, k: (f, k)),
                  pl.BlockSpec((tk, d), lambda b, f, k: (b * nk + k, 0)),
                  pl.BlockSpec((tk, d), lambda b, f, k: (b * nk + k, 0)),
                  pl.BlockSpec((tf, d), lambda b, f, k: (b * nf + f, 0)),
                  pl.BlockSpec((None, 6, d), lambda b, f, k: (mod_base + b * mod_stride, 0, 0)),
                  pl.BlockSpec((d, d), lambda b, f, k: (0, 0))],
        out_specs=pl.BlockSpec((tf, d), lambda b, f, k: (b * nf + f, 0)),
        out_shape=jax.ShapeDtypeStruct((n, d), F32),
        scratch_shapes=[pltpu.VMEM((tf, d), F32)],
        input_output_aliases={4: 0},
        compiler_params=_cparams("parallel", "parallel", "arbitrary"),
        name="time_dft",
    )(ct, st, xc, xs, x, mod, wf)


def _trig(rows, cols, period):
    prod = jnp.bitwise_and(jnp.arange(rows, dtype=jnp.int32)[:, None] * jnp.arange(cols, dtype=jnp.int32)[None, :],
                           period - 1)
    ang = prod.astype(F32) * (2.0 * math.pi / period)
    return jnp.cos(ang), jnp.sin(ang)


def _dft_table_kernel(ca_ref, sa_ref, cb_ref, sb_ref, ct_ref, st_ref):
    ca, sa = ca_ref[...], sa_ref[...]
    cb, sb = cb_ref[...], sb_ref[...]
    for j in range(ca.shape[1]):
        a, s = ca[:, j:j + 1], sa[:, j:j + 1]
        ct_ref[:, j * LANES:(j + 1) * LANES] = (a * cb - s * sb).astype(ct_ref.dtype)
        st_ref[:, j * LANES:(j + 1) * LANES] = (s * cb + a * sb).astype(st_ref.dtype)


def _time_tables(t, scale):
    hi_n = t // LANES
    ca, sa = _trig(t, hi_n, hi_n)
    cb, sb = _trig(t, LANES, t)
    tf = min(t, 256)
    small = pl.BlockSpec((tf, hi_n), lambda f: (f, 0))
    lanes = pl.BlockSpec((tf, LANES), lambda f: (f, 0))
    wide = pl.BlockSpec((tf, t), lambda f: (f, 0))
    return pl.pallas_call(
        _dft_table_kernel,
        grid=(t // tf,),
        in_specs=[small, small, lanes, lanes],
        out_specs=[wide, wide],
        out_shape=[jax.ShapeDtypeStruct((t, t), BF16)] * 2,
        compiler_params=_cparams("parallel"),
        name="dft_table",
    )(ca * scale, sa * scale, cb, sb)


def _router_kernel(x_ref, g_ref, m_ref, wr_ref, br_ref, tri_ref, ids_out, gates_out, cnt_out, base_ref):
    i = pl.program_id(0)

    @pl.when(i == 0)
    def _():
        base_ref[...] = jnp.zeros_like(base_ref)

    h2 = _modulate(x_ref[...], g_ref[...], m_ref[3:4, :], m_ref[4:5, :])
    h_hi = h2.astype(BF16)
    hs = (h_hi, (h2 - h_hi.astype(F32)).astype(BF16))
    lt = jnp.zeros((wr_ref.shape[1], h2.shape[0]), F32)
    for ia, ib in ((0, 0), (0, 1), (1, 0)):
        lt = lt + lax.dot_general(wr_ref[ib], hs[ia], (((1,), (1,)), ((), ())), preferred_element_type=F32)
    logits = lt[0:ROUTE_ROWS] + br_ref[0:ROUTE_ROWS, :]

    tm = logits.shape[1]
    row_i = lax.broadcasted_iota(jnp.int32, (ROUTE_ROWS, tm), 0)
    row = row_i.astype(F32)
    neg = -jnp.inf
    far = float(ROUTE_ROWS)
    gmask = row_i < MOE_GROUPS
    lc = jnp.where(gmask, logits, neg)
    mc = jnp.max(lc, axis=0, keepdims=True)
    sc = jnp.sum(jnp.where(gmask, jnp.exp(logits - mc), 0.0), axis=0, keepdims=True)
    g_val = 1.0 / sc
    g_idx = jnp.min(jnp.where(lc == mc, row, far), axis=0, keepdims=True)
    lgroup = jnp.where(row_i >= MOE_GROUPS, jnp.right_shift(row_i - MOE_GROUPS, 3), -1).astype(F32)
    sel = lgroup == g_idx
    lf = jnp.where(sel, logits, neg)
    m1 = jnp.max(lf, axis=0, keepdims=True)
    ef = jnp.where(sel, jnp.exp(logits - m1), 0.0)
    p = ef / jnp.sum(ef, axis=0, keepdims=True)
    p1 = jnp.where(sel, p, -1.0)
    v1 = jnp.max(p1, axis=0, keepdims=True)
    i1 = jnp.min(jnp.where(p1 == v1, row, far), axis=0, keepdims=True)
    p2 = jnp.where(row == i1, -1.0, p1)
    v2 = jnp.max(p2, axis=0, keepdims=True)
    i2 = jnp.min(jnp.where(p2 == v2, row, far), axis=0, keepdims=True)
    denom = v1 + v2
    gate1 = g_val * v1 / denom
    gate2 = g_val * v2 / denom

    tri = tri_ref[...]
    oh1 = jnp.where(row == i1, 1.0, 0.0)
    oh2 = jnp.where(row == i2, 1.0, 0.0)
    base = base_ref[:, 0:1]
    tot1 = jnp.sum(oh1, axis=1, keepdims=True)
    tot2 = jnp.sum(oh2, axis=1, keepdims=True)
    c1 = jnp.dot(oh1.astype(BF16), tri, preferred_element_type=F32)
    c2 = jnp.dot(oh2.astype(BF16), tri, preferred_element_type=F32)
    rank1 = jnp.sum(oh1 * (base + c1), axis=0, keepdims=True)
    rank2 = jnp.sum(oh2 * (base + tot1 + c2), axis=0, keepdims=True)
    base = jnp.broadcast_to(base + tot1 + tot2, base_ref.shape)
    base_ref[...] = base
    cnt_out[...] = base
    zrow = jnp.zeros((4, tm), F32)
    ids_out[...] = jnp.concatenate([i1 - MOE_GROUPS, i2 - MOE_GROUPS, rank1, rank2, zrow], axis=0).astype(jnp.int32)
    gates_out[...] = jnp.concatenate([gate1, gate2, zrow, zrow[0:2]], axis=0)


def _router(x, g, mod, mod_base, mod_stride, wr3, br, tri, t, tm):
    n, d = x.shape
    tpb = t // tm
    full = pl.BlockSpec((tm, d), lambda i: (i, 0))
    lanes = pl.BlockSpec((8, tm), lambda i: (0, i))
    cnt = pl.BlockSpec((ROUTE_ROWS, LANES), lambda i: (0, 0))
    return pl.pallas_call(
        _router_kernel,
        grid=(n // tm,),
        in_specs=[full, pl.BlockSpec((1, d), lambda i: (0, 0)),
                  pl.BlockSpec((None, 6, d), lambda i: (mod_base + (i // tpb) * mod_stride, 0, 0)),
                  pl.BlockSpec(wr3.shape, lambda i: (0, 0, 0)),
                  pl.BlockSpec(br.shape, lambda i: (0, 0)),
                  pl.BlockSpec((tm, tm), lambda i: (0, 0))],
        out_specs=[lanes, lanes, cnt],
        out_shape=[jax.ShapeDtypeStruct((8, n), jnp.int32),
                   jax.ShapeDtypeStruct((8, n), F32), jax.ShapeDtypeStruct((ROUTE_ROWS, LANES), F32)],
        scratch_shapes=[pltpu.VMEM((ROUTE_ROWS, LANES), F32)],
        compiler_params=_cparams("arbitrary"),
        name="moe_router",
    )(x, g.reshape(1, d), mod, wr3, br, tri)


def _row_copy(src, src_row, dst, dst_row, sem):
    return pltpu.make_async_copy(src.at[pl.ds(src_row, 1)], dst.at[pl.ds(dst_row, 1)], sem)


def _dispatch_kernel(dest_ref, zb_ref, x_ref, g_ref, m_ref, xbuf_ref, h_ref, zero_ref, sem, zsem, *, tm):
    step = pl.program_id(0)
    ntok = pl.num_programs(0) * tm

    def zero_copy(s):
        start = pl.multiple_of(zb_ref[s] * MOE_BLOCK, MOE_BLOCK)
        return pltpu.make_async_copy(zero_ref, xbuf_ref.at[pl.ds(start, MOE_BLOCK)], zsem)

    @pl.when(step == 0)
    def _():
        zero_ref[...] = jnp.zeros_like(zero_ref)
        for s in range(zb_ref.shape[0]):
            @pl.when(zb_ref[s] >= 0)
            def _():
                zero_copy(s).start()
        for s in range(zb_ref.shape[0]):
            @pl.when(zb_ref[s] >= 0)
            def _():
                zero_copy(s).wait()

    h_ref[...] = _modulate(x_ref[...], g_ref[...], m_ref[3:4, :], m_ref[4:5, :])
    base = step * tm

    def body(r, carry):
        tok = base + r
        for j in range(2):
            _row_copy(h_ref, r, xbuf_ref, dest_ref[j * ntok + tok], sem).start(priority=j)
        return carry

    lax.fori_loop(0, tm, body, 0, unroll=8)

    def drain(r, carry):
        for j in range(2):
            _row_copy(h_ref, 0, xbuf_ref, 0, sem).wait()
        return carry

    lax.fori_loop(0, tm, drain, 0, unroll=8)


def _dispatch(dest, zblocks, x, g, mod, mod_base, mod_stride, rows, t, tm):
    n, d = x.shape
    tpb = t // tm
    return pl.pallas_call(
        functools.partial(_dispatch_kernel, tm=tm),
        grid_spec=pltpu.PrefetchScalarGridSpec(
            num_scalar_prefetch=2,
            grid=(n // tm,),
            in_specs=[pl.BlockSpec((tm, d), lambda i, dr, zb: (i, 0)),
                      pl.BlockSpec((1, d), lambda i, dr, zb: (0, 0)),
                      pl.BlockSpec((None, 6, d), lambda i, dr, zb: (mod_base + (i // tpb) * mod_stride, 0, 0))],
            out_specs=pl.BlockSpec(memory_space=pl.ANY),
            scratch_shapes=[pltpu.VMEM((tm, d), F32), pltpu.VMEM((MOE_BLOCK, d), F32),
                            pltpu.SemaphoreType.DMA, pltpu.SemaphoreType.DMA],
        ),
        out_shape=jax.ShapeDtypeStruct((rows, d), F32),
        compiler_params=_cparams("arbitrary"),
        name="moe_dispatch",
    )(dest, zblocks, x, g.reshape(1, d), mod)


def _expert_kernel(be_ref, na_ref, x_ref, w1_ref, w3_ref, w2_ref, o_ref):
    active = pl.program_id(0) < na_ref[0]

    @pl.when(active)
    def _():
        xb = x_ref[...].astype(BF16)
        h1 = jnp.dot(xb, w1_ref[...].astype(BF16), preferred_element_type=F32)
        h3 = jnp.dot(xb, w3_ref[...].astype(BF16), preferred_element_type=F32)
        act = h1 * _sigmoid(h1) * h3
        o_ref[...] = _dot(act, w2_ref[...])

    @pl.when(jnp.logical_not(active))
    def _():
        o_ref[...] = jnp.zeros_like(o_ref)


def _experts(block_e, nact, xbuf, w1, w3, w2, layer):
    rows, d = xbuf.shape
    nb = rows // MOE_BLOCK
    de = w1.shape[3]
    blk = lambda i, be, na: (jnp.minimum(i, na[0] - 1), 0)
    wmap = lambda i, be, na: (layer, be[jnp.minimum(i, na[0] - 1)], 0, 0)
    return pl.pallas_call(
        _expert_kernel,
        grid_spec=pltpu.PrefetchScalarGridSpec(
            num_scalar_prefetch=2,
            grid=(nb,),
            in_specs=[pl.BlockSpec((MOE_BLOCK, d), blk),
                      pl.BlockSpec((None, None, d, de), wmap),
                      pl.BlockSpec((None, None, d, de), wmap),
                      pl.BlockSpec((None, None, de, d), wmap)],
            out_specs=pl.BlockSpec((MOE_BLOCK, d), lambda i, be, na: (i, 0)),
        ),
        out_shape=jax.ShapeDtypeStruct((rows, d), F32),
        compiler_params=_cparams("arbitrary"),
        name="moe_experts",
    )(block_e, nact, xbuf, w1, w3, w2)


def _combine_kernel(dest_ref, y_ref, gates_ref, x_ref, m_ref, fn_ref, o_ref, rows_ref, sem, *, tm, final):
    base = pl.program_id(0) * tm
    ntok = pl.num_programs(0) * tm

    def body(r, carry):
        tok = base + r
        for j in range(2):
            _row_copy(y_ref, dest_ref[j * ntok + tok], rows_ref.at[j], r, sem).start(priority=j)
        return carry

    lax.fori_loop(0, tm, body, 0, unroll=8)

    def drain(r, carry):
        for j in range(2):
            _row_copy(y_ref, 0, rows_ref.at[j], 0, sem).wait()
        return carry

    lax.fori_loop(0, tm, drain, 0, unroll=8)

    gates = gates_ref[...]
    y = gates[:, 0:1] * rows_ref[0] + gates[:, 1:2] * rows_ref[1]
    out = x_ref[...] + m_ref[5:6, :] * y
    if final:
        ms = jnp.mean(out * out, axis=-1, keepdims=True)
        out = out * lax.rsqrt(ms + NORM_EPS) * fn_ref[...]
    o_ref[...] = out


def _combine(dest, ybuf, gates, x, mod, mod_base, mod_stride, fnorm, t, tm, final):
    n, d = x.shape
    tpb = t // tm
    full = pl.BlockSpec((tm, d), lambda i, dr: (i, 0))
    return pl.pallas_call(
        functools.partial(_combine_kernel, tm=tm, final=final),
        grid_spec=pltpu.PrefetchScalarGridSpec(
            num_scalar_prefetch=1,
            grid=(n // tm,),
            in_specs=[pl.BlockSpec(memory_space=pl.ANY),
                      pl.BlockSpec((tm, 2), lambda i, dr: (i, 0)),
                      full,
                      pl.BlockSpec((None, 6, d), lambda i, dr: (mod_base + (i // tpb) * mod_stride, 0, 0)),
                      pl.BlockSpec((1, d), lambda i, dr: (0, 0))],
            out_specs=full,
            scratch_shapes=[pltpu.VMEM((2, tm, d), F32), pltpu.SemaphoreType.DMA],
        ),
        out_shape=jax.ShapeDtypeStruct((n, d), F32),
        input_output_aliases={3: 0},
        compiler_params=_cparams("arbitrary"),
        name="moe_combine",
    )(dest, ybuf, gates, x, mod, fnorm)


def _moe(x, g, mod, mod_base, mod_stride, rt, w1, w3, w2, layer, fnorm, t, tm, final):
    n, d = x.shape
    n_exp = w1.shape[1]
    wr3, br, tri = rt
    ids, gates, counts = _router(x, g, mod, mod_base, mod_stride, wr3, br, tri, t, tm)
    counts = counts[MOE_GROUPS:MOE_GROUPS + n_exp, 0].astype(jnp.int32)
    padded = (counts + MOE_BLOCK - 1) // MOE_BLOCK * MOE_BLOCK
    pend = jnp.cumsum(padded)
    pstart = pend - padded
    onehot = ids[0:2, :, None] == jnp.arange(n_exp, dtype=jnp.int32)
    dest = (jnp.sum(jnp.where(onehot, pstart, 0), axis=-1) + ids[2:4]).reshape(2 * n)
    gates = gates[0:2].T
    nb = (2 * n) // MOE_BLOCK + n_exp
    blk_start = jnp.arange(nb, dtype=jnp.int32) * MOE_BLOCK
    block_e = jnp.minimum(jnp.sum((pend[None, :] <= blk_start[:, None]).astype(jnp.int32), axis=1), n_exp - 1)
    nact = pend[-1:] // MOE_BLOCK
    partial = jnp.where(padded > counts, pend // MOE_BLOCK - 1, -1)
    tail = nact + jnp.arange(n_exp, dtype=jnp.int32)
    zblocks = jnp.concatenate([partial, jnp.where(tail < nb, tail, -1)]).astype(jnp.int32)
    xbuf = _dispatch(dest, zblocks, x, g, mod, mod_base, mod_stride, nb * MOE_BLOCK, t, tm)
    ybuf = _experts(block_e, nact, xbuf, w1, w3, w2, layer)
    return _combine(dest, ybuf, gates, x, mod, mod_base, mod_stride, fnorm, t, tm, final)


def _seg_ones(width):
    idx = np.arange(width) // HEAD_DIM
    return jnp.asarray((idx[:, None] == idx[None, :]).astype(np.float32), dtype=BF16)


def _even_weights(j, mu_x, mu_p, decay_w0, decay_w1, decay_w2, lr_a0, lr_a1, lr_a2, gate_g1, gate_g2,
                  k_k, k_a, r_k, pool_w, pool_scale, seg):
    wa = mu_p.shape[-1]

    def cat1(w):
        return jnp.concatenate([w[0], w[1]], axis=1).astype(BF16)

    def pad2(w):
        z = jnp.zeros_like(w[0])
        return jnp.stack([jnp.concatenate([w[0], z], axis=0), jnp.concatenate([z, w[1]], axis=0)]).astype(BF16)

    return (mu_x[j], mu_p[j], decay_w0[j], cat1(decay_w1[j]), pad2(decay_w2[j]),
            lr_a0[j], cat1(lr_a1[j]), pad2(lr_a2[j]),
            gate_g1[j].astype(BF16), gate_g2[j].astype(BF16),
            k_k[j].reshape(1, wa), k_a[j].reshape(1, wa), r_k[j].reshape(1, wa),
            pool_w[j].astype(BF16), pool_scale[j].reshape(1, wa), seg)


def kernel(x, c, ctx, c_ctx, ada_w, ada_b, norm_mix, norm_ffn, w_in, mu_x, mu_p, decay_w0, decay_w1, decay_w2, lr_a0, lr_a1, lr_a2, gate_g1, gate_g2, k_k, k_a, r_k, gn_w, gn_b, pool_w, pool_scale, w_out, w_fourier, router_c, router_c_b, router_f, router_f_b, moe_w1, moe_w3, moe_w2, final_norm):
    batch, t, d = x.shape
    tc = ctx.shape[1]
    depth = ada_w.shape[0]
    wa = d // 2
    n, ncx = batch * t, batch * tc
    assert batch <= 4 and t % 512 == 0 and tc % 256 == 0 and d % 512 == 0

    cond8 = jnp.zeros((8, d), F32).at[:batch].set(c).at[4].set(c_ctx)
    mod_all = _ada(cond8, ada_w, ada_b).reshape(depth, 8, 6, d)
    seg = _seg_ones(wa)
    fnorm = final_norm.reshape(1, d)

    fgroups = 4
    gc = d // fgroups
    cc, sc = _trig(gc, gc, gc)
    cs = (jnp.concatenate([cc, sc], axis=1) * gc ** -0.5).astype(BF16)
    stage_m, stage_cs, stage_twc, stage_tws = _two_stage_tables(t)
    tab_ctx = _time_tables(tc, tc ** -0.5)

    n_exp = moe_w1.shape[1]
    tri = {tm: jnp.asarray(np.triu(np.ones((tm, tm), np.float32), 1), dtype=BF16) for tm in (512, 256)}

    lat = x.reshape(n, d)
    cx = ctx.reshape(ncx, d)
    last_read = 2 * ((depth - 1) // 2)
    npair = wa // PAIR

    for i in range(depth):
        ctx_in = i <= last_read
        ctx_out = i < last_read
        mod = mod_all[i]
        j = i // 2
        if i % 2 == 0:
            wts = _even_weights(j, mu_x, mu_p, decay_w0, decay_w1, decay_w2, lr_a0, lr_a1, lr_a2,
                                gate_g1, gate_g2, k_k, k_a, r_k, pool_w, pool_scale, seg)
            w_in_b = w_in[j].astype(BF16)
            w_out_b = w_out[j].astype(BF16)
            gnw, gnb = gn_w[j].reshape(1, wa), gn_b[j].reshape(1, wa)
            pc = _proj(cx, norm_mix[i], mod, w_in_b, tc, 4, 0, 0, 256)
            fc = _rwkv_prep(cx, pc, norm_mix[i], mod, 4, 0, wts, batch, tc, False)
            ryc, pqc = _wkv_chunks(*fc[:6], batch, tc)
            g0 = jnp.zeros((2, batch, npair, PAIR, PAIR), F32)
            yc0, yc1, gctx = _wkv_scan(ryc, pqc, g0)
            pl_ = _proj(lat, norm_mix[i], mod, w_in_b, t, 0, 1, 0, 512)
            fl = _rwkv_prep(lat, pl_, norm_mix[i], mod, 0, 1, wts, batch, t, True)
            ryl, pql = _wkv_chunks(*fl[:6], batch, t)
            yl0, yl1, _ = _wkv_scan(ryl, pql, gctx)
            lat = _mix_out(yl0.reshape(n, wa), yl1.reshape(n, wa), fl[6], fl[7], fl[8], lat, mod, 0, 1,
                           gnw, gnb, seg, w_out_b, t, 512)
            if ctx_out:
                cx = _mix_out(yc0.reshape(ncx, wa), yc1.reshape(ncx, wa), fc[6], fc[7], fc[8], cx, mod, 4, 0,
                              gnw, gnb, seg, w_out_b, tc, 256)
        else:
            wf_b = w_fourier[j].astype(BF16)
            xc, xs = _chan_dft(lat, norm_mix[i], mod, 0, 1, cs, t, 512, fgroups, F32)
            ar, ai = _dft_stage_a(stage_m, xc, xs, batch, t)
            yf = _dft_stage_c(stage_cs, stage_twc, stage_tws, ar, ai, batch, t)
            lat = _resid_matmul(yf, lat, mod, 0, 1, wf_b, t, 512)
            if ctx_out:
                xcc, xsc = _chan_dft(cx, norm_mix[i], mod, 4, 0, cs, tc, 256, fgroups, BF16)
                cx = _time_dft(tab_ctx, xcc, xsc, cx, mod, 4, 0, wf_b, batch, tc, tc, tc)
        wr = jnp.zeros((LANES, d), F32).at[:MOE_GROUPS].set(router_c[i].T).at[MOE_GROUPS:MOE_GROUPS + n_exp].set(router_f[i].T)
        br = jnp.zeros((LANES, 1), F32).at[:MOE_GROUPS, 0].set(router_c_b[i]).at[MOE_GROUPS:MOE_GROUPS + n_exp, 0].set(router_f_b[i])
        wr3 = jnp.stack(_split3(wr))
        final = i == depth - 1
        lat = _moe(lat, norm_ffn[i], mod, 0, 1, (wr3, br, tri[512]), moe_w1, moe_w3, moe_w2, i, fnorm,
                   t, 512, final)
        if ctx_out:
            cx = _moe(cx, norm_ffn[i], mod, 4, 0, (wr3, br, tri[256]), moe_w1, moe_w3, moe_w2, i, fnorm,
                      tc, 256, False)
    return lat.reshape(batch, t, d)
```

```python
import functools
import math

import jax
import jax.numpy as jnp
import numpy as np
from jax import lax
from jax.experimental import pallas as pl
from jax.experimental.pallas import tpu as pltpu

F32 = jnp.float32
BF16 = jnp.bfloat16

GRID_W = 64
HEAD_DIM = 64
CHUNK = 64
PAIR = 2 * HEAD_DIM
NORM_EPS = 1e-6
GN_EPS = 64e-5
POOL_WINDOWS = (2, 4, 8, 16)
MOE_GROUPS = 4
ROUTE_ROWS = 40
EXPERTS_PER_GROUP = 8
MOE_BLOCK = 512
VMEM_LIMIT = 56 * 1024 * 1024


def _cparams(*sem):
    return pltpu.CompilerParams(dimension_semantics=tuple(sem), vmem_limit_bytes=VMEM_LIMIT)


def _dot(a, b):
    return jnp.dot(a.astype(BF16), b.astype(BF16), preferred_element_type=F32)


def _dot_nt(a, b):
    return lax.dot_general(a.astype(BF16), b.astype(BF16), (((1,), (1,)), ((), ())),
                           preferred_element_type=F32)


def _dot_tn(a, b):
    return lax.dot_general(a.astype(BF16), b.astype(BF16), (((0,), (0,)), ((), ())),
                           preferred_element_type=F32)


def _split3(x):
    hi = x.astype(BF16)
    r1 = x - hi.astype(F32)
    mid = r1.astype(BF16)
    lo = (r1 - mid.astype(F32)).astype(BF16)
    return hi, mid, lo


def _sigmoid(x):
    return 1.0 / (1.0 + jnp.exp(-x))


def _modulate(x, g, shift, scale):
    ms = jnp.mean(x * x, axis=-1, keepdims=True)
    return x * lax.rsqrt(ms + NORM_EPS) * g * (1.0 + scale) + shift


def _ada_kernel(c_ref, w_ref, b_ref, o_ref):
    c = c_ref[...]
    s = c * _sigmoid(c)
    o_ref[...] = _dot(s, w_ref[...]) + b_ref[...]


def _ada(cond8, ada_w, ada_b):
    depth, d, n6 = ada_w.shape
    tn = 1536
    return pl.pallas_call(
        _ada_kernel,
        grid=(depth, n6 // tn),
        in_specs=[pl.BlockSpec((8, d), lambda l, j: (0, 0)),
                  pl.BlockSpec((None, d, tn), lambda l, j: (l, 0, j)),
                  pl.BlockSpec((None, 1, tn), lambda l, j: (l, 0, j))],
        out_specs=pl.BlockSpec((None, 8, tn), lambda l, j: (l, 0, j)),
        out_shape=jax.ShapeDtypeStruct((depth, 8, n6), F32),
        compiler_params=_cparams("parallel", "parallel"),
        name="ada_mod",
    )(cond8, ada_w, ada_b.reshape(depth, 1, n6))


def _proj_kernel(x_ref, g_ref, m_ref, w_ref, o_ref, *, shift_row, scale_row):
    h = _modulate(x_ref[...], g_ref[...], m_ref[shift_row:shift_row + 1, :], m_ref[scale_row:scale_row + 1, :])
    o_ref[...] = _dot(h, w_ref[...]).astype(o_ref.dtype)


def _proj(x, g, mod, w, rows_per_batch, mod_base, mod_stride, shift_row, tm, out_dtype=F32):
    n, d = x.shape
    nout = w.shape[1]
    tpb = rows_per_batch // tm
    return pl.pallas_call(
        functools.partial(_proj_kernel, shift_row=shift_row, scale_row=shift_row + 1),
        grid=(n // tm,),
        in_specs=[pl.BlockSpec((tm, d), lambda i: (i, 0)),
                  pl.BlockSpec((1, d), lambda i: (0, 0)),
                  pl.BlockSpec((None, 6, d), lambda i: (mod_base + (i // tpb) * mod_stride, 0, 0)),
                  pl.BlockSpec((d, nout), lambda i: (0, 0))],
        out_specs=pl.BlockSpec((tm, nout), lambda i: (i, 0)),
        out_shape=jax.ShapeDtypeStruct((n, nout), out_dtype),
        compiler_params=_cparams("parallel"),
        name="mod_proj",
    )(x, g.reshape(1, d), mod, w)


def _wkv_chunk_kernel(r_ref, v_ref, an_ref, lw_ref, kd_ref, bb_ref, ry_ref, pq_ref):
    d = pl.program_id(0)
    width = lw_ref.shape[1]
    nsub = lw_ref.shape[0] // CHUNK
    row = lax.broadcasted_iota(jnp.int32, (CHUNK, CHUNK), 0)
    col = lax.broadcasted_iota(jnp.int32, (CHUNK, CHUNK), 1)
    sgn = 1 - 2 * d
    tri = jnp.where((row - col) * sgn >= 0, 1.0, 0.0).astype(BF16)

    def chunk_feats(sc):
        rows = slice(sc * CHUNK, (sc + 1) * CHUNK)
        lw = lw_ref[rows, :]
        hi, mid, lo = _split3(lw)
        csum = (jnp.dot(tri, hi, preferred_element_type=F32)
                + jnp.dot(tri, mid, preferred_element_type=F32)
                + jnp.dot(tri, lo, preferred_element_type=F32))
        e_neg = jnp.exp(-csum)
        return (an_ref[rows, :] * jnp.exp(csum - lw), bb_ref[rows, :] * e_neg, kd_ref[rows, :] * e_neg,
                r_ref[rows, :] * jnp.exp(csum), v_ref[rows, :], jnp.exp(jnp.sum(lw, axis=0, keepdims=True)))

    feats = [chunk_feats(sc) for sc in range(nsub)]

    lane = lax.broadcasted_iota(jnp.int32, (1, PAIR), 1)
    head0 = lane < HEAD_DIM
    i2 = lax.broadcasted_iota(jnp.int32, (PAIR, PAIR), 0)
    j2 = lax.broadcasted_iota(jnp.int32, (PAIR, PAIR), 1)
    ti = jnp.bitwise_and(i2, CHUNK - 1)
    tj = jnp.bitwise_and(j2, CHUNK - 1)
    strict = (ti - tj) * sgn > 0
    incl = (ti - tj) * sgn >= 0
    eye = (i2 == j2).astype(F32)

    def stack(x):
        return jnp.concatenate([jnp.where(head0, x, 0.0), jnp.where(head0, 0.0, x)], axis=0)

    units = [(sc, p) for sc in range(nsub) for p in range(width // PAIR)]
    us = range(len(units))
    sls = [slice(p * PAIR, (p + 1) * PAIR) for _, p in units]
    a_s, b_s, k_s, r_s, v_s = ([stack(feats[sc][f][:, sls[u]]) for u, (sc, _) in enumerate(units)]
                               for f in range(5))
    m1 = [_dot_nt(jnp.concatenate([a_s[u], r_s[u]], axis=0), jnp.concatenate([b_s[u], k_s[u]], axis=0))
          for u in us]
    npow = [jnp.where(strict, m[:PAIR, :PAIR], 0.0) for m in m1]
    a_ak = [jnp.where(strict, m[:PAIR, PAIR:], 0.0) for m in m1]
    a_rb = [jnp.where(incl, m[PAIR:, :PAIR], 0.0) for m in m1]
    a_rk = [jnp.where(incl, m[PAIR:, PAIR:], 0.0) for m in m1]
    x = [jnp.concatenate([a_s[u], _dot(a_ak[u], v_s[u])], axis=1) for u in us]
    steps = int(math.log2(CHUNK))
    for it in range(steps):
        x = [x[u] + _dot(npow[u], x[u]) for u in us]
        if it + 1 < steps:
            npow = [_dot(npow[u], npow[u]) for u in us]
    for u, (sc, p) in enumerate(units):
        ry = jnp.concatenate([r_s[u], _dot(a_rk[u], v_s[u])], axis=1) + _dot(a_rb[u], x[u])
        ry_ref[sc, p] = ry.astype(ry_ref.dtype)
    for u, (sc, p) in enumerate(units):
        ge = feats[sc][5][:, sls[u]]
        pt = (eye + _dot_tn(x[u][:, :PAIR], b_s[u])) * ge
        qt = (_dot_tn(x[u][:, PAIR:], b_s[u]) + _dot_tn(v_s[u], k_s[u])) * ge
        pq_ref[sc, p] = jnp.concatenate([pt, qt], axis=0).astype(pq_ref.dtype)


def _wkv_chunks(r, v, an, lw, kd, bb, batch, t):
    n, w = r.shape
    nc = t // CHUNK
    npair = w // PAIR
    nsub = 4
    ns = nc // nsub
    shared = pl.BlockSpec((nsub * CHUNK, w), lambda d, b, c: (b * ns + c, 0))
    perdir = pl.BlockSpec((None, nsub * CHUNK, w), lambda d, b, c: (d, b * ns + c, 0))
    return pl.pallas_call(
        _wkv_chunk_kernel,
        grid=(2, batch, ns),
        in_specs=[shared, shared, shared, perdir, perdir, perdir],
        out_specs=[pl.BlockSpec((None, None, nsub, npair, PAIR, 2 * PAIR), lambda d, b, c: (d, b, c, 0, 0, 0)),
                   pl.BlockSpec((None, None, nsub, npair, 2 * PAIR, PAIR), lambda d, b, c: (d, b, c, 0, 0, 0))],
        out_shape=[jax.ShapeDtypeStruct((2, batch, nc, npair, PAIR, 2 * PAIR), BF16),
                   jax.ShapeDtypeStruct((2, batch, nc, npair, 2 * PAIR, PAIR), BF16)],
        compiler_params=_cparams("parallel", "parallel", "parallel"),
        name="wkv_chunks",
    )(r, v, an, lw, kd, bb)


def _wkv_scan_kernel(ryf_ref, ryb_ref, pqf_ref, pqb_ref, g0_ref, yf_ref, yb_ref, gout_ref, g_ref):
    pos = pl.program_id(0)

    @pl.when(pos == 0)
    def _():
        g_ref[...] = g0_ref[...]

    batch, npair = g_ref.shape[1], g_ref.shape[2]
    for d, (ry_ref, pq_ref, y_ref) in enumerate(((ryf_ref, pqf_ref, yf_ref), (ryb_ref, pqb_ref, yb_ref))):
        for b in range(batch):
            for p in range(npair):
                g = g_ref[d, b, p]
                ry = ry_ref[b, p]
                pq = pq_ref[b, p]
                ys = _dot_nt(ry[:, :PAIR], g) + ry[:, PAIR:].astype(F32)
                y_ref[b, :, p * PAIR:(p + 1) * PAIR] = ys[:CHUNK] + ys[CHUNK:]
                g_ref[d, b, p] = _dot(g, pq[:PAIR]) + pq[PAIR:].astype(F32)

    @pl.when(pos == pl.num_programs(0) - 1)
    def _():
        gout_ref[...] = g_ref[...]


def _wkv_scan(ry, pq, g0):
    _, batch, nc, npair, _, _ = ry.shape
    w = npair * PAIR
    t = nc * CHUNK
    ry_f = pl.BlockSpec((None, batch, None, npair, PAIR, 2 * PAIR), lambda s: (0, 0, s, 0, 0, 0))
    ry_b = pl.BlockSpec((None, batch, None, npair, PAIR, 2 * PAIR), lambda s: (1, 0, nc - 1 - s, 0, 0, 0))
    pq_f = pl.BlockSpec((None, batch, None, npair, 2 * PAIR, PAIR), lambda s: (0, 0, s, 0, 0, 0))
    pq_b = pl.BlockSpec((None, batch, None, npair, 2 * PAIR, PAIR), lambda s: (1, 0, nc - 1 - s, 0, 0, 0))
    gspec = pl.BlockSpec((2, batch, npair, PAIR, PAIR), lambda s: (0, 0, 0, 0, 0))
    return pl.pallas_call(
        _wkv_scan_kernel,
        grid=(nc,),
        in_specs=[ry_f, ry_b, pq_f, pq_b, gspec],
        out_specs=[pl.BlockSpec((batch, CHUNK, w), lambda s: (0, s, 0)),
                   pl.BlockSpec((batch, CHUNK, w), lambda s: (0, nc - 1 - s, 0)),
                   gspec],
        out_shape=[jax.ShapeDtypeStruct((batch, t, w), F32),
                   jax.ShapeDtypeStruct((batch, t, w), F32),
                   jax.ShapeDtypeStruct((2, batch, npair, PAIR, PAIR), F32)],
        scratch_shapes=[pltpu.VMEM((2, batch, npair, PAIR, PAIR), F32)],
        compiler_params=_cparams("arbitrary"),
        name="wkv_scan",
    )(ry, ry, pq, pq, g0)


GRID_SHIFT = ((-1, "first_col"), (1, "last_col"), (-GRID_W, None), (GRID_W, None))
SEQ_SHIFT = ((-1, None), (1, None))


def _prep_kernel(xp_ref, xm_ref, xn_ref, pp_ref, pm_ref, pn_ref, g_ref, m_ref,
                 mux_ref, mup_ref, dw0_ref, dw1_ref, dw2_ref, la0_ref, la1_ref, la2_ref,
                 gg1_ref, gg2_ref, kk_ref, ka_ref, rk_ref, pw_ref, ps_ref, seg_ref,
                 r_out, v_out, an_out, lw_out, kd_out, bb_out, bonus_out, gate_out, yb_out,
                 hext_ref, pext_ref, *, parts, tm, pad, tiles_per_batch, seq_len):
    i = pl.program_id(0)
    tile = lax.rem(i, tiles_per_batch)
    keep_prev = jnp.where(tile == 0, 0.0, 1.0)
    keep_next = jnp.where(tile == tiles_per_batch - 1, 0.0, 1.0)
    g = g_ref[...]
    shift, scale = m_ref[0:1, :], m_ref[1:2, :]
    d_model = xm_ref.shape[1]
    wa = r_out.shape[1]

    hext_ref[0:pad, :] = _modulate(xp_ref[...], g, shift, scale) * keep_prev
    hext_ref[pad:pad + tm, :] = _modulate(xm_ref[...], g, shift, scale)
    hext_ref[pad + tm:, :] = _modulate(xn_ref[...], g, shift, scale) * keep_next
    pext_ref[0:pad, :] = pp_ref[...] * keep_prev
    pext_ref[pad:pad + tm, :] = pm_ref[...]
    pext_ref[pad + tm:, :] = pn_ref[...] * keep_next

    colidx = jnp.bitwise_and(lax.broadcasted_iota(jnp.int32, (tm, 1), 0), GRID_W - 1)

    def shifted(ext_ref, col0, width):
        pw = width // len(parts)
        outs = []
        for q, (off, mask) in enumerate(parts):
            blk = ext_ref[pad + off:pad + off + tm, col0 + q * pw:col0 + (q + 1) * pw]
            if mask == "first_col":
                blk = jnp.where(colidx == 0, 0.0, blk)
            elif mask == "last_col":
                blk = jnp.where(colidx == GRID_W - 1, 0.0, blk)
            outs.append(blk)
        return jnp.concatenate(outs, axis=1)

    h = hext_ref[pad:pad + tm, :]
    hx = shifted(hext_ref, 0, d_model) - h
    x_w = h + hx * mux_ref[0:1, :]
    x_a = h + hx * mux_ref[1:2, :]
    x_g = h + hx * mux_ref[2:3, :]
    zw_mid = jnp.tanh(_dot(x_w, dw1_ref[...]))
    xa_mid = _dot(x_a, la1_ref[...])
    gate_out[...] = _dot(_sigmoid(_dot(x_g, gg1_ref[...])), gg2_ref[...])

    def mixed(n):
        p_n = pext_ref[pad:pad + tm, n * wa:(n + 1) * wa]
        return p_n + (shifted(pext_ref, n * wa, wa) - p_n) * mup_ref[n:n + 1, :]

    r, k, v = mixed(0), mixed(1), mixed(2)
    seg = seg_ref[...]
    kk = k * kk_ref[...]
    kk = kk * lax.rsqrt(jnp.maximum(_dot(kk * kk, seg), 1e-12))
    r_out[...] = r
    v_out[...] = v
    an_out[...] = -kk
    ka = ka_ref[...]
    kd_sum = jnp.zeros_like(k)
    for d in range(2):
        zw = dw0_ref[d:d + 1, :] + _dot(zw_mid, dw2_ref[d])
        lw_out[d] = -math.exp(-0.5) * _sigmoid(zw)
        a_lr = _sigmoid(la0_ref[d:d + 1, :] + _dot(xa_mid, la2_ref[d]))
        kd = k * (1.0 + (a_lr - 1.0) * ka)
        kd_out[d] = kd
        bb_out[d] = kk * a_lr
        kd_sum = kd_sum + kd
    bonus_out[...] = _dot(r * kd_sum * rk_ref[...], seg) * v

    pos = tile * tm + lax.broadcasted_iota(jnp.int32, (tm, 1), 0)
    trow = lax.broadcasted_iota(jnp.int32, (tm, tm + 2 * pad), 0)
    srow = lax.broadcasted_iota(jnp.int32, (tm, tm + 2 * pad), 1) - pad
    gp = wa // len(POOL_WINDOWS)
    ybs = []
    for gi, win in enumerate(POOL_WINDOWS):
        half = win // 2
        c0 = 3 * wa + gi * gp
        band = jnp.where((srow >= trow - half) & (srow < trow + half), 1.0, 0.0).astype(BF16)
        sums = _dot(band, pext_ref[:, c0:c0 + gp])
        cnt = (jnp.minimum(pos + half, seq_len) - jnp.maximum(pos - half, 0)).astype(F32)
        diff = sums / cnt - pext_ref[pad:pad + tm, c0:c0 + gp]
        ybs.append(_dot(diff, pw_ref[gi]))
    yb_out[...] = jnp.concatenate(ybs, axis=1) * ps_ref[...]


def _rwkv_prep(x, proj, g, mod, mod_base, mod_stride, wts, batch, t, grid_mode):
    n, d = x.shape
    wa = d // 2
    pcols = proj.shape[1]
    tm = 256
    pad = GRID_W if grid_mode else 8
    parts = GRID_SHIFT if grid_mode else SEQ_SHIFT
    tpb = t // tm
    hb = tm // pad
    nhb = n // pad
    main = lambda i: (i, 0)
    prev = lambda i: (jnp.maximum(i * hb - 1, 0), 0)
    nxt = lambda i: (jnp.minimum((i + 1) * hb, nhb - 1), 0)
    full2 = lambda i: (0, 0)
    full3 = lambda i: (0, 0, 0)
    in_specs = [pl.BlockSpec((pad, d), prev), pl.BlockSpec((tm, d), main), pl.BlockSpec((pad, d), nxt),
                pl.BlockSpec((pad, pcols), prev), pl.BlockSpec((tm, pcols), main), pl.BlockSpec((pad, pcols), nxt),
                pl.BlockSpec((1, d), full2),
                pl.BlockSpec((None, 6, d), lambda i: (mod_base + (i // tpb) * mod_stride, 0, 0))]
    for a in wts:
        in_specs.append(pl.BlockSpec(a.shape, full2 if a.ndim == 2 else full3))
    one = pl.BlockSpec((tm, wa), main)
    two = pl.BlockSpec((2, tm, wa), lambda i: (0, i, 0))
    sd1 = jax.ShapeDtypeStruct((n, wa), F32)
    sd2 = jax.ShapeDtypeStruct((2, n, wa), F32)
    return pl.pallas_call(
        functools.partial(_prep_kernel, parts=parts, tm=tm, pad=pad, tiles_per_batch=tpb, seq_len=t),
        grid=(n // tm,),
        in_specs=in_specs,
        out_specs=[one, one, one, two, two, two, one, one, one],
        out_shape=[sd1, sd1, sd1, sd2, sd2, sd2, sd1, sd1, sd1],
        scratch_shapes=[pltpu.VMEM((tm + 2 * pad, d), F32), pltpu.VMEM((tm + 2 * pad, pcols), F32)],
        compiler_params=_cparams("parallel"),
        name="rwkv_prep",
    )(x, x, x, proj, proj, proj, g.reshape(1, d), mod, *wts)


def _mix_out_kernel(y0_ref, y1_ref, bonus_ref, gate_ref, yb_ref, x_ref, m_ref, gnw_ref, gnb_ref,
                    seg_ref, wout_ref, o_ref):
    y = y0_ref[...] + y1_ref[...]
    seg = seg_ref[...]
    inv = 1.0 / HEAD_DIM
    yh = y.astype(BF16)
    mu = (jnp.dot(yh, seg, preferred_element_type=F32) + _dot(y - yh.astype(F32), seg)) * inv
    dlt = y - mu
    var = _dot(dlt * dlt, seg) * inv
    yn = dlt * lax.rsqrt(var + GN_EPS) * gnw_ref[...] + gnb_ref[...]
    ya = (yn + bonus_ref[...]) * gate_ref[...]
    cat = jnp.concatenate([ya, yb_ref[...]], axis=1)
    o_ref[...] = x_ref[...] + m_ref[2:3, :] * _dot(cat, wout_ref[...])


def _mix_out(y0, y1, bonus, gate, yb, x, mod, mod_base, mod_stride, gnw, gnb, seg, wout, t, tm):
    n, d = x.shape
    wa = d // 2
    tpb = t // tm
    half = pl.BlockSpec((tm, wa), lambda i: (i, 0))
    full = pl.BlockSpec((tm, d), lambda i: (i, 0))
    const = lambda a: pl.BlockSpec(a.shape, lambda i: (0, 0))
    return pl.pallas_call(
        _mix_out_kernel,
        grid=(n // tm,),
        in_specs=[half, half, half, half, half, full,
                  pl.BlockSpec((None, 6, d), lambda i: (mod_base + (i // tpb) * mod_stride, 0, 0)),
                  const(gnw), const(gnb), const(seg), const(wout)],
        out_specs=full,
        out_shape=jax.ShapeDtypeStruct((n, d), F32),
        input_output_aliases={5: 0},
        compiler_params=_cparams("parallel"),
        name="mix_out",
    )(y0, y1, bonus, gate, yb, x, mod, gnw, gnb, seg, wout)


def _chan_dft_kernel(x_ref, g_ref, m_ref, cs_ref, xc_ref, xs_ref, *, groups):
    h = _modulate(x_ref[...], g_ref[...], m_ref[0:1, :], m_ref[1:2, :])
    gc = h.shape[1] // groups
    cs = cs_ref[...]
    for gi in range(groups):
        res = _dot(h[:, gi * gc:(gi + 1) * gc], cs)
        xc_ref[:, gi * gc:(gi + 1) * gc] = res[:, :gc].astype(xc_ref.dtype)
        xs_ref[:, gi * gc:(gi + 1) * gc] = res[:, gc:].astype(xs_ref.dtype)


def _chan_dft(x, g, mod, mod_base, mod_stride, cs, t, tm, groups, out_dtype):
    n, d = x.shape
    tpb = t // tm
    full = pl.BlockSpec((tm, d), lambda i: (i, 0))
    return pl.pallas_call(
        functools.partial(_chan_dft_kernel, groups=groups),
        grid=(n // tm,),
        in_specs=[full, pl.BlockSpec((1, d), lambda i: (0, 0)),
                  pl.BlockSpec((None, 6, d), lambda i: (mod_base + (i // tpb) * mod_stride, 0, 0)),
                  pl.BlockSpec(cs.shape, lambda i: (0, 0))],
        out_specs=[full, full],
        out_shape=[jax.ShapeDtypeStruct((n, d), out_dtype)] * 2,
        compiler_params=_cparams("parallel"),
        name="chan_dft",
    )(x, g.reshape(1, d), mod, cs)


LANES = 128
SUBLANES = 8


def _dft_stage_a_kernel(m_ref, xc_ref, xs_ref, ar_ref, ai_ref):
    m = m_ref[...]
    n1 = ar_ref.shape[0]
    for v in range(SUBLANES):
        rhs = jnp.concatenate([xc_ref[:, v, :], xs_ref[:, v, :]], axis=0).astype(BF16)
        res = jnp.dot(m, rhs, preferred_element_type=F32)
        ar_ref[:, v, :] = res[:n1]
        ai_ref[:, v, :] = res[n1:]


def _dft_stage_a(m, xc, xs, batch, t):
    n, d = xc.shape
    n1 = t // LANES
    nu = LANES // SUBLANES
    view = (batch, n1, nu, SUBLANES, d)
    blk = pl.BlockSpec((None, n1, None, SUBLANES, d), lambda b, u: (b, 0, u, 0, 0))
    return pl.pallas_call(
        _dft_stage_a_kernel,
        grid=(batch, nu),
        in_specs=[pl.BlockSpec(m.shape, lambda b, u: (0, 0)), blk, blk],
        out_specs=[blk, blk],
        out_shape=[jax.ShapeDtypeStruct(view, F32)] * 2,
        compiler_params=_cparams("parallel", "parallel"),
        name="dft_stage_a",
    )(m, xc.reshape(view), xs.reshape(view))


def _dft_stage_c_kernel(cs_ref, twc_ref, tws_ref, ar_ref, ai_ref, y_ref):
    cs = cs_ref[...]
    for w in range(SUBLANES):
        ar, ai = ar_ref[w], ai_ref[w]
        cw, sw = twc_ref[w], tws_ref[w]
        b = jnp.concatenate([ar * cw + ai * sw, ai * cw - ar * sw], axis=0)
        y_ref[:, w, :] = _dot(cs, b)


def _dft_stage_c(cs, twc, tws, ar, ai, batch, t):
    d = ar.shape[-1]
    n1 = t // LANES
    ng = n1 // SUBLANES
    a_view = (batch, n1, LANES, d)
    a_blk = pl.BlockSpec((None, SUBLANES, LANES, d), lambda b, g: (b, g, 0, 0))
    tw_blk = pl.BlockSpec((SUBLANES, LANES, 1), lambda b, g: (g, 0, 0))
    return pl.pallas_call(
        _dft_stage_c_kernel,
        grid=(batch, ng),
        in_specs=[pl.BlockSpec(cs.shape, lambda b, g: (0, 0)), tw_blk, tw_blk, a_blk, a_blk],
        out_specs=pl.BlockSpec((None, LANES, None, SUBLANES, d), lambda b, g: (b, 0, g, 0, 0)),
        out_shape=jax.ShapeDtypeStruct((batch, LANES, ng, SUBLANES, d), F32),
        compiler_params=_cparams("parallel", "parallel"),
        name="dft_stage_c",
    )(cs, twc, tws, ar.reshape(a_view), ai.reshape(a_view)).reshape(batch * t, d)


def _resid_matmul_kernel(y_ref, x_ref, m_ref, w_ref, o_ref):
    o_ref[...] = x_ref[...] + m_ref[2:3, :] * _dot(y_ref[...], w_ref[...])


def _resid_matmul(y, x, mod, mod_base, mod_stride, w, t, tm):
    n, d = x.shape
    tpb = t // tm
    full = pl.BlockSpec((tm, d), lambda i: (i, 0))
    return pl.pallas_call(
        _resid_matmul_kernel,
        grid=(n // tm,),
        in_specs=[full, full,
                  pl.BlockSpec((None, 6, d), lambda i: (mod_base + (i // tpb) * mod_stride, 0, 0)),
                  pl.BlockSpec(w.shape, lambda i: (0, 0))],
        out_specs=full,
        out_shape=jax.ShapeDtypeStruct((n, d), F32),
        input_output_aliases={1: 0},
        compiler_params=_cparams("parallel"),
        name="resid_matmul",
    )(y, x, mod, w)


def _two_stage_tables(t):
    n1 = t // LANES
    c1, s1 = _trig(n1, n1, n1)
    m = jnp.concatenate([jnp.concatenate([c1, -s1], axis=1), jnp.concatenate([-s1, -c1], axis=1)], axis=0)
    c2, s2 = _trig(LANES, LANES, LANES)
    twc, tws = _trig(n1, LANES, t)
    return ((m * t ** -0.5).astype(BF16), jnp.concatenate([c2, s2], axis=1).astype(BF16),
            twc[:, :, None], tws[:, :, None])


def _time_dft_kernel(ct_ref, st_ref, xc_ref, xs_ref, x_ref, m_ref, wf_ref, o_ref, acc_ref):
    ki = pl.program_id(2)

    @pl.when(ki == 0)
    def _():
        acc_ref[...] = jnp.zeros_like(acc_ref)

    acc_ref[...] += (jnp.dot(ct_ref[...], xc_ref[...], preferred_element_type=F32)
                     - jnp.dot(st_ref[...], xs_ref[...], preferred_element_type=F32))

    @pl.when(ki == pl.num_programs(2) - 1)
    def _():
        o_ref[...] = x_ref[...] + m_ref[2:3, :] * _dot(acc_ref[...], wf_ref[...])


def _time_dft(tabs, xc, xs, x, mod, mod_base, mod_stride, wf, batch, t, tf, tk):
    ct, st = tabs
    n, d = x.shape
    nf, nk = t // tf, t // tk
    return pl.pallas_call(
        _time_dft_kernel,
        grid=(batch, nf, nk),
        in_specs=[pl.BlockSpec((tf, tk), lambda b, f, k: (f, k)),
                  pl.BlockSpec((tf, tk), lambda b, f, k: (f, k)),
                  pl.BlockSpec((tk, d), lambda b, f, k: (b * nk + k, 0)),
                  pl.BlockSpec((tk, d), lambda b, f, k: (b * nk + k, 0)),
                  pl.BlockSpec((tf, d), lambda b, f, k: (b * nf + f, 0)),
                  pl.BlockSpec((None, 6, d), lambda b, f, k: (mod_base + b * mod_stride, 0, 0)),
                  pl.BlockSpec((d, d), lambda b, f, k: (0, 0))],
        out_specs=pl.BlockSpec((tf, d), lambda b, f, k: (b * nf + f, 0)),
        out_shape=jax.ShapeDtypeStruct((n, d), F32),
        scratch_shapes=[pltpu.VMEM((tf, d), F32)],
        input_output_aliases={4: 0},
        compiler_params=_cparams("parallel", "parallel", "arbitrary"),
        name="time_dft",
    )(ct, st, xc, xs, x, mod, wf)


def _trig(rows, cols, period):
    prod = jnp.bitwise_and(jnp.arange(rows, dtype=jnp.int32)[:, None] * jnp.arange(cols, dtype=jnp.int32)[None, :],
                           period - 1)
    ang = prod.astype(F32) * (2.0 * math.pi / period)
    return jnp.cos(ang), jnp.sin(ang)


def _dft_table_kernel(ca_ref, sa_ref, cb_ref, sb_ref, ct_ref, st_ref):
    ca, sa = ca_ref[...], sa_ref[...]
    cb, sb = cb_ref[...], sb_ref[...]
    for j in range(ca.shape[1]):
        a, s = ca[:, j:j + 1], sa[:, j:j + 1]
        ct_ref[:, j * LANES:(j + 1) * LANES] = (a * cb - s * sb).astype(ct_ref.dtype)
        st_ref[:, j * LANES:(j + 1) * LANES] = (s * cb + a * sb).astype(st_ref.dtype)


def _time_tables(t, scale):
    hi_n = t // LANES
    ca, sa = _trig(t, hi_n, hi_n)
    cb, sb = _trig(t, LANES, t)
    tf = min(t, 256)
    small = pl.BlockSpec((tf, hi_n), lambda f: (f, 0))
    lanes = pl.BlockSpec((tf, LANES), lambda f: (f, 0))
    wide = pl.BlockSpec((tf, t), lambda f: (f, 0))
    return pl.pallas_call(
        _dft_table_kernel,
        grid=(t // tf,),
        in_specs=[small, small, lanes, lanes],
        out_specs=[wide, wide],
        out_shape=[jax.ShapeDtypeStruct((t, t), BF16)] * 2,
        compiler_params=_cparams("parallel"),
        name="dft_table",
    )(ca * scale, sa * scale, cb, sb)


def _router_kernel(x_ref, g_ref, m_ref, wr_ref, br_ref, tri_ref, ids_out, gates_out, cnt_out, tbase_out, trun_out,
                   base_ref):
    i = pl.program_id(0)

    @pl.when(i == 0)
    def _():
        base_ref[...] = jnp.zeros_like(base_ref)

    h2 = _modulate(x_ref[...], g_ref[...], m_ref[3:4, :], m_ref[4:5, :])
    h_hi = h2.astype(BF16)
    hs = (h_hi, (h2 - h_hi.astype(F32)).astype(BF16))
    lt = jnp.zeros((wr_ref.shape[1], h2.shape[0]), F32)
    for ia, ib in ((0, 0), (0, 1), (1, 0)):
        lt = lt + lax.dot_general(wr_ref[ib], hs[ia], (((1,), (1,)), ((), ())), preferred_element_type=F32)
    logits = lt[0:ROUTE_ROWS] + br_ref[0:ROUTE_ROWS, :]

    tm = logits.shape[1]
    row_i = lax.broadcasted_iota(jnp.int32, (ROUTE_ROWS, tm), 0)
    row = row_i.astype(F32)
    neg = -jnp.inf
    far = float(ROUTE_ROWS)
    gmask = row_i < MOE_GROUPS
    lc = jnp.where(gmask, logits, neg)
    mc = jnp.max(lc, axis=0, keepdims=True)
    sc = jnp.sum(jnp.where(gmask, jnp.exp(logits - mc), 0.0), axis=0, keepdims=True)
    g_val = 1.0 / sc
    g_idx = jnp.min(jnp.where(lc == mc, row, far), axis=0, keepdims=True)
    lgroup = jnp.where(row_i >= MOE_GROUPS, jnp.right_shift(row_i - MOE_GROUPS, 3), -1).astype(F32)
    sel = lgroup == g_idx
    lf = jnp.where(sel, logits, neg)
    m1 = jnp.max(lf, axis=0, keepdims=True)
    ef = jnp.where(sel, jnp.exp(logits - m1), 0.0)
    p = ef / jnp.sum(ef, axis=0, keepdims=True)
    p1 = jnp.where(sel, p, -1.0)
    v1 = jnp.max(p1, axis=0, keepdims=True)
    i1 = jnp.min(jnp.where(p1 == v1, row, far), axis=0, keepdims=True)
    p2 = jnp.where(row == i1, -1.0, p1)
    v2 = jnp.max(p2, axis=0, keepdims=True)
    i2 = jnp.min(jnp.where(p2 == v2, row, far), axis=0, keepdims=True)
    denom = v1 + v2
    gate1 = g_val * v1 / denom
    gate2 = g_val * v2 / denom

    tri = tri_ref[...]
    oh1 = jnp.where(row == i1, 1.0, 0.0)
    oh2 = jnp.where(row == i2, 1.0, 0.0)
    tot1 = jnp.sum(oh1, axis=1, keepdims=True)
    tot2 = jnp.sum(oh2, axis=1, keepdims=True)
    run = jnp.floor((tot1 + tot2 + (SUBLANES - 1)) * (1.0 / SUBLANES))
    er = lax.broadcasted_iota(jnp.int32, (ROUTE_ROWS, ROUTE_ROWS), 0)
    ec = lax.broadcasted_iota(jnp.int32, (ROUTE_ROWS, ROUTE_ROWS), 1)
    before = jnp.where(ec < er, 1.0, 0.0).astype(BF16)
    run_b = jnp.broadcast_to(run, (ROUTE_ROWS, LANES))
    off = jnp.dot(before, run_b.astype(BF16), preferred_element_type=F32)[:, 0:1] * SUBLANES
    c1 = jnp.dot(oh1.astype(BF16), tri, preferred_element_type=F32)
    c2 = jnp.dot(oh2.astype(BF16), tri, preferred_element_type=F32)
    pos1 = jnp.sum(oh1 * (off + c1), axis=0, keepdims=True)
    pos2 = jnp.sum(oh2 * (off + tot1 + c2), axis=0, keepdims=True)
    base = base_ref[...]
    tbase_out[...] = base
    trun_out[...] = run_b * SUBLANES
    base = base + run_b * SUBLANES
    base_ref[...] = base
    cnt_out[...] = base
    zrow = jnp.zeros((4, tm), F32)
    ids_out[...] = jnp.concatenate([i1 - MOE_GROUPS, i2 - MOE_GROUPS, pos1, pos2, zrow], axis=0).astype(jnp.int32)
    gates_out[...] = jnp.concatenate([gate1, gate2, zrow, zrow[0:2]], axis=0)


def _router(x, g, mod, mod_base, mod_stride, wr3, br, tri, t, tm):
    n, d = x.shape
    tpb = t // tm
    full = pl.BlockSpec((tm, d), lambda i: (i, 0))
    lanes = pl.BlockSpec((8, tm), lambda i: (0, i))
    cnt = pl.BlockSpec((ROUTE_ROWS, LANES), lambda i: (0, 0))
    per_tile = pl.BlockSpec((None, ROUTE_ROWS, LANES), lambda i: (i, 0, 0))
    tile_sd = jax.ShapeDtypeStruct((n // tm, ROUTE_ROWS, LANES), F32)
    return pl.pallas_call(
        _router_kernel,
        grid=(n // tm,),
        in_specs=[full, pl.BlockSpec((1, d), lambda i: (0, 0)),
                  pl.BlockSpec((None, 6, d), lambda i: (mod_base + (i // tpb) * mod_stride, 0, 0)),
                  pl.BlockSpec(wr3.shape, lambda i: (0, 0, 0)),
                  pl.BlockSpec(br.shape, lambda i: (0, 0)),
                  pl.BlockSpec((tm, tm), lambda i: (0, 0))],
        out_specs=[lanes, lanes, cnt, per_tile, per_tile],
        out_shape=[jax.ShapeDtypeStruct((8, n), jnp.int32),
                   jax.ShapeDtypeStruct((8, n), F32), jax.ShapeDtypeStruct((ROUTE_ROWS, LANES), F32),
                   tile_sd, tile_sd],
        scratch_shapes=[pltpu.VMEM((ROUTE_ROWS, LANES), F32)],
        compiler_params=_cparams("arbitrary"),
        name="moe_router",
    )(x, g.reshape(1, d), mod, wr3, br, tri)


RUN_SLACK = 256


def _run_copies(len_ref, pos_ref, dst_ref, tile, n_exp, max_len, make_copy, wait):
    sizes = []
    size = max_len
    while size >= SUBLANES:
        sizes.append(size)
        size //= 2

    def body(e, carry):
        idx = tile * n_exp + e
        n, pos, row = len_ref[idx], pos_ref[idx], dst_ref[idx]
        for size in sizes:
            done = jnp.bitwise_and(n, -2 * size)

            @pl.when(jnp.bitwise_and(n, size) != 0)
            def _():
                cp = make_copy(pl.multiple_of(pos + done, SUBLANES), pl.multiple_of(row + done, SUBLANES), size)
                if wait:
                    cp.wait()
                else:
                    cp.start()
        return carry

    lax.fori_loop(0, n_exp, body, 0)


def _dispatch_kernel(len_ref, pos_ref, dst_ref, zb_ref, x_ref, g_ref, m_ref, q_ref, xbuf_ref,
                     s_ref, zero_ref, sem, zsem, *, tm, n_exp):
    step = pl.program_id(0)

    def zero_copy(s):
        start = pl.multiple_of(zb_ref[s] * MOE_BLOCK, MOE_BLOCK)
        return pltpu.make_async_copy(zero_ref, xbuf_ref.at[pl.ds(start, MOE_BLOCK)], zsem)

    @pl.when(step == 0)
    def _():
        zero_ref[...] = jnp.zeros_like(zero_ref)
        for s in range(zb_ref.shape[0]):
            @pl.when(zb_ref[s] >= 0)
            def _():
                zero_copy(s).start()
        for s in range(zb_ref.shape[0]):
            @pl.when(zb_ref[s] >= 0)
            def _():
                zero_copy(s).wait()

    h2 = _modulate(x_ref[...], g_ref[...], m_ref[3:4, :], m_ref[4:5, :]).astype(BF16)
    srows = s_ref.shape[0]
    pos = lax.broadcasted_iota(jnp.int32, (srows, tm), 0)
    perm = jnp.where((pos == q_ref[2:3, :]) | (pos == q_ref[3:4, :]), 1.0, 0.0).astype(BF16)
    s_ref[...] = jnp.dot(perm, h2, preferred_element_type=F32)

    def run_copy(tile_pos, buf_row, size):
        return pltpu.make_async_copy(s_ref.at[pl.ds(tile_pos, size)], xbuf_ref.at[pl.ds(buf_row, size)], sem)

    _run_copies(len_ref, pos_ref, dst_ref, step, n_exp, 2 * tm, run_copy, False)
    _run_copies(len_ref, pos_ref, dst_ref, step, n_exp, 2 * tm, run_copy, True)


def _dispatch(runs, zblocks, x, g, mod, mod_base, mod_stride, ids, rows, n_exp, t, tm):
    n, d = x.shape
    tpb = t // tm
    cmap = lambda i, *_: (0, 0)
    return pl.pallas_call(
        functools.partial(_dispatch_kernel, tm=tm, n_exp=n_exp),
        grid_spec=pltpu.PrefetchScalarGridSpec(
            num_scalar_prefetch=4,
            grid=(n // tm,),
            in_specs=[pl.BlockSpec((tm, d), lambda i, *_: (i, 0)),
                      pl.BlockSpec((1, d), cmap),
                      pl.BlockSpec((None, 6, d), lambda i, *_: (mod_base + (i // tpb) * mod_stride, 0, 0)),
                      pl.BlockSpec((8, tm), lambda i, *_: (0, i))],
            out_specs=pl.BlockSpec(memory_space=pl.ANY),
            scratch_shapes=[pltpu.VMEM((2 * tm + RUN_SLACK, d), F32), pltpu.VMEM((MOE_BLOCK, d), F32),
                            pltpu.SemaphoreType.DMA, pltpu.SemaphoreType.DMA],
        ),
        out_shape=jax.ShapeDtypeStruct((rows, d), F32),
        compiler_params=_cparams("arbitrary"),
        name="moe_dispatch",
    )(*runs, zblocks, x, g.reshape(1, d), mod, ids)


def _expert_kernel(be_ref, na_ref, x_ref, w1_ref, w3_ref, w2_ref, o_ref):
    active = pl.program_id(0) < na_ref[0]

    @pl.when(active)
    def _():
        xb = x_ref[...].astype(BF16)
        h1 = jnp.dot(xb, w1_ref[...].astype(BF16), preferred_element_type=F32)
        h3 = jnp.dot(xb, w3_ref[...].astype(BF16), preferred_element_type=F32)
        act = h1 * _sigmoid(h1) * h3
        o_ref[...] = _dot(act, w2_ref[...])

    @pl.when(jnp.logical_not(active))
    def _():
        o_ref[...] = jnp.zeros_like(o_ref)


def _experts(block_e, nact, xbuf, w1, w3, w2, layer):
    rows, d = xbuf.shape
    nb = rows // MOE_BLOCK
    de = w1.shape[3]
    blk = lambda i, be, na: (jnp.minimum(i, na[0] - 1), 0)
    wmap = lambda i, be, na: (layer, be[jnp.minimum(i, na[0] - 1)], 0, 0)
    return pl.pallas_call(
        _expert_kernel,
        grid_spec=pltpu.PrefetchScalarGridSpec(
            num_scalar_prefetch=2,
            grid=(nb,),
            in_specs=[pl.BlockSpec((MOE_BLOCK, d), blk),
                      pl.BlockSpec((None, None, d, de), wmap),
                      pl.BlockSpec((None, None, d, de), wmap),
                      pl.BlockSpec((None, None, de, d), wmap)],
            out_specs=pl.BlockSpec((MOE_BLOCK, d), lambda i, be, na: (i, 0)),
        ),
        out_shape=jax.ShapeDtypeStruct((rows, d), F32),
        compiler_params=_cparams("arbitrary"),
        name="moe_experts",
    )(block_e, nact, xbuf, w1, w3, w2)


def _combine_kernel(len_ref, pos_ref, dst_ref, y_ref, q_ref, gates_ref, x_ref, m_ref, fn_ref, o_ref, w_ref, sem,
                    *, tm, n_exp, final):
    step = pl.program_id(0)

    @pl.when(step == 0)
    def _():
        w_ref[...] = jnp.zeros_like(w_ref)

    def run_copy(tile_pos, buf_row, size):
        return pltpu.make_async_copy(y_ref.at[pl.ds(buf_row, size)], w_ref.at[pl.ds(tile_pos, size)], sem)

    _run_copies(len_ref, pos_ref, dst_ref, step, n_exp, 2 * tm, run_copy, False)
    _run_copies(len_ref, pos_ref, dst_ref, step, n_exp, 2 * tm, run_copy, True)

    wrows = w_ref.shape[0]
    pos = lax.broadcasted_iota(jnp.int32, (tm, wrows), 1)
    q = q_ref[...]
    sel = jnp.concatenate([jnp.where(pos == q[:, 0:1], 1.0, 0.0), jnp.where(pos == q[:, 1:2], 1.0, 0.0)],
                          axis=0).astype(BF16)
    picked = jnp.dot(sel, w_ref[...].astype(BF16), preferred_element_type=F32)
    gates = gates_ref[...]
    y = gates[:, 0:1] * picked[:tm] + gates[:, 1:2] * picked[tm:]
    out = x_ref[...] + m_ref[5:6, :] * y
    if final:
        ms = jnp.mean(out * out, axis=-1, keepdims=True)
        out = out * lax.rsqrt(ms + NORM_EPS) * fn_ref[...]
    o_ref[...] = out


def _combine(runs, ybuf, qcols, gates, x, mod, mod_base, mod_stride, fnorm, n_exp, t, tm, final):
    n, d = x.shape
    tpb = t // tm
    full = pl.BlockSpec((tm, d), lambda i, *_: (i, 0))
    pair = pl.BlockSpec((tm, 2), lambda i, *_: (i, 0))
    return pl.pallas_call(
        functools.partial(_combine_kernel, tm=tm, n_exp=n_exp, final=final),
        grid_spec=pltpu.PrefetchScalarGridSpec(
            num_scalar_prefetch=3,
            grid=(n // tm,),
            in_specs=[pl.BlockSpec(memory_space=pl.ANY), pair, pair, full,
                      pl.BlockSpec((None, 6, d), lambda i, *_: (mod_base + (i // tpb) * mod_stride, 0, 0)),
                      pl.BlockSpec((1, d), lambda i, *_: (0, 0))],
            out_specs=full,
            scratch_shapes=[pltpu.VMEM((2 * tm + RUN_SLACK, d), F32), pltpu.SemaphoreType.DMA],
        ),
        out_shape=jax.ShapeDtypeStruct((n, d), F32),
        input_output_aliases={6: 0},
        compiler_params=_cparams("arbitrary"),
        name="moe_combine",
    )(*runs, ybuf, qcols, gates, x, mod, fnorm)


def _moe(x, g, mod, mod_base, mod_stride, rt, w1, w3, w2, layer, fnorm, t, tm, final):
    n, d = x.shape
    n_exp = w1.shape[1]
    wr3, br, tri = rt
    assert n_exp * (SUBLANES - 1) <= RUN_SLACK
    ids, gates, counts, tbase, trun = _router(x, g, mod, mod_base, mod_stride, wr3, br, tri, t, tm)
    experts = slice(MOE_GROUPS, MOE_GROUPS + n_exp)
    counts = counts[experts, 0].astype(jnp.int32)
    padded = (counts + MOE_BLOCK - 1) // MOE_BLOCK * MOE_BLOCK
    pend = jnp.cumsum(padded)
    pstart = pend - padded
    run_len = trun[:, experts, 0].astype(jnp.int32)
    run_pos = jnp.cumsum(run_len, axis=1) - run_len
    run_dst = pstart[None, :] + tbase[:, experts, 0].astype(jnp.int32)
    runs = (run_len.reshape(-1), run_pos.reshape(-1), run_dst.reshape(-1))
    ntiles = n // tm
    nb = -(-(2 * n + ntiles * n_exp * (SUBLANES - 1)) // MOE_BLOCK) + n_exp
    blk_start = jnp.arange(nb, dtype=jnp.int32) * MOE_BLOCK
    block_e = jnp.minimum(jnp.sum((pend[None, :] <= blk_start[:, None]).astype(jnp.int32), axis=1), n_exp - 1)
    nact = pend[-1:] // MOE_BLOCK
    partial = jnp.where(padded > counts, pend // MOE_BLOCK - 1, -1)
    tail = nact + jnp.arange(nb - (2 * n) // MOE_BLOCK, dtype=jnp.int32)
    zblocks = jnp.concatenate([partial, jnp.where(tail < nb, tail, -1)]).astype(jnp.int32)
    xbuf = _dispatch(runs, zblocks, x, g, mod, mod_base, mod_stride, ids, nb * MOE_BLOCK, n_exp, t, tm)
    ybuf = _experts(block_e, nact, xbuf, w1, w3, w2, layer)
    return _combine(runs, ybuf, ids[2:4].T, gates[0:2].T, x, mod, mod_base, mod_stride, fnorm, n_exp, t, tm, final)


def _seg_ones(width):
    idx = np.arange(width) // HEAD_DIM
    return jnp.asarray((idx[:, None] == idx[None, :]).astype(np.float32), dtype=BF16)


def _even_weights(j, mu_x, mu_p, decay_w0, decay_w1, decay_w2, lr_a0, lr_a1, lr_a2, gate_g1, gate_g2,
                  k_k, k_a, r_k, pool_w, pool_scale, seg):
    wa = mu_p.shape[-1]

    def cat1(w):
        return jnp.concatenate([w[0], w[1]], axis=1).astype(BF16)

    def pad2(w):
        z = jnp.zeros_like(w[0])
        return jnp.stack([jnp.concatenate([w[0], z], axis=0), jnp.concatenate([z, w[1]], axis=0)]).astype(BF16)

    return (mu_x[j], mu_p[j], decay_w0[j], cat1(decay_w1[j]), pad2(decay_w2[j]),
            lr_a0[j], cat1(lr_a1[j]), pad2(lr_a2[j]),
            gate_g1[j].astype(BF16), gate_g2[j].astype(BF16),
            k_k[j].reshape(1, wa), k_a[j].reshape(1, wa), r_k[j].reshape(1, wa),
            pool_w[j].astype(BF16), pool_scale[j].reshape(1, wa), seg)


def kernel(x, c, ctx, c_ctx, ada_w, ada_b, norm_mix, norm_ffn, w_in, mu_x, mu_p, decay_w0, decay_w1, decay_w2, lr_a0, lr_a1, lr_a2, gate_g1, gate_g2, k_k, k_a, r_k, gn_w, gn_b, pool_w, pool_scale, w_out, w_fourier, router_c, router_c_b, router_f, router_f_b, moe_w1, moe_w3, moe_w2, final_norm):
    batch, t, d = x.shape
    tc = ctx.shape[1]
    depth = ada_w.shape[0]
    wa = d // 2
    n, ncx = batch * t, batch * tc
    assert batch <= 4 and t % 512 == 0 and tc % 256 == 0 and d % 512 == 0

    cond8 = jnp.zeros((8, d), F32).at[:batch].set(c).at[4].set(c_ctx)
    mod_all = _ada(cond8, ada_w, ada_b).reshape(depth, 8, 6, d)
    seg = _seg_ones(wa)
    fnorm = final_norm.reshape(1, d)

    fgroups = 4
    gc = d // fgroups
    cc, sc = _trig(gc, gc, gc)
    cs = (jnp.concatenate([cc, sc], axis=1) * gc ** -0.5).astype(BF16)
    stage_m, stage_cs, stage_twc, stage_tws = _two_stage_tables(t)
    tab_ctx = _time_tables(tc, tc ** -0.5)

    n_exp = moe_w1.shape[1]
    tri = {tm: jnp.asarray(np.triu(np.ones((tm, tm), np.float32), 1), dtype=BF16) for tm in (512, 256)}

    lat = x.reshape(n, d)
    cx = ctx.reshape(ncx, d)
    last_read = 2 * ((depth - 1) // 2)
    npair = wa // PAIR

    for i in range(depth):
        ctx_in = i <= last_read
        ctx_out = i < last_read
        mod = mod_all[i]
        j = i // 2
        if i % 2 == 0:
            wts = _even_weights(j, mu_x, mu_p, decay_w0, decay_w1, decay_w2, lr_a0, lr_a1, lr_a2,
                                gate_g1, gate_g2, k_k, k_a, r_k, pool_w, pool_scale, seg)
            w_in_b = w_in[j].astype(BF16)
            w_out_b = w_out[j].astype(BF16)
            gnw, gnb = gn_w[j].reshape(1, wa), gn_b[j].reshape(1, wa)
            pc = _proj(cx, norm_mix[i], mod, w_in_b, tc, 4, 0, 0, 256)
            fc = _rwkv_prep(cx, pc, norm_mix[i], mod, 4, 0, wts, batch, tc, False)
            ryc, pqc = _wkv_chunks(*fc[:6], batch, tc)
            g0 = jnp.zeros((2, batch, npair, PAIR, PAIR), F32)
            yc0, yc1, gctx = _wkv_scan(ryc, pqc, g0)
            pl_ = _proj(lat, norm_mix[i], mod, w_in_b, t, 0, 1, 0, 512)
            fl = _rwkv_prep(lat, pl_, norm_mix[i], mod, 0, 1, wts, batch, t, True)
            ryl, pql = _wkv_chunks(*fl[:6], batch, t)
            yl0, yl1, _ = _wkv_scan(ryl, pql, gctx)
            lat = _mix_out(yl0.reshape(n, wa), yl1.reshape(n, wa), fl[6], fl[7], fl[8], lat, mod, 0, 1,
                           gnw, gnb, seg, w_out_b, t, 512)
            if ctx_out:
                cx = _mix_out(yc0.reshape(ncx, wa), yc1.reshape(ncx, wa), fc[6], fc[7], fc[8], cx, mod, 4, 0,
                              gnw, gnb, seg, w_out_b, tc, 256)
        else:
            wf_b = w_fourier[j].astype(BF16)
            xc, xs = _chan_dft(lat, norm_mix[i], mod, 0, 1, cs, t, 512, fgroups, F32)
            ar, ai = _dft_stage_a(stage_m, xc, xs, batch, t)
            yf = _dft_stage_c(stage_cs, stage_twc, stage_tws, ar, ai, batch, t)
            lat = _resid_matmul(yf, lat, mod, 0, 1, wf_b, t, 512)
            if ctx_out:
                xcc, xsc = _chan_dft(cx, norm_mix[i], mod, 4, 0, cs, tc, 256, fgroups, BF16)
                cx = _time_dft(tab_ctx, xcc, xsc, cx, mod, 4, 0, wf_b, batch, tc, tc, tc)
        wr = jnp.zeros((LANES, d), F32).at[:MOE_GROUPS].set(router_c[i].T).at[MOE_GROUPS:MOE_GROUPS + n_exp].set(router_f[i].T)
        br = jnp.zeros((LANES, 1), F32).at[:MOE_GROUPS, 0].set(router_c_b[i]).at[MOE_GROUPS:MOE_GROUPS + n_exp, 0].set(router_f_b[i])
        wr3 = jnp.stack(_split3(wr))
        final = i == depth - 1
        lat = _moe(lat, norm_ffn[i], mod, 0, 1, (wr3, br, tri[512]), moe_w1, moe_w3, moe_w2, i, fnorm,
                   t, 512, final)
        if ctx_out:
            cx = _moe(cx, norm_ffn[i], mod, 4, 0, (wr3, br, tri[256]), moe_w1, moe_w3, moe_w2, i, fnorm,
                      tc, 256, False)
    return lat.reshape(batch, t, d)
```

```python
import functools
import math

import jax
import jax.numpy as jnp
import numpy as np
from jax import lax
from jax.experimental import pallas as pl
from jax.experimental.pallas import tpu as pltpu

F32 = jnp.float32
BF16 = jnp.bfloat16

GRID_W = 64
HEAD_DIM = 64
CHUNK = 64
PAIR = 2 * HEAD_DIM
NORM_EPS = 1e-6
GN_EPS = 64e-5
POOL_WINDOWS = (2, 4, 8, 16)
MOE_GROUPS = 4
ROUTE_ROWS = 40
EXPERTS_PER_GROUP = 8
MOE_BLOCK = 512
VMEM_LIMIT = 56 * 1024 * 1024


def _cparams(*sem):
    return pltpu.CompilerParams(dimension_semantics=tuple(sem), vmem_limit_bytes=VMEM_LIMIT)


def _dot(a, b):
    return jnp.dot(a.astype(BF16), b.astype(BF16), preferred_element_type=F32)


def _dot_nt(a, b):
    return lax.dot_general(a.astype(BF16), b.astype(BF16), (((1,), (1,)), ((), ())),
                           preferred_element_type=F32)


def _dot_tn(a, b):
    return lax.dot_general(a.astype(BF16), b.astype(BF16), (((0,), (0,)), ((), ())),
                           preferred_element_type=F32)


def _split3(x):
    hi = x.astype(BF16)
    r1 = x - hi.astype(F32)
    mid = r1.astype(BF16)
    lo = (r1 - mid.astype(F32)).astype(BF16)
    return hi, mid, lo


def _sigmoid(x):
    return 1.0 / (1.0 + jnp.exp(-x))


def _modulate(x, g, shift, scale):
    ms = jnp.mean(x * x, axis=-1, keepdims=True)
    return x * lax.rsqrt(ms + NORM_EPS) * g * (1.0 + scale) + shift


def _ada_kernel(c_ref, w_ref, b_ref, o_ref):
    c = c_ref[...]
    s = c * _sigmoid(c)
    o_ref[...] = _dot(s, w_ref[...]) + b_ref[...]


def _ada(cond8, ada_w, ada_b):
    depth, d, n6 = ada_w.shape
    tn = 1536
    return pl.pallas_call(
        _ada_kernel,
        grid=(depth, n6 // tn),
        in_specs=[pl.BlockSpec((8, d), lambda l, j: (0, 0)),
                  pl.BlockSpec((None, d, tn), lambda l, j: (l, 0, j)),
                  pl.BlockSpec((None, 1, tn), lambda l, j: (l, 0, j))],
        out_specs=pl.BlockSpec((None, 8, tn), lambda l, j: (l, 0, j)),
        out_shape=jax.ShapeDtypeStruct((depth, 8, n6), F32),
        compiler_params=_cparams("parallel", "parallel"),
        name="ada_mod",
    )(cond8, ada_w, ada_b.reshape(depth, 1, n6))


def _proj_kernel(x_ref, g_ref, m_ref, w_ref, o_ref, *, shift_row, scale_row):
    h = _modulate(x_ref[...], g_ref[...], m_ref[shift_row:shift_row + 1, :], m_ref[scale_row:scale_row + 1, :])
    o_ref[...] = _dot(h, w_ref[...]).astype(o_ref.dtype)


def _proj(x, g, mod, w, rows_per_batch, mod_base, mod_stride, shift_row, tm, out_dtype=F32):
    n, d = x.shape
    nout = w.shape[1]
    tpb = rows_per_batch // tm
    return pl.pallas_call(
        functools.partial(_proj_kernel, shift_row=shift_row, scale_row=shift_row + 1),
        grid=(n // tm,),
        in_specs=[pl.BlockSpec((tm, d), lambda i: (i, 0)),
                  pl.BlockSpec((1, d), lambda i: (0, 0)),
                  pl.BlockSpec((None, 6, d), lambda i: (mod_base + (i // tpb) * mod_stride, 0, 0)),
                  pl.BlockSpec((d, nout), lambda i: (0, 0))],
        out_specs=pl.BlockSpec((tm, nout), lambda i: (i, 0)),
        out_shape=jax.ShapeDtypeStruct((n, nout), out_dtype),
        compiler_params=_cparams("parallel"),
        name="mod_proj",
    )(x, g.reshape(1, d), mod, w)


def _wkv_chunk_kernel(r_ref, v_ref, an_ref, lw_ref, kd_ref, bb_ref, ry_ref, pq_ref):
    d = pl.program_id(0)
    width = lw_ref.shape[1]
    nsub = lw_ref.shape[0] // CHUNK
    row = lax.broadcasted_iota(jnp.int32, (CHUNK, CHUNK), 0)
    col = lax.broadcasted_iota(jnp.int32, (CHUNK, CHUNK), 1)
    sgn = 1 - 2 * d
    tri = jnp.where((row - col) * sgn >= 0, 1.0, 0.0).astype(BF16)

    def chunk_feats(sc):
        rows = slice(sc * CHUNK, (sc + 1) * CHUNK)
        lw = lw_ref[rows, :]
        hi, mid, lo = _split3(lw)
        csum = (jnp.dot(tri, hi, preferred_element_type=F32)
                + jnp.dot(tri, mid, preferred_element_type=F32)
                + jnp.dot(tri, lo, preferred_element_type=F32))
        e_neg = jnp.exp(-csum)
        return (an_ref[rows, :] * jnp.exp(csum - lw), bb_ref[rows, :] * e_neg, kd_ref[rows, :] * e_neg,
                r_ref[rows, :] * jnp.exp(csum), v_ref[rows, :], jnp.exp(jnp.sum(lw, axis=0, keepdims=True)))

    feats = [chunk_feats(sc) for sc in range(nsub)]

    lane = lax.broadcasted_iota(jnp.int32, (1, PAIR), 1)
    head0 = lane < HEAD_DIM
    i2 = lax.broadcasted_iota(jnp.int32, (PAIR, PAIR), 0)
    j2 = lax.broadcasted_iota(jnp.int32, (PAIR, PAIR), 1)
    ti = jnp.bitwise_and(i2, CHUNK - 1)
    tj = jnp.bitwise_and(j2, CHUNK - 1)
    strict = (ti - tj) * sgn > 0
    incl = (ti - tj) * sgn >= 0
    eye = (i2 == j2).astype(F32)

    def stack(x):
        return jnp.concatenate([jnp.where(head0, x, 0.0), jnp.where(head0, 0.0, x)], axis=0)

    units = [(sc, p) for sc in range(nsub) for p in range(width // PAIR)]
    us = range(len(units))
    sls = [slice(p * PAIR, (p + 1) * PAIR) for _, p in units]
    a_s, b_s, k_s, r_s, v_s = ([stack(feats[sc][f][:, sls[u]]) for u, (sc, _) in enumerate(units)]
                               for f in range(5))
    m1 = [_dot_nt(jnp.concatenate([a_s[u], r_s[u]], axis=0), jnp.concatenate([b_s[u], k_s[u]], axis=0))
          for u in us]
    npow = [jnp.where(strict, m[:PAIR, :PAIR], 0.0) for m in m1]
    a_ak = [jnp.where(strict, m[:PAIR, PAIR:], 0.0) for m in m1]
    a_rb = [jnp.where(incl, m[PAIR:, :PAIR], 0.0) for m in m1]
    a_rk = [jnp.where(incl, m[PAIR:, PAIR:], 0.0) for m in m1]
    x = [jnp.concatenate([a_s[u], _dot(a_ak[u], v_s[u])], axis=1) for u in us]
    steps = int(math.log2(CHUNK))
    for it in range(steps):
        x = [x[u] + _dot(npow[u], x[u]) for u in us]
        if it + 1 < steps:
            npow = [_dot(npow[u], npow[u]) for u in us]
    for u, (sc, p) in enumerate(units):
        ry = jnp.concatenate([r_s[u], _dot(a_rk[u], v_s[u])], axis=1) + _dot(a_rb[u], x[u])
        ry_ref[sc, p] = ry.astype(ry_ref.dtype)
    for u, (sc, p) in enumerate(units):
        ge = feats[sc][5][:, sls[u]]
        pt = (eye + _dot_tn(x[u][:, :PAIR], b_s[u])) * ge
        qt = (_dot_tn(x[u][:, PAIR:], b_s[u]) + _dot_tn(v_s[u], k_s[u])) * ge
        pq_ref[sc, p] = jnp.concatenate([pt, qt], axis=0).astype(pq_ref.dtype)


def _wkv_chunks(r, v, an, lw, kd, bb, batch, t):
    n, w = r.shape
    nc = t // CHUNK
    npair = w // PAIR
    nsub = min(4, nc)
    ns = nc // nsub
    shared = pl.BlockSpec((nsub * CHUNK, w), lambda d, b, c: (b * ns + c, 0))
    perdir = pl.BlockSpec((None, nsub * CHUNK, w), lambda d, b, c: (d, b * ns + c, 0))
    return pl.pallas_call(
        _wkv_chunk_kernel,
        grid=(2, batch, ns),
        in_specs=[shared, shared, shared, perdir, perdir, perdir],
        out_specs=[pl.BlockSpec((None, None, nsub, npair, PAIR, 2 * PAIR), lambda d, b, c: (d, b, c, 0, 0, 0)),
                   pl.BlockSpec((None, None, nsub, npair, 2 * PAIR, PAIR), lambda d, b, c: (d, b, c, 0, 0, 0))],
        out_shape=[jax.ShapeDtypeStruct((2, batch, nc, npair, PAIR, 2 * PAIR), BF16),
                   jax.ShapeDtypeStruct((2, batch, nc, npair, 2 * PAIR, PAIR), BF16)],
        compiler_params=_cparams("parallel", "parallel", "parallel"),
        name="wkv_chunks",
    )(r, v, an, lw, kd, bb)


def _wkv_scan_kernel(ryf_ref, ryb_ref, pqf_ref, pqb_ref, g0_ref, yf_ref, yb_ref, gout_ref, g_ref):
    pos = pl.program_id(0)

    @pl.when(pos == 0)
    def _():
        g_ref[...] = g0_ref[...]

    batch, npair = g_ref.shape[1], g_ref.shape[2]
    for d, (ry_ref, pq_ref, y_ref) in enumerate(((ryf_ref, pqf_ref, yf_ref), (ryb_ref, pqb_ref, yb_ref))):
        for b in range(batch):
            for p in range(npair):
                g = g_ref[d, b, p]
                ry = ry_ref[b, p]
                pq = pq_ref[b, p]
                ys = _dot_nt(ry[:, :PAIR], g) + ry[:, PAIR:].astype(F32)
                y_ref[b, :, p * PAIR:(p + 1) * PAIR] = ys[:CHUNK] + ys[CHUNK:]
                g_ref[d, b, p] = _dot(g, pq[:PAIR]) + pq[PAIR:].astype(F32)

    @pl.when(pos == pl.num_programs(0) - 1)
    def _():
        gout_ref[...] = g_ref[...]


def _wkv_scan(ry, pq, g0):
    _, batch, nc, npair, _, _ = ry.shape
    w = npair * PAIR
    t = nc * CHUNK
    ry_f = pl.BlockSpec((None, batch, None, npair, PAIR, 2 * PAIR), lambda s: (0, 0, s, 0, 0, 0))
    ry_b = pl.BlockSpec((None, batch, None, npair, PAIR, 2 * PAIR), lambda s: (1, 0, nc - 1 - s, 0, 0, 0))
    pq_f = pl.BlockSpec((None, batch, None, npair, 2 * PAIR, PAIR), lambda s: (0, 0, s, 0, 0, 0))
    pq_b = pl.BlockSpec((None, batch, None, npair, 2 * PAIR, PAIR), lambda s: (1, 0, nc - 1 - s, 0, 0, 0))
    gspec = pl.BlockSpec((2, batch, npair, PAIR, PAIR), lambda s: (0, 0, 0, 0, 0))
    return pl.pallas_call(
        _wkv_scan_kernel,
        grid=(nc,),
        in_specs=[ry_f, ry_b, pq_f, pq_b, gspec],
        out_specs=[pl.BlockSpec((batch, CHUNK, w), lambda s: (0, s, 0)),
                   pl.BlockSpec((batch, CHUNK, w), lambda s: (0, nc - 1 - s, 0)),
                   gspec],
        out_shape=[jax.ShapeDtypeStruct((batch, t, w), F32),
                   jax.ShapeDtypeStruct((batch, t, w), F32),
                   jax.ShapeDtypeStruct((2, batch, npair, PAIR, PAIR), F32)],
        scratch_shapes=[pltpu.VMEM((2, batch, npair, PAIR, PAIR), F32)],
        compiler_params=_cparams("arbitrary"),
        name="wkv_scan",
    )(ry, ry, pq, pq, g0)


GRID_SHIFT = ((-1, "first_col"), (1, "last_col"), (-GRID_W, None), (GRID_W, None))
SEQ_SHIFT = ((-1, None), (1, None))


def _prep_kernel(xp_ref, xm_ref, xn_ref, pp_ref, pm_ref, pn_ref, g_ref, m_ref,
                 mux_ref, mup_ref, dw0_ref, dw1_ref, dw2_ref, la0_ref, la1_ref, la2_ref,
                 gg1_ref, gg2_ref, kk_ref, ka_ref, rk_ref, pw_ref, ps_ref, seg_ref,
                 r_out, v_out, an_out, lw_out, kd_out, bb_out, bonus_out, gate_out, yb_out,
                 hext_ref, pext_ref, *, parts, tm, pad, tiles_per_batch, seq_len):
    i = pl.program_id(0)
    tile = lax.rem(i, tiles_per_batch)
    keep_prev = jnp.where(tile == 0, 0.0, 1.0)
    keep_next = jnp.where(tile == tiles_per_batch - 1, 0.0, 1.0)
    g = g_ref[...]
    shift, scale = m_ref[0:1, :], m_ref[1:2, :]
    d_model = xm_ref.shape[1]
    wa = r_out.shape[1]

    hext_ref[0:pad, :] = _modulate(xp_ref[...], g, shift, scale) * keep_prev
    hext_ref[pad:pad + tm, :] = _modulate(xm_ref[...], g, shift, scale)
    hext_ref[pad + tm:, :] = _modulate(xn_ref[...], g, shift, scale) * keep_next
    pext_ref[0:pad, :] = pp_ref[...] * keep_prev
    pext_ref[pad:pad + tm, :] = pm_ref[...]
    pext_ref[pad + tm:, :] = pn_ref[...] * keep_next

    colidx = jnp.bitwise_and(lax.broadcasted_iota(jnp.int32, (tm, 1), 0), GRID_W - 1)

    def shifted(ext_ref, col0, width):
        pw = width // len(parts)
        outs = []
        for q, (off, mask) in enumerate(parts):
            blk = ext_ref[pad + off:pad + off + tm, col0 + q * pw:col0 + (q + 1) * pw]
            if mask == "first_col":
                blk = jnp.where(colidx == 0, 0.0, blk)
            elif mask == "last_col":
                blk = jnp.where(colidx == GRID_W - 1, 0.0, blk)
            outs.append(blk)
        return jnp.concatenate(outs, axis=1)

    h = hext_ref[pad:pad + tm, :]
    hx = shifted(hext_ref, 0, d_model) - h
    x_w = h + hx * mux_ref[0:1, :]
    x_a = h + hx * mux_ref[1:2, :]
    x_g = h + hx * mux_ref[2:3, :]
    zw_mid = jnp.tanh(_dot(x_w, dw1_ref[...]))
    xa_mid = _dot(x_a, la1_ref[...])
    gate_out[...] = _dot(_sigmoid(_dot(x_g, gg1_ref[...])), gg2_ref[...])

    def mixed(n):
        p_n = pext_ref[pad:pad + tm, n * wa:(n + 1) * wa]
        return p_n + (shifted(pext_ref, n * wa, wa) - p_n) * mup_ref[n:n + 1, :]

    r, k, v = mixed(0), mixed(1), mixed(2)
    seg = seg_ref[...]
    kk = k * kk_ref[...]
    kk = kk * lax.rsqrt(jnp.maximum(_dot(kk * kk, seg), 1e-12))
    r_out[...] = r
    v_out[...] = v
    an_out[...] = -kk
    ka = ka_ref[...]
    kd_sum = jnp.zeros_like(k)
    for d in range(2):
        zw = dw0_ref[d:d + 1, :] + _dot(zw_mid, dw2_ref[d])
        lw_out[d] = -math.exp(-0.5) * _sigmoid(zw)
        a_lr = _sigmoid(la0_ref[d:d + 1, :] + _dot(xa_mid, la2_ref[d]))
        kd = k * (1.0 + (a_lr - 1.0) * ka)
        kd_out[d] = kd
        bb_out[d] = kk * a_lr
        kd_sum = kd_sum + kd
    bonus_out[...] = _dot(r * kd_sum * rk_ref[...], seg) * v

    pos = tile * tm + lax.broadcasted_iota(jnp.int32, (tm, 1), 0)
    trow = lax.broadcasted_iota(jnp.int32, (tm, tm + 2 * pad), 0)
    srow = lax.broadcasted_iota(jnp.int32, (tm, tm + 2 * pad), 1) - pad
    gp = wa // len(POOL_WINDOWS)
    ybs = []
    for gi, win in enumerate(POOL_WINDOWS):
        half = win // 2
        c0 = 3 * wa + gi * gp
        band = jnp.where((srow >= trow - half) & (srow < trow + half), 1.0, 0.0).astype(BF16)
        sums = _dot(band, pext_ref[:, c0:c0 + gp])
        cnt = (jnp.minimum(pos + half, seq_len) - jnp.maximum(pos - half, 0)).astype(F32)
        diff = sums / cnt - pext_ref[pad:pad + tm, c0:c0 + gp]
        ybs.append(_dot(diff, pw_ref[gi]))
    yb_out[...] = jnp.concatenate(ybs, axis=1) * ps_ref[...]


def _rwkv_prep(x, proj, g, mod, mod_base, mod_stride, wts, batch, t, grid_mode):
    n, d = x.shape
    wa = d // 2
    pcols = proj.shape[1]
    tm = 256
    pad = GRID_W if grid_mode else 8
    parts = GRID_SHIFT if grid_mode else SEQ_SHIFT
    tpb = t // tm
    hb = tm // pad
    nhb = n // pad
    main = lambda i: (i, 0)
    prev = lambda i: (jnp.maximum(i * hb - 1, 0), 0)
    nxt = lambda i: (jnp.minimum((i + 1) * hb, nhb - 1), 0)
    full2 = lambda i: (0, 0)
    full3 = lambda i: (0, 0, 0)
    in_specs = [pl.BlockSpec((pad, d), prev), pl.BlockSpec((tm, d), main), pl.BlockSpec((pad, d), nxt),
                pl.BlockSpec((pad, pcols), prev), pl.BlockSpec((tm, pcols), main), pl.BlockSpec((pad, pcols), nxt),
                pl.BlockSpec((1, d), full2),
                pl.BlockSpec((None, 6, d), lambda i: (mod_base + (i // tpb) * mod_stride, 0, 0))]
    for a in wts:
        in_specs.append(pl.BlockSpec(a.shape, full2 if a.ndim == 2 else full3))
    one = pl.BlockSpec((tm, wa), main)
    two = pl.BlockSpec((2, tm, wa), lambda i: (0, i, 0))
    sd1 = jax.ShapeDtypeStruct((n, wa), F32)
    sd2 = jax.ShapeDtypeStruct((2, n, wa), F32)
    return pl.pallas_call(
        functools.partial(_prep_kernel, parts=parts, tm=tm, pad=pad, tiles_per_batch=tpb, seq_len=t),
        grid=(n // tm,),
        in_specs=in_specs,
        out_specs=[one, one, one, two, two, two, one, one, one],
        out_shape=[sd1, sd1, sd1, sd2, sd2, sd2, sd1, sd1, sd1],
        scratch_shapes=[pltpu.VMEM((tm + 2 * pad, d), F32), pltpu.VMEM((tm + 2 * pad, pcols), F32)],
        compiler_params=_cparams("parallel"),
        name="rwkv_prep",
    )(x, x, x, proj, proj, proj, g.reshape(1, d), mod, *wts)


def _mix_out_kernel(y0_ref, y1_ref, bonus_ref, gate_ref, yb_ref, x_ref, m_ref, gnw_ref, gnb_ref,
                    seg_ref, wout_ref, o_ref):
    y = y0_ref[...] + y1_ref[...]
    seg = seg_ref[...]
    inv = 1.0 / HEAD_DIM
    yh = y.astype(BF16)
    mu = (jnp.dot(yh, seg, preferred_element_type=F32) + _dot(y - yh.astype(F32), seg)) * inv
    dlt = y - mu
    var = _dot(dlt * dlt, seg) * inv
    yn = dlt * lax.rsqrt(var + GN_EPS) * gnw_ref[...] + gnb_ref[...]
    ya = (yn + bonus_ref[...]) * gate_ref[...]
    cat = jnp.concatenate([ya, yb_ref[...]], axis=1)
    o_ref[...] = x_ref[...] + m_ref[2:3, :] * _dot(cat, wout_ref[...])


def _mix_out(y0, y1, bonus, gate, yb, x, mod, mod_base, mod_stride, gnw, gnb, seg, wout, t, tm):
    n, d = x.shape
    wa = d // 2
    tpb = t // tm
    half = pl.BlockSpec((tm, wa), lambda i: (i, 0))
    full = pl.BlockSpec((tm, d), lambda i: (i, 0))
    const = lambda a: pl.BlockSpec(a.shape, lambda i: (0, 0))
    return pl.pallas_call(
        _mix_out_kernel,
        grid=(n // tm,),
        in_specs=[half, half, half, half, half, full,
                  pl.BlockSpec((None, 6, d), lambda i: (mod_base + (i // tpb) * mod_stride, 0, 0)),
                  const(gnw), const(gnb), const(seg), const(wout)],
        out_specs=full,
        out_shape=jax.ShapeDtypeStruct((n, d), F32),
        input_output_aliases={5: 0},
        compiler_params=_cparams("parallel"),
        name="mix_out",
    )(y0, y1, bonus, gate, yb, x, mod, gnw, gnb, seg, wout)


def _chan_dft_kernel(x_ref, g_ref, m_ref, cs_ref, xc_ref, xs_ref, *, groups):
    h = _modulate(x_ref[...], g_ref[...], m_ref[0:1, :], m_ref[1:2, :])
    gc = h.shape[1] // groups
    cs = cs_ref[...]
    for gi in range(groups):
        res = _dot(h[:, gi * gc:(gi + 1) * gc], cs)
        xc_ref[:, gi * gc:(gi + 1) * gc] = res[:, :gc].astype(xc_ref.dtype)
        xs_ref[:, gi * gc:(gi + 1) * gc] = res[:, gc:].astype(xs_ref.dtype)


def _chan_dft(x, g, mod, mod_base, mod_stride, cs, t, tm, groups, out_dtype):
    n, d = x.shape
    tpb = t // tm
    full = pl.BlockSpec((tm, d), lambda i: (i, 0))
    return pl.pallas_call(
        functools.partial(_chan_dft_kernel, groups=groups),
        grid=(n // tm,),
        in_specs=[full, pl.BlockSpec((1, d), lambda i: (0, 0)),
                  pl.BlockSpec((None, 6, d), lambda i: (mod_base + (i // tpb) * mod_stride, 0, 0)),
                  pl.BlockSpec(cs.shape, lambda i: (0, 0))],
        out_specs=[full, full],
        out_shape=[jax.ShapeDtypeStruct((n, d), out_dtype)] * 2,
        compiler_params=_cparams("parallel"),
        name="chan_dft",
    )(x, g.reshape(1, d), mod, cs)


LANES = 128
SUBLANES = 8


def _dft_stage_a_kernel(m_ref, xc_ref, xs_ref, ar_ref, ai_ref):
    m = m_ref[...]
    n1 = ar_ref.shape[0]
    for v in range(SUBLANES):
        rhs = jnp.concatenate([xc_ref[:, v, :], xs_ref[:, v, :]], axis=0).astype(BF16)
        res = jnp.dot(m, rhs, preferred_element_type=F32)
        ar_ref[:, v, :] = res[:n1]
        ai_ref[:, v, :] = res[n1:]


def _dft_stage_a(m, xc, xs, batch, t):
    n, d = xc.shape
    n1 = t // LANES
    nu = LANES // SUBLANES
    view = (batch, n1, nu, SUBLANES, d)
    blk = pl.BlockSpec((None, n1, None, SUBLANES, d), lambda b, u: (b, 0, u, 0, 0))
    return pl.pallas_call(
        _dft_stage_a_kernel,
        grid=(batch, nu),
        in_specs=[pl.BlockSpec(m.shape, lambda b, u: (0, 0)), blk, blk],
        out_specs=[blk, blk],
        out_shape=[jax.ShapeDtypeStruct(view, F32)] * 2,
        compiler_params=_cparams("parallel", "parallel"),
        name="dft_stage_a",
    )(m, xc.reshape(view), xs.reshape(view))


def _dft_stage_c_kernel(cs_ref, twc_ref, tws_ref, ar_ref, ai_ref, y_ref):
    cs = cs_ref[...]
    for w in range(SUBLANES):
        ar, ai = ar_ref[w], ai_ref[w]
        cw, sw = twc_ref[w], tws_ref[w]
        b = jnp.concatenate([ar * cw + ai * sw, ai * cw - ar * sw], axis=0)
        y_ref[:, w, :] = _dot(cs, b)


def _dft_stage_c(cs, twc, tws, ar, ai, batch, t):
    d = ar.shape[-1]
    n1 = t // LANES
    ng = n1 // SUBLANES
    a_view = (batch, n1, LANES, d)
    a_blk = pl.BlockSpec((None, SUBLANES, LANES, d), lambda b, g: (b, g, 0, 0))
    tw_blk = pl.BlockSpec((SUBLANES, LANES, 1), lambda b, g: (g, 0, 0))
    return pl.pallas_call(
        _dft_stage_c_kernel,
        grid=(batch, ng),
        in_specs=[pl.BlockSpec(cs.shape, lambda b, g: (0, 0)), tw_blk, tw_blk, a_blk, a_blk],
        out_specs=pl.BlockSpec((None, LANES, None, SUBLANES, d), lambda b, g: (b, 0, g, 0, 0)),
        out_shape=jax.ShapeDtypeStruct((batch, LANES, ng, SUBLANES, d), F32),
        compiler_params=_cparams("parallel", "parallel"),
        name="dft_stage_c",
    )(cs, twc, tws, ar.reshape(a_view), ai.reshape(a_view)).reshape(batch * t, d)


def _resid_matmul_kernel(y_ref, x_ref, m_ref, w_ref, o_ref):
    o_ref[...] = x_ref[...] + m_ref[2:3, :] * _dot(y_ref[...], w_ref[...])


def _resid_matmul(y, x, mod, mod_base, mod_stride, w, t, tm):
    n, d = x.shape
    tpb = t // tm
    full = pl.BlockSpec((tm, d), lambda i: (i, 0))
    return pl.pallas_call(
        _resid_matmul_kernel,
        grid=(n // tm,),
        in_specs=[full, full,
                  pl.BlockSpec((None, 6, d), lambda i: (mod_base + (i // tpb) * mod_stride, 0, 0)),
                  pl.BlockSpec(w.shape, lambda i: (0, 0))],
        out_specs=full,
        out_shape=jax.ShapeDtypeStruct((n, d), F32),
        input_output_aliases={1: 0},
        compiler_params=_cparams("parallel"),
        name="resid_matmul",
    )(y, x, mod, w)


def _two_stage_tables(t):
    n1 = t // LANES
    c1, s1 = _trig(n1, n1, n1)
    m = jnp.concatenate([jnp.concatenate([c1, -s1], axis=1), jnp.concatenate([-s1, -c1], axis=1)], axis=0)
    c2, s2 = _trig(LANES, LANES, LANES)
    twc, tws = _trig(n1, LANES, t)
    return ((m * t ** -0.5).astype(BF16), jnp.concatenate([c2, s2], axis=1).astype(BF16),
            twc[:, :, None], tws[:, :, None])


def _time_dft_kernel(ct_ref, st_ref, xc_ref, xs_ref, x_ref, m_ref, wf_ref, o_ref, acc_ref):
    ki = pl.program_id(2)

    @pl.when(ki == 0)
    def _():
        acc_ref[...] = jnp.zeros_like(acc_ref)

    acc_ref[...] += (jnp.dot(ct_ref[...], xc_ref[...], preferred_element_type=F32)
                     - jnp.dot(st_ref[...], xs_ref[...], preferred_element_type=F32))

    @pl.when(ki == pl.num_programs(2) - 1)
    def _():
        o_ref[...] = x_ref[...] + m_ref[2:3, :] * _dot(acc_ref[...], wf_ref[...])


def _time_dft(tabs, xc, xs, x, mod, mod_base, mod_stride, wf, batch, t, tf, tk):
    ct, st = tabs
    n, d = x.shape
    nf, nk = t // tf, t // tk
    return pl.pallas_call(
        _time_dft_kernel,
        grid=(batch, nf, nk),
        in_specs=[pl.BlockSpec((tf, tk), lambda b, f, k: (f, k)),
                  pl.BlockSpec((tf, tk), lambda b, f, k: (f, k)),
                  pl.BlockSpec((tk, d), lambda b, f, k: (b * nk + k, 0)),
                  pl.BlockSpec((tk, d), lambda b, f, k: (b * nk + k, 0)),
                  pl.BlockSpec((tf, d), lambda b, f, k: (b * nf + f, 0)),
                  pl.BlockSpec((None, 6, d), lambda b, f, k: (mod_base + b * mod_stride, 0, 0)),
                  pl.BlockSpec((d, d), lambda b, f, k: (0, 0))],
        out_specs=pl.BlockSpec((tf, d), lambda b, f, k: (b * nf + f, 0)),
        out_shape=jax.ShapeDtypeStruct((n, d), F32),
        scratch_shapes=[pltpu.VMEM((tf, d), F32)],
        input_output_aliases={4: 0},
        compiler_params=_cparams("parallel", "parallel", "arbitrary"),
        name="time_dft",
    )(ct, st, xc, xs, x, mod, wf)


def _trig(rows, cols, period):
    prod = jnp.bitwise_and(jnp.arange(rows, dtype=jnp.int32)[:, None] * jnp.arange(cols, dtype=jnp.int32)[None, :],
                           period - 1)
    ang = prod.astype(F32) * (2.0 * math.pi / period)
    return jnp.cos(ang), jnp.sin(ang)


def _dft_table_kernel(ca_ref, sa_ref, cb_ref, sb_ref, ct_ref, st_ref):
    ca, sa = ca_ref[...], sa_ref[...]
    cb, sb = cb_ref[...], sb_ref[...]
    for j in range(ca.shape[1]):
        a, s = ca[:, j:j + 1], sa[:, j:j + 1]
        ct_ref[:, j * LANES:(j + 1) * LANES] = (a * cb - s * sb).astype(ct_ref.dtype)
        st_ref[:, j * LANES:(j + 1) * LANES] = (s * cb + a * sb).astype(st_ref.dtype)


def _time_tables(t, scale):
    hi_n = t // LANES
    ca, sa = _trig(t, hi_n, hi_n)
    cb, sb = _trig(t, LANES, t)
    tf = min(t, 256)
    small = pl.BlockSpec((tf, hi_n), lambda f: (f, 0))
    lanes = pl.BlockSpec((tf, LANES), lambda f: (f, 0))
    wide = pl.BlockSpec((tf, t), lambda f: (f, 0))
    return pl.pallas_call(
        _dft_table_kernel,
        grid=(t // tf,),
        in_specs=[small, small, lanes, lanes],
        out_specs=[wide, wide],
        out_shape=[jax.ShapeDtypeStruct((t, t), BF16)] * 2,
        compiler_params=_cparams("parallel"),
        name="dft_table",
    )(ca * scale, sa * scale, cb, sb)


def _router_kernel(x_ref, g_ref, m_ref, wr_ref, br_ref, tri_ref, ids_out, gates_out, cnt_out, tbase_out, trun_out,
                   base_ref):
    i = pl.program_id(0)

    @pl.when(i == 0)
    def _():
        base_ref[...] = jnp.zeros_like(base_ref)

    h2 = _modulate(x_ref[...], g_ref[...], m_ref[3:4, :], m_ref[4:5, :])
    h_hi = h2.astype(BF16)
    hs = (h_hi, (h2 - h_hi.astype(F32)).astype(BF16))
    lt = jnp.zeros((wr_ref.shape[1], h2.shape[0]), F32)
    for ia, ib in ((0, 0), (0, 1), (1, 0)):
        lt = lt + lax.dot_general(wr_ref[ib], hs[ia], (((1,), (1,)), ((), ())), preferred_element_type=F32)
    logits = lt[0:ROUTE_ROWS] + br_ref[0:ROUTE_ROWS, :]

    tm = logits.shape[1]
    row_i = lax.broadcasted_iota(jnp.int32, (ROUTE_ROWS, tm), 0)
    row = row_i.astype(F32)
    neg = -jnp.inf
    far = float(ROUTE_ROWS)
    gmask = row_i < MOE_GROUPS
    lc = jnp.where(gmask, logits, neg)
    mc = jnp.max(lc, axis=0, keepdims=True)
    sc = jnp.sum(jnp.where(gmask, jnp.exp(logits - mc), 0.0), axis=0, keepdims=True)
    g_val = 1.0 / sc
    g_idx = jnp.min(jnp.where(lc == mc, row, far), axis=0, keepdims=True)
    lgroup = jnp.where(row_i >= MOE_GROUPS, jnp.right_shift(row_i - MOE_GROUPS, 3), -1).astype(F32)
    sel = lgroup == g_idx
    lf = jnp.where(sel, logits, neg)
    m1 = jnp.max(lf, axis=0, keepdims=True)
    ef = jnp.where(sel, jnp.exp(logits - m1), 0.0)
    p = ef / jnp.sum(ef, axis=0, keepdims=True)
    p1 = jnp.where(sel, p, -1.0)
    v1 = jnp.max(p1, axis=0, keepdims=True)
    i1 = jnp.min(jnp.where(p1 == v1, row, far), axis=0, keepdims=True)
    p2 = jnp.where(row == i1, -1.0, p1)
    v2 = jnp.max(p2, axis=0, keepdims=True)
    i2 = jnp.min(jnp.where(p2 == v2, row, far), axis=0, keepdims=True)
    denom = v1 + v2
    gate1 = g_val * v1 / denom
    gate2 = g_val * v2 / denom

    tri = tri_ref[...]
    oh1 = jnp.where(row == i1, 1.0, 0.0)
    oh2 = jnp.where(row == i2, 1.0, 0.0)
    tot1 = jnp.sum(oh1, axis=1, keepdims=True)
    tot2 = jnp.sum(oh2, axis=1, keepdims=True)
    run = jnp.floor((tot1 + tot2 + (SUBLANES - 1)) * (1.0 / SUBLANES))
    er = lax.broadcasted_iota(jnp.int32, (ROUTE_ROWS, ROUTE_ROWS), 0)
    ec = lax.broadcasted_iota(jnp.int32, (ROUTE_ROWS, ROUTE_ROWS), 1)
    before = jnp.where(ec < er, 1.0, 0.0).astype(BF16)
    run_b = jnp.broadcast_to(run, (ROUTE_ROWS, LANES))
    off = jnp.dot(before, run_b.astype(BF16), preferred_element_type=F32)[:, 0:1] * SUBLANES
    c1 = jnp.dot(oh1.astype(BF16), tri, preferred_element_type=F32)
    c2 = jnp.dot(oh2.astype(BF16), tri, preferred_element_type=F32)
    pos1 = jnp.sum(oh1 * (off + c1), axis=0, keepdims=True)
    pos2 = jnp.sum(oh2 * (off + tot1 + c2), axis=0, keepdims=True)
    base = base_ref[...]
    tbase_out[...] = base
    trun_out[...] = run_b * SUBLANES
    base = base + run_b * SUBLANES
    base_ref[...] = base
    cnt_out[...] = base
    zrow = jnp.zeros((4, tm), F32)
    ids_out[...] = jnp.concatenate([i1 - MOE_GROUPS, i2 - MOE_GROUPS, pos1, pos2, zrow], axis=0).astype(jnp.int32)
    gates_out[...] = jnp.concatenate([gate1, gate2, zrow, zrow[0:2]], axis=0)


def _router(x, g, mod, mod_base, mod_stride, wr3, br, tri, t, tm):
    n, d = x.shape
    tpb = t // tm
    full = pl.BlockSpec((tm, d), lambda i: (i, 0))
    lanes = pl.BlockSpec((8, tm), lambda i: (0, i))
    cnt = pl.BlockSpec((ROUTE_ROWS, LANES), lambda i: (0, 0))
    per_tile = pl.BlockSpec((None, ROUTE_ROWS, LANES), lambda i: (i, 0, 0))
    tile_sd = jax.ShapeDtypeStruct((n // tm, ROUTE_ROWS, LANES), F32)
    return pl.pallas_call(
        _router_kernel,
        grid=(n // tm,),
        in_specs=[full, pl.BlockSpec((1, d), lambda i: (0, 0)),
                  pl.BlockSpec((None, 6, d), lambda i: (mod_base + (i // tpb) * mod_stride, 0, 0)),
                  pl.BlockSpec(wr3.shape, lambda i: (0, 0, 0)),
                  pl.BlockSpec(br.shape, lambda i: (0, 0)),
                  pl.BlockSpec((tm, tm), lambda i: (0, 0))],
        out_specs=[lanes, lanes, cnt, per_tile, per_tile],
        out_shape=[jax.ShapeDtypeStruct((8, n), jnp.int32),
                   jax.ShapeDtypeStruct((8, n), F32), jax.ShapeDtypeStruct((ROUTE_ROWS, LANES), F32),
                   tile_sd, tile_sd],
        scratch_shapes=[pltpu.VMEM((ROUTE_ROWS, LANES), F32)],
        compiler_params=_cparams("arbitrary"),
        name="moe_router",
    )(x, g.reshape(1, d), mod, wr3, br, tri)


RUN_SLACK = 256
RUN_COMMON = 64


def _run_copies(len_ref, pos_ref, dst_ref, tile, n_exp, max_len, make_copy, wait):
    sizes = []
    size = max_len
    while size >= SUBLANES:
        sizes.append(size)
        size //= 2

    def pieces(n, pos, row, some_sizes):
        for size in some_sizes:
            done = jnp.bitwise_and(n, -2 * size)

            @pl.when(jnp.bitwise_and(n, size) != 0)
            def _():
                cp = make_copy(pl.multiple_of(pos + done, SUBLANES), pl.multiple_of(row + done, SUBLANES), size)
                if wait:
                    cp.wait()
                else:
                    cp.start()

    small = [s for s in sizes if s < RUN_COMMON]

    def body(e, carry):
        idx = tile * n_exp + e
        n, pos, row = len_ref[idx], pos_ref[idx], dst_ref[idx]

        @pl.when(n >= RUN_COMMON)
        def _():
            pieces(n, pos, row, [s for s in sizes if s >= RUN_COMMON])

        pieces(n, pos, row, small)
        return carry

    lax.fori_loop(0, n_exp, body, 0)


def _dispatch_kernel(len_ref, pos_ref, dst_ref, zb_ref, x_ref, g_ref, m_ref, q_ref, xbuf_ref,
                     s_ref, zero_ref, sem, zsem, *, tm, n_exp):
    step = pl.program_id(0)

    def zero_copy(s):
        start = pl.multiple_of(zb_ref[s] * MOE_BLOCK, MOE_BLOCK)
        return pltpu.make_async_copy(zero_ref, xbuf_ref.at[pl.ds(start, MOE_BLOCK)], zsem)

    @pl.when(step == 0)
    def _():
        zero_ref[...] = jnp.zeros_like(zero_ref)
        for s in range(zb_ref.shape[0]):
            @pl.when(zb_ref[s] >= 0)
            def _():
                zero_copy(s).start()
        for s in range(zb_ref.shape[0]):
            @pl.when(zb_ref[s] >= 0)
            def _():
                zero_copy(s).wait()

    h2 = _modulate(x_ref[...], g_ref[...], m_ref[3:4, :], m_ref[4:5, :]).astype(BF16)
    srows = s_ref.shape[1]
    pos = lax.broadcasted_iota(jnp.int32, (srows, tm), 0)
    perm = jnp.where((pos == q_ref[2:3, :]) | (pos == q_ref[3:4, :]), 1.0, 0.0).astype(BF16)
    slot = lax.rem(step, 2)
    s_ref[slot] = jnp.dot(perm, h2, preferred_element_type=F32)

    def run_copy(buf_slot):
        def make(tile_pos, buf_row, size):
            return pltpu.make_async_copy(s_ref.at[buf_slot, pl.ds(tile_pos, size)],
                                         xbuf_ref.at[pl.ds(buf_row, size)], sem.at[buf_slot])
        return make

    _run_copies(len_ref, pos_ref, dst_ref, step, n_exp, 2 * tm, run_copy(slot), False)

    @pl.when(step > 0)
    def _():
        _run_copies(len_ref, pos_ref, dst_ref, step - 1, n_exp, 2 * tm, run_copy(1 - slot), True)

    @pl.when(step == pl.num_programs(0) - 1)
    def _():
        _run_copies(len_ref, pos_ref, dst_ref, step, n_exp, 2 * tm, run_copy(slot), True)


def _dispatch(runs, zblocks, x, g, mod, mod_base, mod_stride, ids, rows, n_exp, t, tm):
    n, d = x.shape
    tpb = t // tm
    cmap = lambda i, *_: (0, 0)
    return pl.pallas_call(
        functools.partial(_dispatch_kernel, tm=tm, n_exp=n_exp),
        grid_spec=pltpu.PrefetchScalarGridSpec(
            num_scalar_prefetch=4,
            grid=(n // tm,),
            in_specs=[pl.BlockSpec((tm, d), lambda i, *_: (i, 0)),
                      pl.BlockSpec((1, d), cmap),
                      pl.BlockSpec((None, 6, d), lambda i, *_: (mod_base + (i // tpb) * mod_stride, 0, 0)),
                      pl.BlockSpec((8, tm), lambda i, *_: (0, i))],
            out_specs=pl.BlockSpec(memory_space=pl.ANY),
            scratch_shapes=[pltpu.VMEM((2, 2 * tm + RUN_SLACK, d), F32), pltpu.VMEM((MOE_BLOCK, d), F32),
                            pltpu.SemaphoreType.DMA((2,)), pltpu.SemaphoreType.DMA],
        ),
        out_shape=jax.ShapeDtypeStruct((rows, d), F32),
        compiler_params=_cparams("arbitrary"),
        name="moe_dispatch",
    )(*runs, zblocks, x, g.reshape(1, d), mod, ids)


def _expert_kernel(be_ref, na_ref, x_ref, w1_ref, w3_ref, w2_ref, o_ref):
    active = pl.program_id(0) < na_ref[0]

    @pl.when(active)
    def _():
        xb = x_ref[...].astype(BF16)
        h1 = jnp.dot(xb, w1_ref[...].astype(BF16), preferred_element_type=F32)
        h3 = jnp.dot(xb, w3_ref[...].astype(BF16), preferred_element_type=F32)
        act = h1 * _sigmoid(h1) * h3
        o_ref[...] = _dot(act, w2_ref[...])

    @pl.when(jnp.logical_not(active))
    def _():
        o_ref[...] = jnp.zeros_like(o_ref)


def _experts(block_e, nact, xbuf, w1, w3, w2, layer):
    rows, d = xbuf.shape
    nb = rows // MOE_BLOCK
    de = w1.shape[3]
    blk = lambda i, be, na: (jnp.minimum(i, na[0] - 1), 0)
    wmap = lambda i, be, na: (layer, be[jnp.minimum(i, na[0] - 1)], 0, 0)
    return pl.pallas_call(
        _expert_kernel,
        grid_spec=pltpu.PrefetchScalarGridSpec(
            num_scalar_prefetch=2,
            grid=(nb,),
            in_specs=[pl.BlockSpec((MOE_BLOCK, d), blk),
                      pl.BlockSpec((None, None, d, de), wmap),
                      pl.BlockSpec((None, None, d, de), wmap),
                      pl.BlockSpec((None, None, de, d), wmap)],
            out_specs=pl.BlockSpec((MOE_BLOCK, d), lambda i, be, na: (i, 0)),
        ),
        out_shape=jax.ShapeDtypeStruct((rows, d), F32),
        compiler_params=_cparams("arbitrary"),
        name="moe_experts",
    )(block_e, nact, xbuf, w1, w3, w2)


def _combine_kernel(len_ref, pos_ref, dst_ref, y_ref, q_ref, gates_ref, x_ref, m_ref, fn_ref, o_ref, w_ref, sem,
                    *, tm, n_exp, final):
    step = pl.program_id(0)

    slot = lax.rem(step, 2)

    def run_copy(buf_slot):
        def make(tile_pos, buf_row, size):
            return pltpu.make_async_copy(y_ref.at[pl.ds(buf_row, size)],
                                         w_ref.at[buf_slot, pl.ds(tile_pos, size)], sem.at[buf_slot])
        return make

    @pl.when(step == 0)
    def _():
        w_ref[...] = jnp.zeros_like(w_ref)
        _run_copies(len_ref, pos_ref, dst_ref, step, n_exp, 2 * tm, run_copy(slot), False)

    @pl.when(step + 1 < pl.num_programs(0))
    def _():
        _run_copies(len_ref, pos_ref, dst_ref, step + 1, n_exp, 2 * tm, run_copy(1 - slot), False)

    _run_copies(len_ref, pos_ref, dst_ref, step, n_exp, 2 * tm, run_copy(slot), True)

    wrows = w_ref.shape[1]
    pos = lax.broadcasted_iota(jnp.int32, (tm, wrows), 1)
    q = q_ref[...]
    sel = jnp.concatenate([jnp.where(pos == q[:, 0:1], 1.0, 0.0), jnp.where(pos == q[:, 1:2], 1.0, 0.0)],
                          axis=0).astype(BF16)
    picked = jnp.dot(sel, w_ref[slot].astype(BF16), preferred_element_type=F32)
    gates = gates_ref[...]
    y = gates[:, 0:1] * picked[:tm] + gates[:, 1:2] * picked[tm:]
    out = x_ref[...] + m_ref[5:6, :] * y
    if final:
        ms = jnp.mean(out * out, axis=-1, keepdims=True)
        out = out * lax.rsqrt(ms + NORM_EPS) * fn_ref[...]
    o_ref[...] = out


def _combine(runs, ybuf, qcols, gates, x, mod, mod_base, mod_stride, fnorm, n_exp, t, tm, final):
    n, d = x.shape
    tpb = t // tm
    full = pl.BlockSpec((tm, d), lambda i, *_: (i, 0))
    pair = pl.BlockSpec((tm, 2), lambda i, *_: (i, 0))
    return pl.pallas_call(
        functools.partial(_combine_kernel, tm=tm, n_exp=n_exp, final=final),
        grid_spec=pltpu.PrefetchScalarGridSpec(
            num_scalar_prefetch=3,
            grid=(n // tm,),
            in_specs=[pl.BlockSpec(memory_space=pl.ANY), pair, pair, full,
                      pl.BlockSpec((None, 6, d), lambda i, *_: (mod_base + (i // tpb) * mod_stride, 0, 0)),
                      pl.BlockSpec((1, d), lambda i, *_: (0, 0))],
            out_specs=full,
            scratch_shapes=[pltpu.VMEM((2, 2 * tm + RUN_SLACK, d), F32), pltpu.SemaphoreType.DMA((2,))],
        ),
        out_shape=jax.ShapeDtypeStruct((n, d), F32),
        input_output_aliases={6: 0},
        compiler_params=_cparams("arbitrary"),
        name="moe_combine",
    )(*runs, ybuf, qcols, gates, x, mod, fnorm)


def _moe(x, g, mod, mod_base, mod_stride, rt, w1, w3, w2, layer, fnorm, t, tm, final):
    n, d = x.shape
    n_exp = w1.shape[1]
    wr3, br, tri = rt
    assert n_exp * (SUBLANES - 1) <= RUN_SLACK
    ids, gates, counts, tbase, trun = _router(x, g, mod, mod_base, mod_stride, wr3, br, tri, t, tm)
    experts = slice(MOE_GROUPS, MOE_GROUPS + n_exp)
    counts = counts[experts, 0].astype(jnp.int32)
    padded = (counts + MOE_BLOCK - 1) // MOE_BLOCK * MOE_BLOCK
    pend = jnp.cumsum(padded)
    pstart = pend - padded
    run_len = trun[:, experts, 0].astype(jnp.int32)
    run_pos = jnp.cumsum(run_len, axis=1) - run_len
    run_dst = pstart[None, :] + tbase[:, experts, 0].astype(jnp.int32)
    runs = (run_len.reshape(-1), run_pos.reshape(-1), run_dst.reshape(-1))
    ntiles = n // tm
    nb = -(-(2 * n + ntiles * n_exp * (SUBLANES - 1)) // MOE_BLOCK) + n_exp
    blk_start = jnp.arange(nb, dtype=jnp.int32) * MOE_BLOCK
    block_e = jnp.minimum(jnp.sum((pend[None, :] <= blk_start[:, None]).astype(jnp.int32), axis=1), n_exp - 1)
    nact = pend[-1:] // MOE_BLOCK
    partial = jnp.where(padded > counts, pend // MOE_BLOCK - 1, -1)
    tail = nact + jnp.arange(nb - (2 * n) // MOE_BLOCK, dtype=jnp.int32)
    zblocks = jnp.concatenate([partial, jnp.where(tail < nb, tail, -1)]).astype(jnp.int32)
    xbuf = _dispatch(runs, zblocks, x, g, mod, mod_base, mod_stride, ids, nb * MOE_BLOCK, n_exp, t, tm)
    ybuf = _experts(block_e, nact, xbuf, w1, w3, w2, layer)
    return _combine(runs, ybuf, ids[2:4].T, gates[0:2].T, x, mod, mod_base, mod_stride, fnorm, n_exp, t, tm, final)


def _seg_ones(width):
    idx = np.arange(width) // HEAD_DIM
    return jnp.asarray((idx[:, None] == idx[None, :]).astype(np.float32), dtype=BF16)


def _even_weights(j, mu_x, mu_p, decay_w0, decay_w1, decay_w2, lr_a0, lr_a1, lr_a2, gate_g1, gate_g2,
                  k_k, k_a, r_k, pool_w, pool_scale, seg):
    wa = mu_p.shape[-1]

    def cat1(w):
        return jnp.concatenate([w[0], w[1]], axis=1).astype(BF16)

    def pad2(w):
        z = jnp.zeros_like(w[0])
        return jnp.stack([jnp.concatenate([w[0], z], axis=0), jnp.concatenate([z, w[1]], axis=0)]).astype(BF16)

    return (mu_x[j], mu_p[j], decay_w0[j], cat1(decay_w1[j]), pad2(decay_w2[j]),
            lr_a0[j], cat1(lr_a1[j]), pad2(lr_a2[j]),
            gate_g1[j].astype(BF16), gate_g2[j].astype(BF16),
            k_k[j].reshape(1, wa), k_a[j].reshape(1, wa), r_k[j].reshape(1, wa),
            pool_w[j].astype(BF16), pool_scale[j].reshape(1, wa), seg)


def kernel(x, c, ctx, c_ctx, ada_w, ada_b, norm_mix, norm_ffn, w_in, mu_x, mu_p, decay_w0, decay_w1, decay_w2, lr_a0, lr_a1, lr_a2, gate_g1, gate_g2, k_k, k_a, r_k, gn_w, gn_b, pool_w, pool_scale, w_out, w_fourier, router_c, router_c_b, router_f, router_f_b, moe_w1, moe_w3, moe_w2, final_norm):
    batch, t, d = x.shape
    tc = ctx.shape[1]
    depth = ada_w.shape[0]
    wa = d // 2
    n, ncx = batch * t, batch * tc
    assert batch <= 4 and t % 512 == 0 and tc % 256 == 0 and d % 512 == 0

    cond8 = jnp.zeros((8, d), F32).at[:batch].set(c).at[4].set(c_ctx)
    mod_all = _ada(cond8, ada_w, ada_b).reshape(depth, 8, 6, d)
    seg = _seg_ones(wa)
    fnorm = final_norm.reshape(1, d)

    fgroups = 4
    gc = d // fgroups
    cc, sc = _trig(gc, gc, gc)
    cs = (jnp.concatenate([cc, sc], axis=1) * gc ** -0.5).astype(BF16)
    stage_m, stage_cs, stage_twc, stage_tws = _two_stage_tables(t)
    tab_ctx = _time_tables(tc, tc ** -0.5)

    n_exp = moe_w1.shape[1]
    tri = {tm: jnp.asarray(np.triu(np.ones((tm, tm), np.float32), 1), dtype=BF16) for tm in (512, 256)}

    lat = x.reshape(n, d)
    cx = ctx.reshape(ncx, d)
    last_read = 2 * ((depth - 1) // 2)
    npair = wa // PAIR

    for i in range(depth):
        ctx_in = i <= last_read
        ctx_out = i < last_read
        mod = mod_all[i]
        j = i // 2
        if i % 2 == 0:
            wts = _even_weights(j, mu_x, mu_p, decay_w0, decay_w1, decay_w2, lr_a0, lr_a1, lr_a2,
                                gate_g1, gate_g2, k_k, k_a, r_k, pool_w, pool_scale, seg)
            w_in_b = w_in[j].astype(BF16)
            w_out_b = w_out[j].astype(BF16)
            gnw, gnb = gn_w[j].reshape(1, wa), gn_b[j].reshape(1, wa)
            pc = _proj(cx, norm_mix[i], mod, w_in_b, tc, 4, 0, 0, 256)
            fc = _rwkv_prep(cx, pc, norm_mix[i], mod, 4, 0, wts, batch, tc, False)
            ryc, pqc = _wkv_chunks(*fc[:6], batch, tc)
            g0 = jnp.zeros((2, batch, npair, PAIR, PAIR), F32)
            yc0, yc1, gctx = _wkv_scan(ryc, pqc, g0)
            pl_ = _proj(lat, norm_mix[i], mod, w_in_b, t, 0, 1, 0, 512)
            fl = _rwkv_prep(lat, pl_, norm_mix[i], mod, 0, 1, wts, batch, t, True)
            ryl, pql = _wkv_chunks(*fl[:6], batch, t)
            yl0, yl1, _ = _wkv_scan(ryl, pql, gctx)
            lat = _mix_out(yl0.reshape(n, wa), yl1.reshape(n, wa), fl[6], fl[7], fl[8], lat, mod, 0, 1,
                           gnw, gnb, seg, w_out_b, t, 512)
            if ctx_out:
                cx = _mix_out(yc0.reshape(ncx, wa), yc1.reshape(ncx, wa), fc[6], fc[7], fc[8], cx, mod, 4, 0,
                              gnw, gnb, seg, w_out_b, tc, 256)
        else:
            wf_b = w_fourier[j].astype(BF16)
            xc, xs = _chan_dft(lat, norm_mix[i], mod, 0, 1, cs, t, 512, fgroups, F32)
            ar, ai = _dft_stage_a(stage_m, xc, xs, batch, t)
            yf = _dft_stage_c(stage_cs, stage_twc, stage_tws, ar, ai, batch, t)
            lat = _resid_matmul(yf, lat, mod, 0, 1, wf_b, t, 512)
            if ctx_out:
                xcc, xsc = _chan_dft(cx, norm_mix[i], mod, 4, 0, cs, tc, 256, fgroups, BF16)
                cx = _time_dft(tab_ctx, xcc, xsc, cx, mod, 4, 0, wf_b, batch, tc, tc, tc)
        wr = jnp.zeros((LANES, d), F32).at[:MOE_GROUPS].set(router_c[i].T).at[MOE_GROUPS:MOE_GROUPS + n_exp].set(router_f[i].T)
        br = jnp.zeros((LANES, 1), F32).at[:MOE_GROUPS, 0].set(router_c_b[i]).at[MOE_GROUPS:MOE_GROUPS + n_exp, 0].set(router_f_b[i])
        wr3 = jnp.stack(_split3(wr))
        final = i == depth - 1
        lat = _moe(lat, norm_ffn[i], mod, 0, 1, (wr3, br, tri[512]), moe_w1, moe_w3, moe_w2, i, fnorm,
                   t, 512, final)
        if ctx_out:
            cx = _moe(cx, norm_ffn[i], mod, 4, 0, (wr3, br, tri[256]), moe_w1, moe_w3, moe_w2, i, fnorm,
                      tc, 256, False)
    return lat.reshape(batch, t, d)
```

```python
import functools
import math

import jax
import jax.numpy as jnp
import numpy as np
from jax import lax
from jax.experimental import pallas as pl
from jax.experimental.pallas import tpu as pltpu

F32 = jnp.float32
BF16 = jnp.bfloat16

GRID_W = 64
HEAD_DIM = 64
CHUNK = 64
PAIR = 2 * HEAD_DIM
NORM_EPS = 1e-6
GN_EPS = 64e-5
POOL_WINDOWS = (2, 4, 8, 16)
MOE_GROUPS = 4
ROUTE_ROWS = 40
EXPERTS_PER_GROUP = 8
MOE_BLOCK = 512
VMEM_LIMIT = 56 * 1024 * 1024


def _cparams(*sem):
    return pltpu.CompilerParams(dimension_semantics=tuple(sem), vmem_limit_bytes=VMEM_LIMIT)


def _dot(a, b):
    return jnp.dot(a.astype(BF16), b.astype(BF16), preferred_element_type=F32)


def _dot_nt(a, b):
    return lax.dot_general(a.astype(BF16), b.astype(BF16), (((1,), (1,)), ((), ())),
                           preferred_element_type=F32)


def _dot_tn(a, b):
    return lax.dot_general(a.astype(BF16), b.astype(BF16), (((0,), (0,)), ((), ())),
                           preferred_element_type=F32)


def _split3(x):
    hi = x.astype(BF16)
    r1 = x - hi.astype(F32)
    mid = r1.astype(BF16)
    lo = (r1 - mid.astype(F32)).astype(BF16)
    return hi, mid, lo


def _sigmoid(x):
    return 1.0 / (1.0 + jnp.exp(-x))


def _modulate(x, g, shift, scale):
    ms = jnp.mean(x * x, axis=-1, keepdims=True)
    return x * lax.rsqrt(ms + NORM_EPS) * g * (1.0 + scale) + shift


def _ada_kernel(c_ref, w_ref, b_ref, o_ref):
    c = c_ref[...]
    s = c * _sigmoid(c)
    o_ref[...] = _dot(s, w_ref[...]) + b_ref[...]


def _ada(cond8, ada_w, ada_b):
    depth, d, n6 = ada_w.shape
    tn = 1536
    return pl.pallas_call(
        _ada_kernel,
        grid=(depth, n6 // tn),
        in_specs=[pl.BlockSpec((8, d), lambda l, j: (0, 0)),
                  pl.BlockSpec((None, d, tn), lambda l, j: (l, 0, j)),
                  pl.BlockSpec((None, 1, tn), lambda l, j: (l, 0, j))],
        out_specs=pl.BlockSpec((None, 8, tn), lambda l, j: (l, 0, j)),
        out_shape=jax.ShapeDtypeStruct((depth, 8, n6), F32),
        compiler_params=_cparams("parallel", "parallel"),
        name="ada_mod",
    )(cond8, ada_w, ada_b.reshape(depth, 1, n6))


def _proj_kernel(x_ref, g_ref, m_ref, w_ref, o_ref, *, shift_row, scale_row):
    h = _modulate(x_ref[...], g_ref[...], m_ref[shift_row:shift_row + 1, :], m_ref[scale_row:scale_row + 1, :])
    o_ref[...] = _dot(h, w_ref[...]).astype(o_ref.dtype)


def _proj(x, g, mod, w, rows_per_batch, mod_base, mod_stride, shift_row, tm, out_dtype=F32):
    n, d = x.shape
    nout = w.shape[1]
    tpb = rows_per_batch // tm
    return pl.pallas_call(
        functools.partial(_proj_kernel, shift_row=shift_row, scale_row=shift_row + 1),
        grid=(n // tm,),
        in_specs=[pl.BlockSpec((tm, d), lambda i: (i, 0)),
                  pl.BlockSpec((1, d), lambda i: (0, 0)),
                  pl.BlockSpec((None, 6, d), lambda i: (mod_base + (i // tpb) * mod_stride, 0, 0)),
                  pl.BlockSpec((d, nout), lambda i: (0, 0))],
        out_specs=pl.BlockSpec((tm, nout), lambda i: (i, 0)),
        out_shape=jax.ShapeDtypeStruct((n, nout), out_dtype),
        compiler_params=_cparams("parallel"),
        name="mod_proj",
    )(x, g.reshape(1, d), mod, w)


def _wkv_chunk_kernel(r_ref, v_ref, an_ref, lw_ref, kd_ref, bb_ref, ry_ref, pq_ref):
    d = pl.program_id(0)
    width = lw_ref.shape[1]
    nsub = lw_ref.shape[0] // CHUNK
    row = lax.broadcasted_iota(jnp.int32, (CHUNK, CHUNK), 0)
    col = lax.broadcasted_iota(jnp.int32, (CHUNK, CHUNK), 1)
    sgn = 1 - 2 * d
    tri = jnp.where((row - col) * sgn >= 0, 1.0, 0.0).astype(BF16)

    def chunk_feats(sc):
        rows = slice(sc * CHUNK, (sc + 1) * CHUNK)
        lw = lw_ref[rows, :]
        hi, mid, lo = _split3(lw)
        csum = (jnp.dot(tri, hi, preferred_element_type=F32)
                + jnp.dot(tri, mid, preferred_element_type=F32)
                + jnp.dot(tri, lo, preferred_element_type=F32))
        e_neg = jnp.exp(-csum)
        return (an_ref[rows, :] * jnp.exp(csum - lw), bb_ref[rows, :] * e_neg, kd_ref[rows, :] * e_neg,
                r_ref[rows, :] * jnp.exp(csum), v_ref[rows, :], jnp.exp(jnp.sum(lw, axis=0, keepdims=True)))

    feats = [chunk_feats(sc) for sc in range(nsub)]

    lane = lax.broadcasted_iota(jnp.int32, (1, PAIR), 1)
    head0 = lane < HEAD_DIM
    i2 = lax.broadcasted_iota(jnp.int32, (PAIR, PAIR), 0)
    j2 = lax.broadcasted_iota(jnp.int32, (PAIR, PAIR), 1)
    ti = jnp.bitwise_and(i2, CHUNK - 1)
    tj = jnp.bitwise_and(j2, CHUNK - 1)
    strict = (ti - tj) * sgn > 0
    incl = (ti - tj) * sgn >= 0
    eye = (i2 == j2).astype(F32)

    def stack(x):
        return jnp.concatenate([jnp.where(head0, x, 0.0), jnp.where(head0, 0.0, x)], axis=0)

    units = [(sc, p) for sc in range(nsub) for p in range(width // PAIR)]
    us = range(len(units))
    sls = [slice(p * PAIR, (p + 1) * PAIR) for _, p in units]
    a_s, b_s, k_s, r_s, v_s = ([stack(feats[sc][f][:, sls[u]]) for u, (sc, _) in enumerate(units)]
                               for f in range(5))
    m1 = [_dot_nt(jnp.concatenate([a_s[u], r_s[u]], axis=0), jnp.concatenate([b_s[u], k_s[u]], axis=0))
          for u in us]
    npow = [jnp.where(strict, m[:PAIR, :PAIR], 0.0) for m in m1]
    a_ak = [jnp.where(strict, m[:PAIR, PAIR:], 0.0) for m in m1]
    a_rb = [jnp.where(incl, m[PAIR:, :PAIR], 0.0) for m in m1]
    a_rk = [jnp.where(incl, m[PAIR:, PAIR:], 0.0) for m in m1]
    x = [jnp.concatenate([a_s[u], _dot(a_ak[u], v_s[u])], axis=1) for u in us]
    steps = int(math.log2(CHUNK))
    for it in range(steps):
        x = [x[u] + _dot(npow[u], x[u]) for u in us]
        if it + 1 < steps:
            npow = [_dot(npow[u], npow[u]) for u in us]
    for u, (sc, p) in enumerate(units):
        ry = jnp.concatenate([r_s[u], _dot(a_rk[u], v_s[u])], axis=1) + _dot(a_rb[u], x[u])
        ry_ref[sc, p] = ry.astype(ry_ref.dtype)
    for u, (sc, p) in enumerate(units):
        ge = feats[sc][5][:, sls[u]]
        pt = (eye + _dot_tn(x[u][:, :PAIR], b_s[u])) * ge
        qt = (_dot_tn(x[u][:, PAIR:], b_s[u]) + _dot_tn(v_s[u], k_s[u])) * ge
        pq_ref[sc, p] = jnp.concatenate([pt, qt], axis=0).astype(pq_ref.dtype)


def _wkv_chunks(r, v, an, lw, kd, bb, batch, t):
    n, w = r.shape
    nc = t // CHUNK
    npair = w // PAIR
    nsub = min(4, nc)
    ns = nc // nsub
    shared = pl.BlockSpec((nsub * CHUNK, w), lambda d, b, c: (b * ns + c, 0))
    perdir = pl.BlockSpec((None, nsub * CHUNK, w), lambda d, b, c: (d, b * ns + c, 0))
    return pl.pallas_call(
        _wkv_chunk_kernel,
        grid=(2, batch, ns),
        in_specs=[shared, shared, shared, perdir, perdir, perdir],
        out_specs=[pl.BlockSpec((None, None, nsub, npair, PAIR, 2 * PAIR), lambda d, b, c: (d, b, c, 0, 0, 0)),
                   pl.BlockSpec((None, None, nsub, npair, 2 * PAIR, PAIR), lambda d, b, c: (d, b, c, 0, 0, 0))],
        out_shape=[jax.ShapeDtypeStruct((2, batch, nc, npair, PAIR, 2 * PAIR), BF16),
                   jax.ShapeDtypeStruct((2, batch, nc, npair, 2 * PAIR, PAIR), BF16)],
        compiler_params=_cparams("parallel", "parallel", "parallel"),
        name="wkv_chunks",
    )(r, v, an, lw, kd, bb)


def _wkv_scan_kernel(ryf_ref, ryb_ref, pqf_ref, pqb_ref, g0_ref, yf_ref, yb_ref, gout_ref, g_ref):
    pos = pl.program_id(0)

    @pl.when(pos == 0)
    def _():
        g_ref[...] = g0_ref[...]

    batch, npair = g_ref.shape[1], g_ref.shape[2]
    for d, (ry_ref, pq_ref, y_ref) in enumerate(((ryf_ref, pqf_ref, yf_ref), (ryb_ref, pqb_ref, yb_ref))):
        for b in range(batch):
            for p in range(npair):
                g = g_ref[d, b, p]
                ry = ry_ref[b, p]
                pq = pq_ref[b, p]
                ys = _dot_nt(ry[:, :PAIR], g) + ry[:, PAIR:].astype(F32)
                y_ref[b, :, p * PAIR:(p + 1) * PAIR] = (ys[:CHUNK] + ys[CHUNK:]).astype(y_ref.dtype)
                g_ref[d, b, p] = _dot(g, pq[:PAIR]) + pq[PAIR:].astype(F32)

    @pl.when(pos == pl.num_programs(0) - 1)
    def _():
        gout_ref[...] = g_ref[...]


def _wkv_scan(ry, pq, g0):
    _, batch, nc, npair, _, _ = ry.shape
    w = npair * PAIR
    t = nc * CHUNK
    ry_f = pl.BlockSpec((None, batch, None, npair, PAIR, 2 * PAIR), lambda s: (0, 0, s, 0, 0, 0))
    ry_b = pl.BlockSpec((None, batch, None, npair, PAIR, 2 * PAIR), lambda s: (1, 0, nc - 1 - s, 0, 0, 0))
    pq_f = pl.BlockSpec((None, batch, None, npair, 2 * PAIR, PAIR), lambda s: (0, 0, s, 0, 0, 0))
    pq_b = pl.BlockSpec((None, batch, None, npair, 2 * PAIR, PAIR), lambda s: (1, 0, nc - 1 - s, 0, 0, 0))
    gspec = pl.BlockSpec((2, batch, npair, PAIR, PAIR), lambda s: (0, 0, 0, 0, 0))
    return pl.pallas_call(
        _wkv_scan_kernel,
        grid=(nc,),
        in_specs=[ry_f, ry_b, pq_f, pq_b, gspec],
        out_specs=[pl.BlockSpec((batch, CHUNK, w), lambda s: (0, s, 0)),
                   pl.BlockSpec((batch, CHUNK, w), lambda s: (0, nc - 1 - s, 0)),
                   gspec],
        out_shape=[jax.ShapeDtypeStruct((batch, t, w), BF16),
                   jax.ShapeDtypeStruct((batch, t, w), BF16),
                   jax.ShapeDtypeStruct((2, batch, npair, PAIR, PAIR), F32)],
        scratch_shapes=[pltpu.VMEM((2, batch, npair, PAIR, PAIR), F32)],
        compiler_params=_cparams("arbitrary"),
        name="wkv_scan",
    )(ry, ry, pq, pq, g0)


GRID_SHIFT = ((-1, "first_col"), (1, "last_col"), (-GRID_W, None), (GRID_W, None))
SEQ_SHIFT = ((-1, None), (1, None))


def _prep_kernel(xp_ref, xm_ref, xn_ref, pp_ref, pm_ref, pn_ref, g_ref, m_ref,
                 mux_ref, mup_ref, dw0_ref, dw1_ref, dw2_ref, la0_ref, la1_ref, la2_ref,
                 gg1_ref, gg2_ref, kk_ref, ka_ref, rk_ref, pw_ref, ps_ref, seg_ref,
                 r_out, v_out, an_out, lw_out, kd_out, bb_out, bonus_out, gate_out, yb_out,
                 hext_ref, pext_ref, *, parts, tm, pad, tiles_per_batch, seq_len):
    i = pl.program_id(0)
    tile = lax.rem(i, tiles_per_batch)
    keep_prev = jnp.where(tile == 0, 0.0, 1.0)
    keep_next = jnp.where(tile == tiles_per_batch - 1, 0.0, 1.0)
    g = g_ref[...]
    shift, scale = m_ref[0:1, :], m_ref[1:2, :]
    d_model = xm_ref.shape[1]
    wa = r_out.shape[1]

    hext_ref[0:pad, :] = _modulate(xp_ref[...], g, shift, scale) * keep_prev
    hext_ref[pad:pad + tm, :] = _modulate(xm_ref[...], g, shift, scale)
    hext_ref[pad + tm:, :] = _modulate(xn_ref[...], g, shift, scale) * keep_next
    pext_ref[0:pad, :] = pp_ref[...].astype(F32) * keep_prev
    pext_ref[pad:pad + tm, :] = pm_ref[...].astype(F32)
    pext_ref[pad + tm:, :] = pn_ref[...].astype(F32) * keep_next

    colidx = jnp.bitwise_and(lax.broadcasted_iota(jnp.int32, (tm, 1), 0), GRID_W - 1)

    def shifted(ext_ref, col0, width):
        pw = width // len(parts)
        outs = []
        for q, (off, mask) in enumerate(parts):
            blk = ext_ref[pad + off:pad + off + tm, col0 + q * pw:col0 + (q + 1) * pw]
            if mask == "first_col":
                blk = jnp.where(colidx == 0, 0.0, blk)
            elif mask == "last_col":
                blk = jnp.where(colidx == GRID_W - 1, 0.0, blk)
            outs.append(blk)
        return jnp.concatenate(outs, axis=1)

    h = hext_ref[pad:pad + tm, :]
    hx = shifted(hext_ref, 0, d_model) - h
    x_w = h + hx * mux_ref[0:1, :]
    x_a = h + hx * mux_ref[1:2, :]
    x_g = h + hx * mux_ref[2:3, :]
    zw_mid = jnp.tanh(_dot(x_w, dw1_ref[...]))
    xa_mid = _dot(x_a, la1_ref[...])
    gate_out[...] = _dot(_sigmoid(_dot(x_g, gg1_ref[...])), gg2_ref[...]).astype(gate_out.dtype)

    def mixed(n):
        p_n = pext_ref[pad:pad + tm, n * wa:(n + 1) * wa]
        return p_n + (shifted(pext_ref, n * wa, wa) - p_n) * mup_ref[n:n + 1, :]

    r, k, v = mixed(0), mixed(1), mixed(2)
    seg = seg_ref[...]
    kk = k * kk_ref[...]
    kk = kk * lax.rsqrt(jnp.maximum(_dot(kk * kk, seg), 1e-12))
    r_out[...] = r.astype(r_out.dtype)
    v_out[...] = v.astype(v_out.dtype)
    an_out[...] = (-kk).astype(an_out.dtype)
    ka = ka_ref[...]
    kd_sum = jnp.zeros_like(k)
    for d in range(2):
        zw = dw0_ref[d:d + 1, :] + _dot(zw_mid, dw2_ref[d])
        lw_out[d] = -math.exp(-0.5) * _sigmoid(zw)
        a_lr = _sigmoid(la0_ref[d:d + 1, :] + _dot(xa_mid, la2_ref[d]))
        kd = k * (1.0 + (a_lr - 1.0) * ka)
        kd_out[d] = kd.astype(kd_out.dtype)
        bb_out[d] = (kk * a_lr).astype(bb_out.dtype)
        kd_sum = kd_sum + kd
    bonus_out[...] = (_dot(r * kd_sum * rk_ref[...], seg) * v).astype(bonus_out.dtype)

    pos = tile * tm + lax.broadcasted_iota(jnp.int32, (tm, 1), 0)
    trow = lax.broadcasted_iota(jnp.int32, (tm, tm + 2 * pad), 0)
    srow = lax.broadcasted_iota(jnp.int32, (tm, tm + 2 * pad), 1) - pad
    gp = wa // len(POOL_WINDOWS)
    ybs = []
    for gi, win in enumerate(POOL_WINDOWS):
        half = win // 2
        c0 = 3 * wa + gi * gp
        band = jnp.where((srow >= trow - half) & (srow < trow + half), 1.0, 0.0).astype(BF16)
        sums = _dot(band, pext_ref[:, c0:c0 + gp])
        cnt = (jnp.minimum(pos + half, seq_len) - jnp.maximum(pos - half, 0)).astype(F32)
        diff = sums / cnt - pext_ref[pad:pad + tm, c0:c0 + gp]
        ybs.append(_dot(diff, pw_ref[gi]))
    yb_out[...] = (jnp.concatenate(ybs, axis=1) * ps_ref[...]).astype(yb_out.dtype)


def _rwkv_prep(x, proj, g, mod, mod_base, mod_stride, wts, batch, t, grid_mode):
    n, d = x.shape
    wa = d // 2
    pcols = proj.shape[1]
    tm = 256
    pad = GRID_W if grid_mode else 16
    parts = GRID_SHIFT if grid_mode else SEQ_SHIFT
    tpb = t // tm
    hb = tm // pad
    nhb = n // pad
    main = lambda i: (i, 0)
    prev = lambda i: (jnp.maximum(i * hb - 1, 0), 0)
    nxt = lambda i: (jnp.minimum((i + 1) * hb, nhb - 1), 0)
    full2 = lambda i: (0, 0)
    full3 = lambda i: (0, 0, 0)
    in_specs = [pl.BlockSpec((pad, d), prev), pl.BlockSpec((tm, d), main), pl.BlockSpec((pad, d), nxt),
                pl.BlockSpec((pad, pcols), prev), pl.BlockSpec((tm, pcols), main), pl.BlockSpec((pad, pcols), nxt),
                pl.BlockSpec((1, d), full2),
                pl.BlockSpec((None, 6, d), lambda i: (mod_base + (i // tpb) * mod_stride, 0, 0))]
    for a in wts:
        in_specs.append(pl.BlockSpec(a.shape, full2 if a.ndim == 2 else full3))
    one = pl.BlockSpec((tm, wa), main)
    two = pl.BlockSpec((2, tm, wa), lambda i: (0, i, 0))
    sd1 = jax.ShapeDtypeStruct((n, wa), BF16)
    sd2 = jax.ShapeDtypeStruct((2, n, wa), BF16)
    lw2 = jax.ShapeDtypeStruct((2, n, wa), F32)
    return pl.pallas_call(
        functools.partial(_prep_kernel, parts=parts, tm=tm, pad=pad, tiles_per_batch=tpb, seq_len=t),
        grid=(n // tm,),
        in_specs=in_specs,
        out_specs=[one, one, one, two, two, two, one, one, one],
        out_shape=[sd1, sd1, sd1, lw2, sd2, sd2, sd1, sd1, sd1],
        scratch_shapes=[pltpu.VMEM((tm + 2 * pad, d), F32), pltpu.VMEM((tm + 2 * pad, pcols), F32)],
        compiler_params=_cparams("parallel"),
        name="rwkv_prep",
    )(x, x, x, proj, proj, proj, g.reshape(1, d), mod, *wts)


def _mix_out_kernel(y0_ref, y1_ref, bonus_ref, gate_ref, yb_ref, x_ref, m_ref, gnw_ref, gnb_ref,
                    seg_ref, wout_ref, o_ref):
    y = y0_ref[...].astype(F32) + y1_ref[...].astype(F32)
    seg = seg_ref[...]
    inv = 1.0 / HEAD_DIM
    yh = y.astype(BF16)
    mu = (jnp.dot(yh, seg, preferred_element_type=F32) + _dot(y - yh.astype(F32), seg)) * inv
    dlt = y - mu
    var = _dot(dlt * dlt, seg) * inv
    yn = dlt * lax.rsqrt(var + GN_EPS) * gnw_ref[...] + gnb_ref[...]
    ya = (yn + bonus_ref[...]) * gate_ref[...]
    cat = jnp.concatenate([ya.astype(BF16), yb_ref[...].astype(BF16)], axis=1)
    o_ref[...] = x_ref[...] + m_ref[2:3, :] * _dot(cat, wout_ref[...])


def _mix_out(y0, y1, bonus, gate, yb, x, mod, mod_base, mod_stride, gnw, gnb, seg, wout, t, tm):
    n, d = x.shape
    wa = d // 2
    tpb = t // tm
    half = pl.BlockSpec((tm, wa), lambda i: (i, 0))
    full = pl.BlockSpec((tm, d), lambda i: (i, 0))
    const = lambda a: pl.BlockSpec(a.shape, lambda i: (0, 0))
    return pl.pallas_call(
        _mix_out_kernel,
        grid=(n // tm,),
        in_specs=[half, half, half, half, half, full,
                  pl.BlockSpec((None, 6, d), lambda i: (mod_base + (i // tpb) * mod_stride, 0, 0)),
                  const(gnw), const(gnb), const(seg), const(wout)],
        out_specs=full,
        out_shape=jax.ShapeDtypeStruct((n, d), F32),
        input_output_aliases={5: 0},
        compiler_params=_cparams("parallel"),
        name="mix_out",
    )(y0, y1, bonus, gate, yb, x, mod, gnw, gnb, seg, wout)


def _chan_dft_kernel(x_ref, g_ref, m_ref, cs_ref, xc_ref, xs_ref, *, groups):
    h = _modulate(x_ref[...], g_ref[...], m_ref[0:1, :], m_ref[1:2, :])
    gc = h.shape[1] // groups
    cs = cs_ref[...]
    for gi in range(groups):
        res = _dot(h[:, gi * gc:(gi + 1) * gc], cs)
        xc_ref[:, gi * gc:(gi + 1) * gc] = res[:, :gc].astype(xc_ref.dtype)
        xs_ref[:, gi * gc:(gi + 1) * gc] = res[:, gc:].astype(xs_ref.dtype)


def _chan_dft(x, g, mod, mod_base, mod_stride, cs, t, tm, groups, out_dtype):
    n, d = x.shape
    tpb = t // tm
    full = pl.BlockSpec((tm, d), lambda i: (i, 0))
    return pl.pallas_call(
        functools.partial(_chan_dft_kernel, groups=groups),
        grid=(n // tm,),
        in_specs=[full, pl.BlockSpec((1, d), lambda i: (0, 0)),
                  pl.BlockSpec((None, 6, d), lambda i: (mod_base + (i // tpb) * mod_stride, 0, 0)),
                  pl.BlockSpec(cs.shape, lambda i: (0, 0))],
        out_specs=[full, full],
        out_shape=[jax.ShapeDtypeStruct((n, d), out_dtype)] * 2,
        compiler_params=_cparams("parallel"),
        name="chan_dft",
    )(x, g.reshape(1, d), mod, cs)


LANES = 128
SUBLANES = 8


def _dft_stage_a_kernel(m_ref, xc_ref, xs_ref, ar_ref, ai_ref):
    m = m_ref[...]
    n1 = ar_ref.shape[0]
    for v in range(SUBLANES):
        rhs = jnp.concatenate([xc_ref[:, v, :], xs_ref[:, v, :]], axis=0).astype(BF16)
        res = jnp.dot(m, rhs, preferred_element_type=F32)
        ar_ref[:, v, :] = res[:n1]
        ai_ref[:, v, :] = res[n1:]


def _dft_stage_a(m, xc, xs, batch, t):
    n, d = xc.shape
    n1 = t // LANES
    nu = LANES // SUBLANES
    view = (batch, n1, nu, SUBLANES, d)
    blk = pl.BlockSpec((None, n1, None, SUBLANES, d), lambda b, u: (b, 0, u, 0, 0))
    return pl.pallas_call(
        _dft_stage_a_kernel,
        grid=(batch, nu),
        in_specs=[pl.BlockSpec(m.shape, lambda b, u: (0, 0)), blk, blk],
        out_specs=[blk, blk],
        out_shape=[jax.ShapeDtypeStruct(view, F32)] * 2,
        compiler_params=_cparams("parallel", "parallel"),
        name="dft_stage_a",
    )(m, xc.reshape(view), xs.reshape(view))


def _dft_stage_c_kernel(cs_ref, twc_ref, tws_ref, ar_ref, ai_ref, y_ref):
    cs = cs_ref[...]
    for w in range(SUBLANES):
        ar, ai = ar_ref[w], ai_ref[w]
        cw, sw = twc_ref[w], tws_ref[w]
        b = jnp.concatenate([ar * cw + ai * sw, ai * cw - ar * sw], axis=0)
        y_ref[:, w, :] = _dot(cs, b)


def _dft_stage_c(cs, twc, tws, ar, ai, batch, t):
    d = ar.shape[-1]
    n1 = t // LANES
    ng = n1 // SUBLANES
    a_view = (batch, n1, LANES, d)
    a_blk = pl.BlockSpec((None, SUBLANES, LANES, d), lambda b, g: (b, g, 0, 0))
    tw_blk = pl.BlockSpec((SUBLANES, LANES, 1), lambda b, g: (g, 0, 0))
    return pl.pallas_call(
        _dft_stage_c_kernel,
        grid=(batch, ng),
        in_specs=[pl.BlockSpec(cs.shape, lambda b, g: (0, 0)), tw_blk, tw_blk, a_blk, a_blk],
        out_specs=pl.BlockSpec((None, LANES, None, SUBLANES, d), lambda b, g: (b, 0, g, 0, 0)),
        out_shape=jax.ShapeDtypeStruct((batch, LANES, ng, SUBLANES, d), F32),
        compiler_params=_cparams("parallel", "parallel"),
        name="dft_stage_c",
    )(cs, twc, tws, ar.reshape(a_view), ai.reshape(a_view)).reshape(batch * t, d)


def _resid_matmul_kernel(y_ref, x_ref, m_ref, w_ref, o_ref):
    o_ref[...] = x_ref[...] + m_ref[2:3, :] * _dot(y_ref[...], w_ref[...])


def _resid_matmul(y, x, mod, mod_base, mod_stride, w, t, tm):
    n, d = x.shape
    tpb = t // tm
    full = pl.BlockSpec((tm, d), lambda i: (i, 0))
    return pl.pallas_call(
        _resid_matmul_kernel,
        grid=(n // tm,),
        in_specs=[full, full,
                  pl.BlockSpec((None, 6, d), lambda i: (mod_base + (i // tpb) * mod_stride, 0, 0)),
                  pl.BlockSpec(w.shape, lambda i: (0, 0))],
        out_specs=full,
        out_shape=jax.ShapeDtypeStruct((n, d), F32),
        input_output_aliases={1: 0},
        compiler_params=_cparams("parallel"),
        name="resid_matmul",
    )(y, x, mod, w)


def _two_stage_tables(t):
    n1 = t // LANES
    c1, s1 = _trig(n1, n1, n1)
    m = jnp.concatenate([jnp.concatenate([c1, -s1], axis=1), jnp.concatenate([-s1, -c1], axis=1)], axis=0)
    c2, s2 = _trig(LANES, LANES, LANES)
    twc, tws = _trig(n1, LANES, t)
    return ((m * t ** -0.5).astype(BF16), jnp.concatenate([c2, s2], axis=1).astype(BF16),
            twc[:, :, None], tws[:, :, None])


def _time_dft_kernel(ct_ref, st_ref, xc_ref, xs_ref, x_ref, m_ref, wf_ref, o_ref, acc_ref):
    ki = pl.program_id(2)

    @pl.when(ki == 0)
    def _():
        acc_ref[...] = jnp.zeros_like(acc_ref)

    acc_ref[...] += (jnp.dot(ct_ref[...], xc_ref[...], preferred_element_type=F32)
                     - jnp.dot(st_ref[...], xs_ref[...], preferred_element_type=F32))

    @pl.when(ki == pl.num_programs(2) - 1)
    def _():
        o_ref[...] = x_ref[...] + m_ref[2:3, :] * _dot(acc_ref[...], wf_ref[...])


def _time_dft(tabs, xc, xs, x, mod, mod_base, mod_stride, wf, batch, t, tf, tk):
    ct, st = tabs
    n, d = x.shape
    nf, nk = t // tf, t // tk
    return pl.pallas_call(
        _time_dft_kernel,
        grid=(batch, nf, nk),
        in_specs=[pl.BlockSpec((tf, tk), lambda b, f, k: (f, k)),
                  pl.BlockSpec((tf, tk), lambda b, f, k: (f, k)),
                  pl.BlockSpec((tk, d), lambda b, f, k: (b * nk + k, 0)),
                  pl.BlockSpec((tk, d), lambda b, f, k: (b * nk + k, 0)),
                  pl.BlockSpec((tf, d), lambda b, f, k: (b * nf + f, 0)),
                  pl.BlockSpec((None, 6, d), lambda b, f, k: (mod_base + b * mod_stride, 0, 0)),
                  pl.BlockSpec((d, d), lambda b, f, k: (0, 0))],
        out_specs=pl.BlockSpec((tf, d), lambda b, f, k: (b * nf + f, 0)),
        out_shape=jax.ShapeDtypeStruct((n, d), F32),
        scratch_shapes=[pltpu.VMEM((tf, d), F32)],
        input_output_aliases={4: 0},
        compiler_params=_cparams("parallel", "parallel", "arbitrary"),
        name="time_dft",
    )(ct, st, xc, xs, x, mod, wf)


def _trig(rows, cols, period):
    prod = jnp.bitwise_and(jnp.arange(rows, dtype=jnp.int32)[:, None] * jnp.arange(cols, dtype=jnp.int32)[None, :],
                           period - 1)
    ang = prod.astype(F32) * (2.0 * math.pi / period)
    return jnp.cos(ang), jnp.sin(ang)


def _dft_table_kernel(ca_ref, sa_ref, cb_ref, sb_ref, ct_ref, st_ref):
    ca, sa = ca_ref[...], sa_ref[...]
    cb, sb = cb_ref[...], sb_ref[...]
    for j in range(ca.shape[1]):
        a, s = ca[:, j:j + 1], sa[:, j:j + 1]
        ct_ref[:, j * LANES:(j + 1) * LANES] = (a * cb - s * sb).astype(ct_ref.dtype)
        st_ref[:, j * LANES:(j + 1) * LANES] = (s * cb + a * sb).astype(st_ref.dtype)


def _time_tables(t, scale):
    hi_n = t // LANES
    ca, sa = _trig(t, hi_n, hi_n)
    cb, sb = _trig(t, LANES, t)
    tf = min(t, 256)
    small = pl.BlockSpec((tf, hi_n), lambda f: (f, 0))
    lanes = pl.BlockSpec((tf, LANES), lambda f: (f, 0))
    wide = pl.BlockSpec((tf, t), lambda f: (f, 0))
    return pl.pallas_call(
        _dft_table_kernel,
        grid=(t // tf,),
        in_specs=[small, small, lanes, lanes],
        out_specs=[wide, wide],
        out_shape=[jax.ShapeDtypeStruct((t, t), BF16)] * 2,
        compiler_params=_cparams("parallel"),
        name="dft_table",
    )(ca * scale, sa * scale, cb, sb)


def _router_kernel(x_ref, g_ref, m_ref, wr_ref, br_ref, tri_ref, ids_out, gates_out, cnt_out, tbase_out, trun_out,
                   base_ref):
    i = pl.program_id(0)

    @pl.when(i == 0)
    def _():
        base_ref[...] = jnp.zeros_like(base_ref)

    h2 = _modulate(x_ref[...], g_ref[...], m_ref[3:4, :], m_ref[4:5, :])
    h_hi = h2.astype(BF16)
    hs = (h_hi, (h2 - h_hi.astype(F32)).astype(BF16))
    lt = jnp.zeros((wr_ref.shape[1], h2.shape[0]), F32)
    for ia, ib in ((0, 0), (0, 1), (1, 0)):
        lt = lt + lax.dot_general(wr_ref[ib], hs[ia], (((1,), (1,)), ((), ())), preferred_element_type=F32)
    logits = lt[0:ROUTE_ROWS] + br_ref[0:ROUTE_ROWS, :]

    tm = logits.shape[1]
    row_i = lax.broadcasted_iota(jnp.int32, (ROUTE_ROWS, tm), 0)
    row = row_i.astype(F32)
    neg = -jnp.inf
    far = float(ROUTE_ROWS)
    gmask = row_i < MOE_GROUPS
    lc = jnp.where(gmask, logits, neg)
    mc = jnp.max(lc, axis=0, keepdims=True)
    sc = jnp.sum(jnp.where(gmask, jnp.exp(logits - mc), 0.0), axis=0, keepdims=True)
    g_val = 1.0 / sc
    g_idx = jnp.min(jnp.where(lc == mc, row, far), axis=0, keepdims=True)
    lgroup = jnp.where(row_i >= MOE_GROUPS, jnp.right_shift(row_i - MOE_GROUPS, 3), -1).astype(F32)
    sel = lgroup == g_idx
    lf = jnp.where(sel, logits, neg)
    m1 = jnp.max(lf, axis=0, keepdims=True)
    ef = jnp.where(sel, jnp.exp(logits - m1), 0.0)
    p = ef / jnp.sum(ef, axis=0, keepdims=True)
    p1 = jnp.where(sel, p, -1.0)
    v1 = jnp.max(p1, axis=0, keepdims=True)
    i1 = jnp.min(jnp.where(p1 == v1, row, far), axis=0, keepdims=True)
    p2 = jnp.where(row == i1, -1.0, p1)
    v2 = jnp.max(p2, axis=0, keepdims=True)
    i2 = jnp.min(jnp.where(p2 == v2, row, far), axis=0, keepdims=True)
    denom = v1 + v2
    gate1 = g_val * v1 / denom
    gate2 = g_val * v2 / denom

    tri = tri_ref[...]
    oh1 = jnp.where(row == i1, 1.0, 0.0)
    oh2 = jnp.where(row == i2, 1.0, 0.0)
    tot1 = jnp.sum(oh1, axis=1, keepdims=True)
    tot2 = jnp.sum(oh2, axis=1, keepdims=True)
    run = jnp.floor((tot1 + tot2 + (SUBLANES - 1)) * (1.0 / SUBLANES))
    er = lax.broadcasted_iota(jnp.int32, (ROUTE_ROWS, ROUTE_ROWS), 0)
    ec = lax.broadcasted_iota(jnp.int32, (ROUTE_ROWS, ROUTE_ROWS), 1)
    before = jnp.where(ec < er, 1.0, 0.0).astype(BF16)
    run_b = jnp.broadcast_to(run, (ROUTE_ROWS, LANES))
    off = jnp.dot(before, run_b.astype(BF16), preferred_element_type=F32)[:, 0:1] * SUBLANES
    c1 = jnp.dot(oh1.astype(BF16), tri, preferred_element_type=F32)
    c2 = jnp.dot(oh2.astype(BF16), tri, preferred_element_type=F32)
    pos1 = jnp.sum(oh1 * (off + c1), axis=0, keepdims=True)
    pos2 = jnp.sum(oh2 * (off + tot1 + c2), axis=0, keepdims=True)
    base = base_ref[...]
    tbase_out[...] = base
    trun_out[...] = run_b * SUBLANES
    base = base + run_b * SUBLANES
    base_ref[...] = base
    cnt_out[...] = base
    zrow = jnp.zeros((4, tm), F32)
    ids_out[...] = jnp.concatenate([i1 - MOE_GROUPS, i2 - MOE_GROUPS, pos1, pos2, zrow], axis=0).astype(jnp.int32)
    gates_out[...] = jnp.concatenate([gate1, gate2, zrow, zrow[0:2]], axis=0)


def _router(x, g, mod, mod_base, mod_stride, wr3, br, tri, t, tm):
    n, d = x.shape
    tpb = t // tm
    full = pl.BlockSpec((tm, d), lambda i: (i, 0))
    lanes = pl.BlockSpec((8, tm), lambda i: (0, i))
    cnt = pl.BlockSpec((ROUTE_ROWS, LANES), lambda i: (0, 0))
    per_tile = pl.BlockSpec((None, ROUTE_ROWS, LANES), lambda i: (i, 0, 0))
    tile_sd = jax.ShapeDtypeStruct((n // tm, ROUTE_ROWS, LANES), F32)
    return pl.pallas_call(
        _router_kernel,
        grid=(n // tm,),
        in_specs=[full, pl.BlockSpec((1, d), lambda i: (0, 0)),
                  pl.BlockSpec((None, 6, d), lambda i: (mod_base + (i // tpb) * mod_stride, 0, 0)),
                  pl.BlockSpec(wr3.shape, lambda i: (0, 0, 0)),
                  pl.BlockSpec(br.shape, lambda i: (0, 0)),
                  pl.BlockSpec((tm, tm), lambda i: (0, 0))],
        out_specs=[lanes, lanes, cnt, per_tile, per_tile],
        out_shape=[jax.ShapeDtypeStruct((8, n), jnp.int32),
                   jax.ShapeDtypeStruct((8, n), F32), jax.ShapeDtypeStruct((ROUTE_ROWS, LANES), F32),
                   tile_sd, tile_sd],
        scratch_shapes=[pltpu.VMEM((ROUTE_ROWS, LANES), F32)],
        compiler_params=_cparams("arbitrary"),
        name="moe_router",
    )(x, g.reshape(1, d), mod, wr3, br, tri)


RUN_SLACK = 256
RUN_COMMON = 64


def _run_copies(len_ref, pos_ref, dst_ref, tile, n_exp, max_len, make_copy, wait):
    sizes = []
    size = max_len
    while size >= SUBLANES:
        sizes.append(size)
        size //= 2

    def pieces(n, pos, row, some_sizes):
        for size in some_sizes:
            done = jnp.bitwise_and(n, -2 * size)

            @pl.when(jnp.bitwise_and(n, size) != 0)
            def _():
                cp = make_copy(pl.multiple_of(pos + done, SUBLANES), pl.multiple_of(row + done, SUBLANES), size)
                if wait:
                    cp.wait()
                else:
                    cp.start()

    small = [s for s in sizes if s < RUN_COMMON]

    def body(e, carry):
        idx = tile * n_exp + e
        n, pos, row = len_ref[idx], pos_ref[idx], dst_ref[idx]

        @pl.when(n >= RUN_COMMON)
        def _():
            pieces(n, pos, row, [s for s in sizes if s >= RUN_COMMON])

        pieces(n, pos, row, small)
        return carry

    lax.fori_loop(0, n_exp, body, 0)


def _dispatch_kernel(len_ref, pos_ref, dst_ref, zb_ref, x_ref, g_ref, m_ref, q_ref, xbuf_ref,
                     s_ref, zero_ref, sem, zsem, *, tm, n_exp):
    step = pl.program_id(0)

    def zero_copy(s):
        start = pl.multiple_of(zb_ref[s] * MOE_BLOCK, MOE_BLOCK)
        return pltpu.make_async_copy(zero_ref, xbuf_ref.at[pl.ds(start, MOE_BLOCK)], zsem)

    @pl.when(step == 0)
    def _():
        zero_ref[...] = jnp.zeros_like(zero_ref)
        for s in range(zb_ref.shape[0]):
            @pl.when(zb_ref[s] >= 0)
            def _():
                zero_copy(s).start()
        for s in range(zb_ref.shape[0]):
            @pl.when(zb_ref[s] >= 0)
            def _():
                zero_copy(s).wait()

    h2 = _modulate(x_ref[...], g_ref[...], m_ref[3:4, :], m_ref[4:5, :]).astype(BF16)
    srows = s_ref.shape[1]
    pos = lax.broadcasted_iota(jnp.int32, (srows, tm), 0)
    perm = jnp.where((pos == q_ref[2:3, :]) | (pos == q_ref[3:4, :]), 1.0, 0.0).astype(BF16)
    slot = lax.rem(step, 2)
    s_ref[slot] = jnp.dot(perm, h2, preferred_element_type=F32)

    def run_copy(buf_slot):
        def make(tile_pos, buf_row, size):
            return pltpu.make_async_copy(s_ref.at[buf_slot, pl.ds(tile_pos, size)],
                                         xbuf_ref.at[pl.ds(buf_row, size)], sem.at[buf_slot])
        return make

    _run_copies(len_ref, pos_ref, dst_ref, step, n_exp, 2 * tm, run_copy(slot), False)

    @pl.when(step > 0)
    def _():
        _run_copies(len_ref, pos_ref, dst_ref, step - 1, n_exp, 2 * tm, run_copy(1 - slot), True)

    @pl.when(step == pl.num_programs(0) - 1)
    def _():
        _run_copies(len_ref, pos_ref, dst_ref, step, n_exp, 2 * tm, run_copy(slot), True)


def _dispatch(runs, zblocks, x, g, mod, mod_base, mod_stride, ids, rows, n_exp, t, tm):
    n, d = x.shape
    tpb = t // tm
    cmap = lambda i, *_: (0, 0)
    return pl.pallas_call(
        functools.partial(_dispatch_kernel, tm=tm, n_exp=n_exp),
        grid_spec=pltpu.PrefetchScalarGridSpec(
            num_scalar_prefetch=4,
            grid=(n // tm,),
            in_specs=[pl.BlockSpec((tm, d), lambda i, *_: (i, 0)),
                      pl.BlockSpec((1, d), cmap),
                      pl.BlockSpec((None, 6, d), lambda i, *_: (mod_base + (i // tpb) * mod_stride, 0, 0)),
                      pl.BlockSpec((8, tm), lambda i, *_: (0, i))],
            out_specs=pl.BlockSpec(memory_space=pl.ANY),
            scratch_shapes=[pltpu.VMEM((2, 2 * tm + RUN_SLACK, d), F32), pltpu.VMEM((MOE_BLOCK, d), F32),
                            pltpu.SemaphoreType.DMA((2,)), pltpu.SemaphoreType.DMA],
        ),
        out_shape=jax.ShapeDtypeStruct((rows, d), F32),
        compiler_params=_cparams("arbitrary"),
        name="moe_dispatch",
    )(*runs, zblocks, x, g.reshape(1, d), mod, ids)


def _expert_kernel(be_ref, na_ref, x_ref, w1_ref, w3_ref, w2_ref, o_ref):
    active = pl.program_id(0) < na_ref[0]

    @pl.when(active)
    def _():
        xb = x_ref[...].astype(BF16)
        h1 = jnp.dot(xb, w1_ref[...].astype(BF16), preferred_element_type=F32)
        h3 = jnp.dot(xb, w3_ref[...].astype(BF16), preferred_element_type=F32)
        act = h1 * _sigmoid(h1) * h3
        o_ref[...] = _dot(act, w2_ref[...])

    @pl.when(jnp.logical_not(active))
    def _():
        o_ref[...] = jnp.zeros_like(o_ref)


def _experts(block_e, nact, xbuf, w1, w3, w2, layer):
    rows, d = xbuf.shape
    nb = rows // MOE_BLOCK
    de = w1.shape[3]
    blk = lambda i, be, na: (jnp.minimum(i, na[0] - 1), 0)
    wmap = lambda i, be, na: (layer, be[jnp.minimum(i, na[0] - 1)], 0, 0)
    return pl.pallas_call(
        _expert_kernel,
        grid_spec=pltpu.PrefetchScalarGridSpec(
            num_scalar_prefetch=2,
            grid=(nb,),
            in_specs=[pl.BlockSpec((MOE_BLOCK, d), blk),
                      pl.BlockSpec((None, None, d, de), wmap),
                      pl.BlockSpec((None, None, d, de), wmap),
                      pl.BlockSpec((None, None, de, d), wmap)],
            out_specs=pl.BlockSpec((MOE_BLOCK, d), lambda i, be, na: (i, 0)),
        ),
        out_shape=jax.ShapeDtypeStruct((rows, d), F32),
        compiler_params=_cparams("arbitrary"),
        name="moe_experts",
    )(block_e, nact, xbuf, w1, w3, w2)


def _combine_kernel(len_ref, pos_ref, dst_ref, y_ref, q_ref, gates_ref, x_ref, m_ref, fn_ref, o_ref, w_ref, sem,
                    *, tm, n_exp, final):
    step = pl.program_id(0)

    slot = lax.rem(step, 2)

    def run_copy(buf_slot):
        def make(tile_pos, buf_row, size):
            return pltpu.make_async_copy(y_ref.at[pl.ds(buf_row, size)],
                                         w_ref.at[buf_slot, pl.ds(tile_pos, size)], sem.at[buf_slot])
        return make

    @pl.when(step == 0)
    def _():
        w_ref[...] = jnp.zeros_like(w_ref)
        _run_copies(len_ref, pos_ref, dst_ref, step, n_exp, 2 * tm, run_copy(slot), False)

    @pl.when(step + 1 < pl.num_programs(0))
    def _():
        _run_copies(len_ref, pos_ref, dst_ref, step + 1, n_exp, 2 * tm, run_copy(1 - slot), False)

    _run_copies(len_ref, pos_ref, dst_ref, step, n_exp, 2 * tm, run_copy(slot), True)

    wrows = w_ref.shape[1]
    pos = lax.broadcasted_iota(jnp.int32, (tm, wrows), 1)
    q = q_ref[...]
    sel = jnp.concatenate([jnp.where(pos == q[:, 0:1], 1.0, 0.0), jnp.where(pos == q[:, 1:2], 1.0, 0.0)],
                          axis=0).astype(BF16)
    picked = jnp.dot(sel, w_ref[slot].astype(BF16), preferred_element_type=F32)
    gates = gates_ref[...]
    y = gates[:, 0:1] * picked[:tm] + gates[:, 1:2] * picked[tm:]
    out = x_ref[...] + m_ref[5:6, :] * y
    if final:
        ms = jnp.mean(out * out, axis=-1, keepdims=True)
        out = out * lax.rsqrt(ms + NORM_EPS) * fn_ref[...]
    o_ref[...] = out


def _combine(runs, ybuf, qcols, gates, x, mod, mod_base, mod_stride, fnorm, n_exp, t, tm, final):
    n, d = x.shape
    tpb = t // tm
    full = pl.BlockSpec((tm, d), lambda i, *_: (i, 0))
    pair = pl.BlockSpec((tm, 2), lambda i, *_: (i, 0))
    return pl.pallas_call(
        functools.partial(_combine_kernel, tm=tm, n_exp=n_exp, final=final),
        grid_spec=pltpu.PrefetchScalarGridSpec(
            num_scalar_prefetch=3,
            grid=(n // tm,),
            in_specs=[pl.BlockSpec(memory_space=pl.ANY), pair, pair, full,
                      pl.BlockSpec((None, 6, d), lambda i, *_: (mod_base + (i // tpb) * mod_stride, 0, 0)),
                      pl.BlockSpec((1, d), lambda i, *_: (0, 0))],
            out_specs=full,
            scratch_shapes=[pltpu.VMEM((2, 2 * tm + RUN_SLACK, d), F32), pltpu.SemaphoreType.DMA((2,))],
        ),
        out_shape=jax.ShapeDtypeStruct((n, d), F32),
        input_output_aliases={6: 0},
        compiler_params=_cparams("arbitrary"),
        name="moe_combine",
    )(*runs, ybuf, qcols, gates, x, mod, fnorm)


def _moe(x, g, mod, mod_base, mod_stride, rt, w1, w3, w2, layer, fnorm, t, tm, final):
    n, d = x.shape
    n_exp = w1.shape[1]
    wr3, br, tri = rt
    assert n_exp * (SUBLANES - 1) <= RUN_SLACK
    ids, gates, counts, tbase, trun = _router(x, g, mod, mod_base, mod_stride, wr3, br, tri, t, tm)
    experts = slice(MOE_GROUPS, MOE_GROUPS + n_exp)
    counts = counts[experts, 0].astype(jnp.int32)
    padded = (counts + MOE_BLOCK - 1) // MOE_BLOCK * MOE_BLOCK
    pend = jnp.cumsum(padded)
    pstart = pend - padded
    run_len = trun[:, experts, 0].astype(jnp.int32)
    run_pos = jnp.cumsum(run_len, axis=1) - run_len
    run_dst = pstart[None, :] + tbase[:, experts, 0].astype(jnp.int32)
    runs = (run_len.reshape(-1), run_pos.reshape(-1), run_dst.reshape(-1))
    ntiles = n // tm
    nb = -(-(2 * n + ntiles * n_exp * (SUBLANES - 1)) // MOE_BLOCK) + n_exp
    blk_start = jnp.arange(nb, dtype=jnp.int32) * MOE_BLOCK
    block_e = jnp.minimum(jnp.sum((pend[None, :] <= blk_start[:, None]).astype(jnp.int32), axis=1), n_exp - 1)
    nact = pend[-1:] // MOE_BLOCK
    partial = jnp.where(padded > counts, pend // MOE_BLOCK - 1, -1)
    tail = nact + jnp.arange(nb - (2 * n) // MOE_BLOCK, dtype=jnp.int32)
    zblocks = jnp.concatenate([partial, jnp.where(tail < nb, tail, -1)]).astype(jnp.int32)
    xbuf = _dispatch(runs, zblocks, x, g, mod, mod_base, mod_stride, ids, nb * MOE_BLOCK, n_exp, t, tm)
    ybuf = _experts(block_e, nact, xbuf, w1, w3, w2, layer)
    return _combine(runs, ybuf, ids[2:4].T, gates[0:2].T, x, mod, mod_base, mod_stride, fnorm, n_exp, t, tm, final)


def _seg_ones(width):
    idx = np.arange(width) // HEAD_DIM
    return jnp.asarray((idx[:, None] == idx[None, :]).astype(np.float32), dtype=BF16)


def _even_weights(j, mu_x, mu_p, decay_w0, decay_w1, decay_w2, lr_a0, lr_a1, lr_a2, gate_g1, gate_g2,
                  k_k, k_a, r_k, pool_w, pool_scale, seg):
    wa = mu_p.shape[-1]

    def cat1(w):
        return jnp.concatenate([w[0], w[1]], axis=1).astype(BF16)

    def pad2(w):
        z = jnp.zeros_like(w[0])
        return jnp.stack([jnp.concatenate([w[0], z], axis=0), jnp.concatenate([z, w[1]], axis=0)]).astype(BF16)

    return (mu_x[j], mu_p[j], decay_w0[j], cat1(decay_w1[j]), pad2(decay_w2[j]),
            lr_a0[j], cat1(lr_a1[j]), pad2(lr_a2[j]),
            gate_g1[j].astype(BF16), gate_g2[j].astype(BF16),
            k_k[j].reshape(1, wa), k_a[j].reshape(1, wa), r_k[j].reshape(1, wa),
            pool_w[j].astype(BF16), pool_scale[j].reshape(1, wa), seg)


def kernel(x, c, ctx, c_ctx, ada_w, ada_b, norm_mix, norm_ffn, w_in, mu_x, mu_p, decay_w0, decay_w1, decay_w2, lr_a0, lr_a1, lr_a2, gate_g1, gate_g2, k_k, k_a, r_k, gn_w, gn_b, pool_w, pool_scale, w_out, w_fourier, router_c, router_c_b, router_f, router_f_b, moe_w1, moe_w3, moe_w2, final_norm):
    batch, t, d = x.shape
    tc = ctx.shape[1]
    depth = ada_w.shape[0]
    wa = d // 2
    n, ncx = batch * t, batch * tc
    assert batch <= 4 and t % 512 == 0 and tc % 256 == 0 and d % 512 == 0

    cond8 = jnp.zeros((8, d), F32).at[:batch].set(c).at[4].set(c_ctx)
    mod_all = _ada(cond8, ada_w, ada_b).reshape(depth, 8, 6, d)
    seg = _seg_ones(wa)
    fnorm = final_norm.reshape(1, d)

    fgroups = 4
    gc = d // fgroups
    cc, sc = _trig(gc, gc, gc)
    cs = (jnp.concatenate([cc, sc], axis=1) * gc ** -0.5).astype(BF16)
    stage_m, stage_cs, stage_twc, stage_tws = _two_stage_tables(t)
    tab_ctx = _time_tables(tc, tc ** -0.5)

    n_exp = moe_w1.shape[1]
    tri = {tm: jnp.asarray(np.triu(np.ones((tm, tm), np.float32), 1), dtype=BF16) for tm in (512, 256)}

    lat = x.reshape(n, d)
    cx = ctx.reshape(ncx, d)
    last_read = 2 * ((depth - 1) // 2)
    npair = wa // PAIR

    for i in range(depth):
        ctx_in = i <= last_read
        ctx_out = i < last_read
        mod = mod_all[i]
        j = i // 2
        if i % 2 == 0:
            wts = _even_weights(j, mu_x, mu_p, decay_w0, decay_w1, decay_w2, lr_a0, lr_a1, lr_a2,
                                gate_g1, gate_g2, k_k, k_a, r_k, pool_w, pool_scale, seg)
            w_in_b = w_in[j].astype(BF16)
            w_out_b = w_out[j].astype(BF16)
            gnw, gnb = gn_w[j].reshape(1, wa), gn_b[j].reshape(1, wa)
            pc = _proj(cx, norm_mix[i], mod, w_in_b, tc, 4, 0, 0, 256, BF16)
            fc = _rwkv_prep(cx, pc, norm_mix[i], mod, 4, 0, wts, batch, tc, False)
            ryc, pqc = _wkv_chunks(*fc[:6], batch, tc)
            g0 = jnp.zeros((2, batch, npair, PAIR, PAIR), F32)
            yc0, yc1, gctx = _wkv_scan(ryc, pqc, g0)
            pl_ = _proj(lat, norm_mix[i], mod, w_in_b, t, 0, 1, 0, 512, BF16)
            fl = _rwkv_prep(lat, pl_, norm_mix[i], mod, 0, 1, wts, batch, t, True)
            ryl, pql = _wkv_chunks(*fl[:6], batch, t)
            yl0, yl1, _ = _wkv_scan(ryl, pql, gctx)
            lat = _mix_out(yl0.reshape(n, wa), yl1.reshape(n, wa), fl[6], fl[7], fl[8], lat, mod, 0, 1,
                           gnw, gnb, seg, w_out_b, t, 512)
            if ctx_out:
                cx = _mix_out(yc0.reshape(ncx, wa), yc1.reshape(ncx, wa), fc[6], fc[7], fc[8], cx, mod, 4, 0,
                              gnw, gnb, seg, w_out_b, tc, 256)
        else:
            wf_b = w_fourier[j].astype(BF16)
            xc, xs = _chan_dft(lat, norm_mix[i], mod, 0, 1, cs, t, 512, fgroups, F32)
            ar, ai = _dft_stage_a(stage_m, xc, xs, batch, t)
            yf = _dft_stage_c(stage_cs, stage_twc, stage_tws, ar, ai, batch, t)
            lat = _resid_matmul(yf, lat, mod, 0, 1, wf_b, t, 512)
            if ctx_out:
                xcc, xsc = _chan_dft(cx, norm_mix[i], mod, 4, 0, cs, tc, 256, fgroups, BF16)
                cx = _time_dft(tab_ctx, xcc, xsc, cx, mod, 4, 0, wf_b, batch, tc, tc, tc)
        wr = jnp.zeros((LANES, d), F32).at[:MOE_GROUPS].set(router_c[i].T).at[MOE_GROUPS:MOE_GROUPS + n_exp].set(router_f[i].T)
        br = jnp.zeros((LANES, 1), F32).at[:MOE_GROUPS, 0].set(router_c_b[i]).at[MOE_GROUPS:MOE_GROUPS + n_exp, 0].set(router_f_b[i])
        wr3 = jnp.stack(_split3(wr))
        final = i == depth - 1
        lat = _moe(lat, norm_ffn[i], mod, 0, 1, (wr3, br, tri[512]), moe_w1, moe_w3, moe_w2, i, fnorm,
                   t, 512, final)
        if ctx_out:
            cx = _moe(cx, norm_ffn[i], mod, 4, 0, (wr3, br, tri[256]), moe_w1, moe_w3, moe_w2, i, fnorm,
                      tc, 256, False)
    return lat.reshape(batch, t, d)
```

```python
import functools
import math

import jax
import jax.numpy as jnp
import numpy as np
from jax import lax
from jax.experimental import pallas as pl
from jax.experimental.pallas import tpu as pltpu

F32 = jnp.float32
BF16 = jnp.bfloat16

GRID_W = 64
HEAD_DIM = 64
CHUNK = 64
PAIR = 2 * HEAD_DIM
NORM_EPS = 1e-6
GN_EPS = 64e-5
POOL_WINDOWS = (2, 4, 8, 16)
MOE_GROUPS = 4
ROUTE_ROWS = 40
EXPERTS_PER_GROUP = 8
MOE_BLOCK = 512
VMEM_LIMIT = 56 * 1024 * 1024


def _cparams(*sem):
    return pltpu.CompilerParams(dimension_semantics=tuple(sem), vmem_limit_bytes=VMEM_LIMIT)


def _dot(a, b):
    return jnp.dot(a.astype(BF16), b.astype(BF16), preferred_element_type=F32)


def _dot_nt(a, b):
    return lax.dot_general(a.astype(BF16), b.astype(BF16), (((1,), (1,)), ((), ())),
                           preferred_element_type=F32)


def _dot_tn(a, b):
    return lax.dot_general(a.astype(BF16), b.astype(BF16), (((0,), (0,)), ((), ())),
                           preferred_element_type=F32)


def _split3(x):
    hi = x.astype(BF16)
    r1 = x - hi.astype(F32)
    mid = r1.astype(BF16)
    lo = (r1 - mid.astype(F32)).astype(BF16)
    return hi, mid, lo


def _sigmoid(x):
    return 1.0 / (1.0 + jnp.exp(-x))


def _modulate(x, g, shift, scale):
    ms = jnp.mean(x * x, axis=-1, keepdims=True)
    return x * lax.rsqrt(ms + NORM_EPS) * g * (1.0 + scale) + shift


def _ada_kernel(c_ref, w_ref, b_ref, o_ref):
    c = c_ref[...]
    s = c * _sigmoid(c)
    o_ref[...] = _dot(s, w_ref[...]) + b_ref[...]


def _ada(cond8, ada_w, ada_b):
    depth, d, n6 = ada_w.shape
    tn = 1536
    return pl.pallas_call(
        _ada_kernel,
        grid=(depth, n6 // tn),
        in_specs=[pl.BlockSpec((8, d), lambda l, j: (0, 0)),
                  pl.BlockSpec((None, d, tn), lambda l, j: (l, 0, j)),
                  pl.BlockSpec((None, 1, tn), lambda l, j: (l, 0, j))],
        out_specs=pl.BlockSpec((None, 8, tn), lambda l, j: (l, 0, j)),
        out_shape=jax.ShapeDtypeStruct((depth, 8, n6), F32),
        compiler_params=_cparams("parallel", "parallel"),
        name="ada_mod",
    )(cond8, ada_w, ada_b.reshape(depth, 1, n6))


def _proj_kernel(x_ref, g_ref, m_ref, w_ref, o_ref, *, shift_row, scale_row):
    h = _modulate(x_ref[...], g_ref[...], m_ref[shift_row:shift_row + 1, :], m_ref[scale_row:scale_row + 1, :])
    o_ref[...] = _dot(h, w_ref[...]).astype(o_ref.dtype)


def _proj(x, g, mod, w, rows_per_batch, mod_base, mod_stride, shift_row, tm, out_dtype=F32):
    n, d = x.shape
    nout = w.shape[1]
    tpb = rows_per_batch // tm
    return pl.pallas_call(
        functools.partial(_proj_kernel, shift_row=shift_row, scale_row=shift_row + 1),
        grid=(n // tm,),
        in_specs=[pl.BlockSpec((tm, d), lambda i: (i, 0)),
                  pl.BlockSpec((1, d), lambda i: (0, 0)),
                  pl.BlockSpec((None, 6, d), lambda i: (mod_base + (i // tpb) * mod_stride, 0, 0)),
                  pl.BlockSpec((d, nout), lambda i: (0, 0))],
        out_specs=pl.BlockSpec((tm, nout), lambda i: (i, 0)),
        out_shape=jax.ShapeDtypeStruct((n, nout), out_dtype),
        compiler_params=_cparams("parallel"),
        name="mod_proj",
    )(x, g.reshape(1, d), mod, w)


def _wkv_chunk_kernel(r_ref, v_ref, an_ref, lw_ref, kd_ref, bb_ref, ry_ref, pq_ref):
    d = pl.program_id(0)
    width = lw_ref.shape[1]
    nsub = lw_ref.shape[0] // CHUNK
    row = lax.broadcasted_iota(jnp.int32, (CHUNK, CHUNK), 0)
    col = lax.broadcasted_iota(jnp.int32, (CHUNK, CHUNK), 1)
    sgn = 1 - 2 * d
    tri = jnp.where((row - col) * sgn >= 0, 1.0, 0.0).astype(BF16)

    def chunk_feats(sc):
        rows = slice(sc * CHUNK, (sc + 1) * CHUNK)
        lw = lw_ref[rows, :]
        hi, mid, lo = _split3(lw)
        csum = (jnp.dot(tri, hi, preferred_element_type=F32)
                + jnp.dot(tri, mid, preferred_element_type=F32)
                + jnp.dot(tri, lo, preferred_element_type=F32))
        e_neg = jnp.exp(-csum)
        return (an_ref[rows, :] * jnp.exp(csum - lw), bb_ref[rows, :] * e_neg, kd_ref[rows, :] * e_neg,
                r_ref[rows, :] * jnp.exp(csum), v_ref[rows, :], jnp.exp(jnp.sum(lw, axis=0, keepdims=True)))

    feats = [chunk_feats(sc) for sc in range(nsub)]

    lane = lax.broadcasted_iota(jnp.int32, (1, PAIR), 1)
    head0 = lane < HEAD_DIM
    i2 = lax.broadcasted_iota(jnp.int32, (PAIR, PAIR), 0)
    j2 = lax.broadcasted_iota(jnp.int32, (PAIR, PAIR), 1)
    ti = jnp.bitwise_and(i2, CHUNK - 1)
    tj = jnp.bitwise_and(j2, CHUNK - 1)
    strict = (ti - tj) * sgn > 0
    incl = (ti - tj) * sgn >= 0
    eye = (i2 == j2).astype(F32)

    def stack(x):
        return jnp.concatenate([jnp.where(head0, x, 0.0), jnp.where(head0, 0.0, x)], axis=0)

    units = [(sc, p) for sc in range(nsub) for p in range(width // PAIR)]
    us = range(len(units))
    sls = [slice(p * PAIR, (p + 1) * PAIR) for _, p in units]
    a_s, b_s, k_s, r_s, v_s = ([stack(feats[sc][f][:, sls[u]]) for u, (sc, _) in enumerate(units)]
                               for f in range(5))
    m1 = [_dot_nt(jnp.concatenate([a_s[u], r_s[u]], axis=0), jnp.concatenate([b_s[u], k_s[u]], axis=0))
          for u in us]
    npow = [jnp.where(strict, m[:PAIR, :PAIR], 0.0) for m in m1]
    a_ak = [jnp.where(strict, m[:PAIR, PAIR:], 0.0) for m in m1]
    a_rb = [jnp.where(incl, m[PAIR:, :PAIR], 0.0) for m in m1]
    a_rk = [jnp.where(incl, m[PAIR:, PAIR:], 0.0) for m in m1]
    x = [jnp.concatenate([a_s[u], _dot(a_ak[u], v_s[u])], axis=1) for u in us]
    steps = int(math.log2(CHUNK))
    for it in range(steps):
        x = [x[u] + _dot(npow[u], x[u]) for u in us]
        if it + 1 < steps:
            npow = [_dot(npow[u], npow[u]) for u in us]
    for u, (sc, p) in enumerate(units):
        ry = jnp.concatenate([r_s[u], _dot(a_rk[u], v_s[u])], axis=1) + _dot(a_rb[u], x[u])
        ry_ref[sc, p] = ry.astype(ry_ref.dtype)
    for u, (sc, p) in enumerate(units):
        ge = feats[sc][5][:, sls[u]]
        pt = (eye + _dot_tn(x[u][:, :PAIR], b_s[u])) * ge
        qt = (_dot_tn(x[u][:, PAIR:], b_s[u]) + _dot_tn(v_s[u], k_s[u])) * ge
        pq_ref[sc, p] = jnp.concatenate([pt, qt], axis=0).astype(pq_ref.dtype)


def _wkv_chunks(r, v, an, lw, kd, bb, batch, t):
    n, w = r.shape
    nc = t // CHUNK
    npair = w // PAIR
    nsub = min(4, nc)
    ns = nc // nsub
    shared = pl.BlockSpec((nsub * CHUNK, w), lambda d, b, c: (b * ns + c, 0))
    perdir = pl.BlockSpec((None, nsub * CHUNK, w), lambda d, b, c: (d, b * ns + c, 0))
    return pl.pallas_call(
        _wkv_chunk_kernel,
        grid=(2, batch, ns),
        in_specs=[shared, shared, shared, perdir, perdir, perdir],
        out_specs=[pl.BlockSpec((None, None, nsub, npair, PAIR, 2 * PAIR), lambda d, b, c: (d, b, c, 0, 0, 0)),
                   pl.BlockSpec((None, None, nsub, npair, 2 * PAIR, PAIR), lambda d, b, c: (d, b, c, 0, 0, 0))],
        out_shape=[jax.ShapeDtypeStruct((2, batch, nc, npair, PAIR, 2 * PAIR), BF16),
                   jax.ShapeDtypeStruct((2, batch, nc, npair, 2 * PAIR, PAIR), BF16)],
        compiler_params=_cparams("parallel", "parallel", "parallel"),
        name="wkv_chunks",
    )(r, v, an, lw, kd, bb)


def _wkv_scan_kernel(ryf_ref, ryb_ref, pqf_ref, pqb_ref, g0_ref, yf_ref, yb_ref, gout_ref, g_ref):
    pos = pl.program_id(0)

    @pl.when(pos == 0)
    def _():
        g_ref[...] = g0_ref[...]

    batch, npair = g_ref.shape[1], g_ref.shape[2]
    for d, (ry_ref, pq_ref, y_ref) in enumerate(((ryf_ref, pqf_ref, yf_ref), (ryb_ref, pqb_ref, yb_ref))):
        for b in range(batch):
            for p in range(npair):
                g = g_ref[d, b, p]
                ry = ry_ref[b, p]
                pq = pq_ref[b, p]
                ys = _dot_nt(ry[:, :PAIR], g) + ry[:, PAIR:].astype(F32)
                y_ref[b, :, p * PAIR:(p + 1) * PAIR] = (ys[:CHUNK] + ys[CHUNK:]).astype(y_ref.dtype)
                g_ref[d, b, p] = _dot(g, pq[:PAIR]) + pq[PAIR:].astype(F32)

    @pl.when(pos == pl.num_programs(0) - 1)
    def _():
        gout_ref[...] = g_ref[...]


def _wkv_scan(ry, pq, g0):
    _, batch, nc, npair, _, _ = ry.shape
    w = npair * PAIR
    t = nc * CHUNK
    ry_f = pl.BlockSpec((None, batch, None, npair, PAIR, 2 * PAIR), lambda s: (0, 0, s, 0, 0, 0))
    ry_b = pl.BlockSpec((None, batch, None, npair, PAIR, 2 * PAIR), lambda s: (1, 0, nc - 1 - s, 0, 0, 0))
    pq_f = pl.BlockSpec((None, batch, None, npair, 2 * PAIR, PAIR), lambda s: (0, 0, s, 0, 0, 0))
    pq_b = pl.BlockSpec((None, batch, None, npair, 2 * PAIR, PAIR), lambda s: (1, 0, nc - 1 - s, 0, 0, 0))
    gspec = pl.BlockSpec((2, batch, npair, PAIR, PAIR), lambda s: (0, 0, 0, 0, 0))
    return pl.pallas_call(
        _wkv_scan_kernel,
        grid=(nc,),
        in_specs=[ry_f, ry_b, pq_f, pq_b, gspec],
        out_specs=[pl.BlockSpec((batch, CHUNK, w), lambda s: (0, s, 0)),
                   pl.BlockSpec((batch, CHUNK, w), lambda s: (0, nc - 1 - s, 0)),
                   gspec],
        out_shape=[jax.ShapeDtypeStruct((batch, t, w), BF16),
                   jax.ShapeDtypeStruct((batch, t, w), BF16),
                   jax.ShapeDtypeStruct((2, batch, npair, PAIR, PAIR), F32)],
        scratch_shapes=[pltpu.VMEM((2, batch, npair, PAIR, PAIR), F32)],
        compiler_params=_cparams("arbitrary"),
        name="wkv_scan",
    )(ry, ry, pq, pq, g0)


GRID_SHIFT = ((-1, "first_col"), (1, "last_col"), (-GRID_W, None), (GRID_W, None))
SEQ_SHIFT = ((-1, None), (1, None))


def _prep_kernel(xp_ref, xm_ref, xn_ref, pp_ref, pm_ref, pn_ref, g_ref, m_ref,
                 mux_ref, mup_ref, dw0_ref, dw1_ref, dw2_ref, la0_ref, la1_ref, la2_ref,
                 gg1_ref, gg2_ref, kk_ref, ka_ref, rk_ref, pw_ref, ps_ref, seg_ref,
                 r_out, v_out, an_out, lw_out, kd_out, bb_out, bonus_out, gate_out, yb_out,
                 hext_ref, pext_ref, *, parts, tm, pad, tiles_per_batch, seq_len):
    i = pl.program_id(0)
    tile = lax.rem(i, tiles_per_batch)
    keep_prev = jnp.where(tile == 0, 0.0, 1.0)
    keep_next = jnp.where(tile == tiles_per_batch - 1, 0.0, 1.0)
    g = g_ref[...]
    shift, scale = m_ref[0:1, :], m_ref[1:2, :]
    d_model = xm_ref.shape[1]
    wa = r_out.shape[1]

    hext_ref[0:pad, :] = _modulate(xp_ref[...], g, shift, scale) * keep_prev
    hext_ref[pad:pad + tm, :] = _modulate(xm_ref[...], g, shift, scale)
    hext_ref[pad + tm:, :] = _modulate(xn_ref[...], g, shift, scale) * keep_next
    pext_ref[0:pad, :] = pp_ref[...].astype(F32) * keep_prev
    pext_ref[pad:pad + tm, :] = pm_ref[...].astype(F32)
    pext_ref[pad + tm:, :] = pn_ref[...].astype(F32) * keep_next

    colidx = jnp.bitwise_and(lax.broadcasted_iota(jnp.int32, (tm, 1), 0), GRID_W - 1)

    def shifted(ext_ref, col0, width):
        pw = width // len(parts)
        outs = []
        for q, (off, mask) in enumerate(parts):
            blk = ext_ref[pad + off:pad + off + tm, col0 + q * pw:col0 + (q + 1) * pw]
            if mask == "first_col":
                blk = jnp.where(colidx == 0, 0.0, blk)
            elif mask == "last_col":
                blk = jnp.where(colidx == GRID_W - 1, 0.0, blk)
            outs.append(blk)
        return jnp.concatenate(outs, axis=1)

    h = hext_ref[pad:pad + tm, :]
    hx = shifted(hext_ref, 0, d_model) - h
    x_w = h + hx * mux_ref[0:1, :]
    x_a = h + hx * mux_ref[1:2, :]
    x_g = h + hx * mux_ref[2:3, :]
    zw_mid = jnp.tanh(_dot(x_w, dw1_ref[...]))
    xa_mid = _dot(x_a, la1_ref[...])
    gate_out[...] = _dot(_sigmoid(_dot(x_g, gg1_ref[...])), gg2_ref[...]).astype(gate_out.dtype)

    def mixed(n):
        p_n = pext_ref[pad:pad + tm, n * wa:(n + 1) * wa]
        return p_n + (shifted(pext_ref, n * wa, wa) - p_n) * mup_ref[n:n + 1, :]

    r, k, v = mixed(0), mixed(1), mixed(2)
    seg = seg_ref[...]
    kk = k * kk_ref[...]
    kk = kk * lax.rsqrt(jnp.maximum(_dot(kk * kk, seg), 1e-12))
    r_out[...] = r.astype(r_out.dtype)
    v_out[...] = v.astype(v_out.dtype)
    an_out[...] = (-kk).astype(an_out.dtype)
    ka = ka_ref[...]
    kd_sum = jnp.zeros_like(k)
    for d in range(2):
        zw = dw0_ref[d:d + 1, :] + _dot(zw_mid, dw2_ref[d])
        lw_out[d] = -math.exp(-0.5) * _sigmoid(zw)
        a_lr = _sigmoid(la0_ref[d:d + 1, :] + _dot(xa_mid, la2_ref[d]))
        kd = k * (1.0 + (a_lr - 1.0) * ka)
        kd_out[d] = kd.astype(kd_out.dtype)
        bb_out[d] = (kk * a_lr).astype(bb_out.dtype)
        kd_sum = kd_sum + kd
    bonus_out[...] = (_dot(r * kd_sum * rk_ref[...], seg) * v).astype(bonus_out.dtype)

    pos = tile * tm + lax.broadcasted_iota(jnp.int32, (tm, 1), 0)
    trow = lax.broadcasted_iota(jnp.int32, (tm, tm + 2 * pad), 0)
    srow = lax.broadcasted_iota(jnp.int32, (tm, tm + 2 * pad), 1) - pad
    gp = wa // len(POOL_WINDOWS)
    ybs = []
    for gi, win in enumerate(POOL_WINDOWS):
        half = win // 2
        c0 = 3 * wa + gi * gp
        band = jnp.where((srow >= trow - half) & (srow < trow + half), 1.0, 0.0).astype(BF16)
        sums = _dot(band, pext_ref[:, c0:c0 + gp])
        cnt = (jnp.minimum(pos + half, seq_len) - jnp.maximum(pos - half, 0)).astype(F32)
        diff = sums / cnt - pext_ref[pad:pad + tm, c0:c0 + gp]
        ybs.append(_dot(diff, pw_ref[gi]))
    yb_out[...] = (jnp.concatenate(ybs, axis=1) * ps_ref[...]).astype(yb_out.dtype)


def _rwkv_prep(x, proj, g, mod, mod_base, mod_stride, wts, batch, t, grid_mode):
    n, d = x.shape
    wa = d // 2
    pcols = proj.shape[1]
    tm = 256
    pad = GRID_W if grid_mode else 16
    parts = GRID_SHIFT if grid_mode else SEQ_SHIFT
    tpb = t // tm
    hb = tm // pad
    nhb = n // pad
    main = lambda i: (i, 0)
    prev = lambda i: (jnp.maximum(i * hb - 1, 0), 0)
    nxt = lambda i: (jnp.minimum((i + 1) * hb, nhb - 1), 0)
    full2 = lambda i: (0, 0)
    full3 = lambda i: (0, 0, 0)
    in_specs = [pl.BlockSpec((pad, d), prev), pl.BlockSpec((tm, d), main), pl.BlockSpec((pad, d), nxt),
                pl.BlockSpec((pad, pcols), prev), pl.BlockSpec((tm, pcols), main), pl.BlockSpec((pad, pcols), nxt),
                pl.BlockSpec((1, d), full2),
                pl.BlockSpec((None, 6, d), lambda i: (mod_base + (i // tpb) * mod_stride, 0, 0))]
    for a in wts:
        in_specs.append(pl.BlockSpec(a.shape, full2 if a.ndim == 2 else full3))
    one = pl.BlockSpec((tm, wa), main)
    two = pl.BlockSpec((2, tm, wa), lambda i: (0, i, 0))
    sd1 = jax.ShapeDtypeStruct((n, wa), BF16)
    sd2 = jax.ShapeDtypeStruct((2, n, wa), BF16)
    lw2 = jax.ShapeDtypeStruct((2, n, wa), F32)
    return pl.pallas_call(
        functools.partial(_prep_kernel, parts=parts, tm=tm, pad=pad, tiles_per_batch=tpb, seq_len=t),
        grid=(n // tm,),
        in_specs=in_specs,
        out_specs=[one, one, one, two, two, two, one, one, one],
        out_shape=[sd1, sd1, sd1, lw2, sd2, sd2, sd1, sd1, sd1],
        scratch_shapes=[pltpu.VMEM((tm + 2 * pad, d), F32), pltpu.VMEM((tm + 2 * pad, pcols), F32)],
        compiler_params=_cparams("parallel"),
        name="rwkv_prep",
    )(x, x, x, proj, proj, proj, g.reshape(1, d), mod, *wts)


def _mix_out_kernel(y0_ref, y1_ref, bonus_ref, gate_ref, yb_ref, x_ref, m_ref, gnw_ref, gnb_ref,
                    seg_ref, wout_ref, o_ref):
    y = y0_ref[...].astype(F32) + y1_ref[...].astype(F32)
    seg = seg_ref[...]
    inv = 1.0 / HEAD_DIM
    yh = y.astype(BF16)
    mu = (jnp.dot(yh, seg, preferred_element_type=F32) + _dot(y - yh.astype(F32), seg)) * inv
    dlt = y - mu
    var = _dot(dlt * dlt, seg) * inv
    yn = dlt * lax.rsqrt(var + GN_EPS) * gnw_ref[...] + gnb_ref[...]
    ya = (yn + bonus_ref[...]) * gate_ref[...]
    cat = jnp.concatenate([ya.astype(BF16), yb_ref[...].astype(BF16)], axis=1)
    o_ref[...] = x_ref[...] + m_ref[2:3, :] * _dot(cat, wout_ref[...])


def _mix_out(y0, y1, bonus, gate, yb, x, mod, mod_base, mod_stride, gnw, gnb, seg, wout, t, tm):
    n, d = x.shape
    wa = d // 2
    tpb = t // tm
    half = pl.BlockSpec((tm, wa), lambda i: (i, 0))
    full = pl.BlockSpec((tm, d), lambda i: (i, 0))
    const = lambda a: pl.BlockSpec(a.shape, lambda i: (0, 0))
    return pl.pallas_call(
        _mix_out_kernel,
        grid=(n // tm,),
        in_specs=[half, half, half, half, half, full,
                  pl.BlockSpec((None, 6, d), lambda i: (mod_base + (i // tpb) * mod_stride, 0, 0)),
                  const(gnw), const(gnb), const(seg), const(wout)],
        out_specs=full,
        out_shape=jax.ShapeDtypeStruct((n, d), F32),
        input_output_aliases={5: 0},
        compiler_params=_cparams("parallel"),
        name="mix_out",
    )(y0, y1, bonus, gate, yb, x, mod, gnw, gnb, seg, wout)


def _chan_dft_kernel(x_ref, g_ref, m_ref, cs_ref, xc_ref, xs_ref, *, groups):
    h = _modulate(x_ref[...], g_ref[...], m_ref[0:1, :], m_ref[1:2, :])
    gc = h.shape[1] // groups
    cs = cs_ref[...]
    for gi in range(groups):
        res = _dot(h[:, gi * gc:(gi + 1) * gc], cs)
        xc_ref[:, gi * gc:(gi + 1) * gc] = res[:, :gc].astype(xc_ref.dtype)
        xs_ref[:, gi * gc:(gi + 1) * gc] = res[:, gc:].astype(xs_ref.dtype)


def _chan_dft(x, g, mod, mod_base, mod_stride, cs, t, tm, groups, out_dtype):
    n, d = x.shape
    tpb = t // tm
    full = pl.BlockSpec((tm, d), lambda i: (i, 0))
    return pl.pallas_call(
        functools.partial(_chan_dft_kernel, groups=groups),
        grid=(n // tm,),
        in_specs=[full, pl.BlockSpec((1, d), lambda i: (0, 0)),
                  pl.BlockSpec((None, 6, d), lambda i: (mod_base + (i // tpb) * mod_stride, 0, 0)),
                  pl.BlockSpec(cs.shape, lambda i: (0, 0))],
        out_specs=[full, full],
        out_shape=[jax.ShapeDtypeStruct((n, d), out_dtype)] * 2,
        compiler_params=_cparams("parallel"),
        name="chan_dft",
    )(x, g.reshape(1, d), mod, cs)


LANES = 128
SUBLANES = 8


def _dft_stage_a_kernel(m_ref, xc_ref, xs_ref, ar_ref, ai_ref):
    m = m_ref[...]
    n1 = ar_ref.shape[0]
    for v in range(SUBLANES):
        rhs = jnp.concatenate([xc_ref[:, v, :], xs_ref[:, v, :]], axis=0).astype(BF16)
        res = jnp.dot(m, rhs, preferred_element_type=F32)
        ar_ref[:, v, :] = res[:n1]
        ai_ref[:, v, :] = res[n1:]


def _dft_stage_a(m, xc, xs, batch, t):
    n, d = xc.shape
    n1 = t // LANES
    nu = LANES // SUBLANES
    view = (batch, n1, nu, SUBLANES, d)
    blk = pl.BlockSpec((None, n1, None, SUBLANES, d), lambda b, u: (b, 0, u, 0, 0))
    return pl.pallas_call(
        _dft_stage_a_kernel,
        grid=(batch, nu),
        in_specs=[pl.BlockSpec(m.shape, lambda b, u: (0, 0)), blk, blk],
        out_specs=[blk, blk],
        out_shape=[jax.ShapeDtypeStruct(view, F32)] * 2,
        compiler_params=_cparams("parallel", "parallel"),
        name="dft_stage_a",
    )(m, xc.reshape(view), xs.reshape(view))


def _dft_stage_c_kernel(cs_ref, twc_ref, tws_ref, ar_ref, ai_ref, y_ref):
    cs = cs_ref[...]
    for w in range(SUBLANES):
        ar, ai = ar_ref[w], ai_ref[w]
        cw, sw = twc_ref[w], tws_ref[w]
        b = jnp.concatenate([ar * cw + ai * sw, ai * cw - ar * sw], axis=0)
        y_ref[:, w, :] = _dot(cs, b)


def _dft_stage_c(cs, twc, tws, ar, ai, batch, t):
    d = ar.shape[-1]
    n1 = t // LANES
    ng = n1 // SUBLANES
    a_view = (batch, n1, LANES, d)
    a_blk = pl.BlockSpec((None, SUBLANES, LANES, d), lambda b, g: (b, g, 0, 0))
    tw_blk = pl.BlockSpec((SUBLANES, LANES, 1), lambda b, g: (g, 0, 0))
    return pl.pallas_call(
        _dft_stage_c_kernel,
        grid=(batch, ng),
        in_specs=[pl.BlockSpec(cs.shape, lambda b, g: (0, 0)), tw_blk, tw_blk, a_blk, a_blk],
        out_specs=pl.BlockSpec((None, LANES, None, SUBLANES, d), lambda b, g: (b, 0, g, 0, 0)),
        out_shape=jax.ShapeDtypeStruct((batch, LANES, ng, SUBLANES, d), F32),
        compiler_params=_cparams("parallel", "parallel"),
        name="dft_stage_c",
    )(cs, twc, tws, ar.reshape(a_view), ai.reshape(a_view)).reshape(batch * t, d)


def _resid_matmul_kernel(y_ref, x_ref, m_ref, w_ref, o_ref):
    o_ref[...] = x_ref[...] + m_ref[2:3, :] * _dot(y_ref[...], w_ref[...])


def _resid_matmul(y, x, mod, mod_base, mod_stride, w, t, tm):
    n, d = x.shape
    tpb = t // tm
    full = pl.BlockSpec((tm, d), lambda i: (i, 0))
    return pl.pallas_call(
        _resid_matmul_kernel,
        grid=(n // tm,),
        in_specs=[full, full,
                  pl.BlockSpec((None, 6, d), lambda i: (mod_base + (i // tpb) * mod_stride, 0, 0)),
                  pl.BlockSpec(w.shape, lambda i: (0, 0))],
        out_specs=full,
        out_shape=jax.ShapeDtypeStruct((n, d), F32),
        input_output_aliases={1: 0},
        compiler_params=_cparams("parallel"),
        name="resid_matmul",
    )(y, x, mod, w)


def _two_stage_tables(t):
    n1 = t // LANES
    c1, s1 = _trig(n1, n1, n1)
    m = jnp.concatenate([jnp.concatenate([c1, -s1], axis=1), jnp.concatenate([-s1, -c1], axis=1)], axis=0)
    c2, s2 = _trig(LANES, LANES, LANES)
    twc, tws = _trig(n1, LANES, t)
    return ((m * t ** -0.5).astype(BF16), jnp.concatenate([c2, s2], axis=1).astype(BF16),
            twc[:, :, None], tws[:, :, None])


def _time_dft_kernel(ct_ref, st_ref, xc_ref, xs_ref, x_ref, m_ref, wf_ref, o_ref, acc_ref):
    ki = pl.program_id(2)

    @pl.when(ki == 0)
    def _():
        acc_ref[...] = jnp.zeros_like(acc_ref)

    acc_ref[...] += (jnp.dot(ct_ref[...], xc_ref[...], preferred_element_type=F32)
                     - jnp.dot(st_ref[...], xs_ref[...], preferred_element_type=F32))

    @pl.when(ki == pl.num_programs(2) - 1)
    def _():
        o_ref[...] = x_ref[...] + m_ref[2:3, :] * _dot(acc_ref[...], wf_ref[...])


def _time_dft(tabs, xc, xs, x, mod, mod_base, mod_stride, wf, batch, t, tf, tk):
    ct, st = tabs
    n, d = x.shape
    nf, nk = t // tf, t // tk
    return pl.pallas_call(
        _time_dft_kernel,
        grid=(batch, nf, nk),
        in_specs=[pl.BlockSpec((tf, tk), lambda b, f, k: (f, k)),
                  pl.BlockSpec((tf, tk), lambda b, f, k: (f, k)),
                  pl.BlockSpec((tk, d), lambda b, f, k: (b * nk + k, 0)),
                  pl.BlockSpec((tk, d), lambda b, f, k: (b * nk + k, 0)),
                  pl.BlockSpec((tf, d), lambda b, f, k: (b * nf + f, 0)),
                  pl.BlockSpec((None, 6, d), lambda b, f, k: (mod_base + b * mod_stride, 0, 0)),
                  pl.BlockSpec((d, d), lambda b, f, k: (0, 0))],
        out_specs=pl.BlockSpec((tf, d), lambda b, f, k: (b * nf + f, 0)),
        out_shape=jax.ShapeDtypeStruct((n, d), F32),
        scratch_shapes=[pltpu.VMEM((tf, d), F32)],
        input_output_aliases={4: 0},
        compiler_params=_cparams("parallel", "parallel", "arbitrary"),
        name="time_dft",
    )(ct, st, xc, xs, x, mod, wf)


def _trig(rows, cols, period):
    prod = jnp.bitwise_and(jnp.arange(rows, dtype=jnp.int32)[:, None] * jnp.arange(cols, dtype=jnp.int32)[None, :],
                           period - 1)
    ang = prod.astype(F32) * (2.0 * math.pi / period)
    return jnp.cos(ang), jnp.sin(ang)


def _dft_table_kernel(ca_ref, sa_ref, cb_ref, sb_ref, ct_ref, st_ref):
    ca, sa = ca_ref[...], sa_ref[...]
    cb, sb = cb_ref[...], sb_ref[...]
    for j in range(ca.shape[1]):
        a, s = ca[:, j:j + 1], sa[:, j:j + 1]
        ct_ref[:, j * LANES:(j + 1) * LANES] = (a * cb - s * sb).astype(ct_ref.dtype)
        st_ref[:, j * LANES:(j + 1) * LANES] = (s * cb + a * sb).astype(st_ref.dtype)


def _time_tables(t, scale):
    hi_n = t // LANES
    ca, sa = _trig(t, hi_n, hi_n)
    cb, sb = _trig(t, LANES, t)
    tf = min(t, 256)
    small = pl.BlockSpec((tf, hi_n), lambda f: (f, 0))
    lanes = pl.BlockSpec((tf, LANES), lambda f: (f, 0))
    wide = pl.BlockSpec((tf, t), lambda f: (f, 0))
    return pl.pallas_call(
        _dft_table_kernel,
        grid=(t // tf,),
        in_specs=[small, small, lanes, lanes],
        out_specs=[wide, wide],
        out_shape=[jax.ShapeDtypeStruct((t, t), BF16)] * 2,
        compiler_params=_cparams("parallel"),
        name="dft_table",
    )(ca * scale, sa * scale, cb, sb)


def _router_kernel(x_ref, g_ref, m_ref, wr_ref, br_ref, tri_ref, ids_out, gates_out, cnt_out, tbase_out, trun_out,
                   base_ref):
    i = pl.program_id(0)

    @pl.when(i == 0)
    def _():
        base_ref[...] = jnp.zeros_like(base_ref)

    h2 = _modulate(x_ref[...], g_ref[...], m_ref[3:4, :], m_ref[4:5, :])
    h_hi = h2.astype(BF16)
    hs = (h_hi, (h2 - h_hi.astype(F32)).astype(BF16))
    lt = jnp.zeros((wr_ref.shape[1], h2.shape[0]), F32)
    for ia, ib in ((0, 0), (0, 1), (1, 0)):
        lt = lt + lax.dot_general(wr_ref[ib], hs[ia], (((1,), (1,)), ((), ())), preferred_element_type=F32)
    logits = lt[0:ROUTE_ROWS] + br_ref[0:ROUTE_ROWS, :]

    tm = logits.shape[1]
    row_i = lax.broadcasted_iota(jnp.int32, (ROUTE_ROWS, tm), 0)
    row = row_i.astype(F32)
    neg = -jnp.inf
    far = float(ROUTE_ROWS)
    gmask = row_i < MOE_GROUPS
    lc = jnp.where(gmask, logits, neg)
    mc = jnp.max(lc, axis=0, keepdims=True)
    sc = jnp.sum(jnp.where(gmask, jnp.exp(logits - mc), 0.0), axis=0, keepdims=True)
    g_val = 1.0 / sc
    g_idx = jnp.min(jnp.where(lc == mc, row, far), axis=0, keepdims=True)
    lgroup = jnp.where(row_i >= MOE_GROUPS, jnp.right_shift(row_i - MOE_GROUPS, 3), -1).astype(F32)
    sel = lgroup == g_idx
    lf = jnp.where(sel, logits, neg)
    m1 = jnp.max(lf, axis=0, keepdims=True)
    ef = jnp.where(sel, jnp.exp(logits - m1), 0.0)
    p = ef / jnp.sum(ef, axis=0, keepdims=True)
    p1 = jnp.where(sel, p, -1.0)
    v1 = jnp.max(p1, axis=0, keepdims=True)
    i1 = jnp.min(jnp.where(p1 == v1, row, far), axis=0, keepdims=True)
    p2 = jnp.where(row == i1, -1.0, p1)
    v2 = jnp.max(p2, axis=0, keepdims=True)
    i2 = jnp.min(jnp.where(p2 == v2, row, far), axis=0, keepdims=True)
    denom = v1 + v2
    gate1 = g_val * v1 / denom
    gate2 = g_val * v2 / denom

    tri = tri_ref[...]
    oh1 = jnp.where(row == i1, 1.0, 0.0)
    oh2 = jnp.where(row == i2, 1.0, 0.0)
    tot1 = jnp.sum(oh1, axis=1, keepdims=True)
    tot2 = jnp.sum(oh2, axis=1, keepdims=True)
    run = jnp.floor((tot1 + tot2 + (SUBLANES - 1)) * (1.0 / SUBLANES))
    er = lax.broadcasted_iota(jnp.int32, (ROUTE_ROWS, ROUTE_ROWS), 0)
    ec = lax.broadcasted_iota(jnp.int32, (ROUTE_ROWS, ROUTE_ROWS), 1)
    before = jnp.where(ec < er, 1.0, 0.0).astype(BF16)
    run_b = jnp.broadcast_to(run, (ROUTE_ROWS, LANES))
    off = jnp.dot(before, run_b.astype(BF16), preferred_element_type=F32)[:, 0:1] * SUBLANES
    c1 = jnp.dot(oh1.astype(BF16), tri, preferred_element_type=F32)
    c2 = jnp.dot(oh2.astype(BF16), tri, preferred_element_type=F32)
    pos1 = jnp.sum(oh1 * (off + c1), axis=0, keepdims=True)
    pos2 = jnp.sum(oh2 * (off + tot1 + c2), axis=0, keepdims=True)
    base = base_ref[...]
    tbase_out[...] = base
    trun_out[...] = run_b * SUBLANES
    base = base + run_b * SUBLANES
    base_ref[...] = base
    cnt_out[...] = base
    zrow = jnp.zeros((4, tm), F32)
    ids_out[...] = jnp.concatenate([i1 - MOE_GROUPS, i2 - MOE_GROUPS, pos1, pos2, zrow], axis=0).astype(jnp.int32)
    gates_out[...] = jnp.concatenate([gate1, gate2, zrow, zrow[0:2]], axis=0)


def _router(x, g, mod, mod_base, mod_stride, wr3, br, tri, t, tm):
    n, d = x.shape
    tpb = t // tm
    full = pl.BlockSpec((tm, d), lambda i: (i, 0))
    lanes = pl.BlockSpec((8, tm), lambda i: (0, i))
    cnt = pl.BlockSpec((ROUTE_ROWS, LANES), lambda i: (0, 0))
    per_tile = pl.BlockSpec((None, ROUTE_ROWS, LANES), lambda i: (i, 0, 0))
    tile_sd = jax.ShapeDtypeStruct((n // tm, ROUTE_ROWS, LANES), F32)
    return pl.pallas_call(
        _router_kernel,
        grid=(n // tm,),
        in_specs=[full, pl.BlockSpec((1, d), lambda i: (0, 0)),
                  pl.BlockSpec((None, 6, d), lambda i: (mod_base + (i // tpb) * mod_stride, 0, 0)),
                  pl.BlockSpec(wr3.shape, lambda i: (0, 0, 0)),
                  pl.BlockSpec(br.shape, lambda i: (0, 0)),
                  pl.BlockSpec((tm, tm), lambda i: (0, 0))],
        out_specs=[lanes, lanes, cnt, per_tile, per_tile],
        out_shape=[jax.ShapeDtypeStruct((8, n), jnp.int32),
                   jax.ShapeDtypeStruct((8, n), F32), jax.ShapeDtypeStruct((ROUTE_ROWS, LANES), F32),
                   tile_sd, tile_sd],
        scratch_shapes=[pltpu.VMEM((ROUTE_ROWS, LANES), F32)],
        compiler_params=_cparams("arbitrary"),
        name="moe_router",
    )(x, g.reshape(1, d), mod, wr3, br, tri)


RUN_SLACK = 256
RUN_COMMON = 64


def _run_copies(len_ref, pos_ref, dst_ref, tile, n_exp, max_len, make_copy, wait):
    sizes = []
    size = max_len
    while size >= SUBLANES:
        sizes.append(size)
        size //= 2

    def pieces(n, pos, row, some_sizes):
        for size in some_sizes:
            done = jnp.bitwise_and(n, -2 * size)

            @pl.when(jnp.bitwise_and(n, size) != 0)
            def _():
                cp = make_copy(pl.multiple_of(pos + done, SUBLANES), pl.multiple_of(row + done, SUBLANES), size)
                if wait:
                    cp.wait()
                else:
                    cp.start()

    small = [s for s in sizes if s < RUN_COMMON]

    def body(e, carry):
        idx = tile * n_exp + e
        n, pos, row = len_ref[idx], pos_ref[idx], dst_ref[idx]

        @pl.when(n >= RUN_COMMON)
        def _():
            pieces(n, pos, row, [s for s in sizes if s >= RUN_COMMON])

        pieces(n, pos, row, small)
        return carry

    lax.fori_loop(0, n_exp, body, 0)


def _wait_rows(total, max_len, make_copy):
    size = max_len
    while size >= SUBLANES:
        @pl.when(jnp.bitwise_and(total, size) != 0)
        def _():
            make_copy(0, 0, size).wait()
        size //= 2


def _dispatch_kernel(len_ref, pos_ref, dst_ref, tot_ref, zb_ref, x_ref, g_ref, m_ref, q_ref, xbuf_ref,
                     s_ref, zero_ref, sem, zsem, *, tm, n_exp):
    step = pl.program_id(0)

    def zero_copy(s):
        start = pl.multiple_of(zb_ref[s] * MOE_BLOCK, MOE_BLOCK)
        return pltpu.make_async_copy(zero_ref, xbuf_ref.at[pl.ds(start, MOE_BLOCK)], zsem)

    @pl.when(step == 0)
    def _():
        zero_ref[...] = jnp.zeros_like(zero_ref)
        for s in range(zb_ref.shape[0]):
            @pl.when(zb_ref[s] >= 0)
            def _():
                zero_copy(s).start()
        for s in range(zb_ref.shape[0]):
            @pl.when(zb_ref[s] >= 0)
            def _():
                zero_copy(s).wait()

    h2 = _modulate(x_ref[...], g_ref[...], m_ref[3:4, :], m_ref[4:5, :]).astype(BF16)
    srows = s_ref.shape[1]
    pos = lax.broadcasted_iota(jnp.int32, (srows, tm), 0)
    perm = jnp.where((pos == q_ref[2:3, :]) | (pos == q_ref[3:4, :]), 1.0, 0.0).astype(BF16)
    slot = lax.rem(step, 2)
    s_ref[slot] = jnp.dot(perm, h2, preferred_element_type=F32)

    def run_copy(buf_slot):
        def make(tile_pos, buf_row, size):
            return pltpu.make_async_copy(s_ref.at[buf_slot, pl.ds(tile_pos, size)],
                                         xbuf_ref.at[pl.ds(buf_row, size)], sem.at[buf_slot])
        return make

    _run_copies(len_ref, pos_ref, dst_ref, step, n_exp, 2 * tm, run_copy(slot), False)

    @pl.when(step > 0)
    def _():
        _wait_rows(tot_ref[step - 1], 2 * tm, run_copy(1 - slot))

    @pl.when(step == pl.num_programs(0) - 1)
    def _():
        _wait_rows(tot_ref[step], 2 * tm, run_copy(slot))


def _dispatch(runs, zblocks, x, g, mod, mod_base, mod_stride, ids, rows, n_exp, t, tm):
    n, d = x.shape
    tpb = t // tm
    cmap = lambda i, *_: (0, 0)
    return pl.pallas_call(
        functools.partial(_dispatch_kernel, tm=tm, n_exp=n_exp),
        grid_spec=pltpu.PrefetchScalarGridSpec(
            num_scalar_prefetch=5,
            grid=(n // tm,),
            in_specs=[pl.BlockSpec((tm, d), lambda i, *_: (i, 0)),
                      pl.BlockSpec((1, d), cmap),
                      pl.BlockSpec((None, 6, d), lambda i, *_: (mod_base + (i // tpb) * mod_stride, 0, 0)),
                      pl.BlockSpec((8, tm), lambda i, *_: (0, i))],
            out_specs=pl.BlockSpec(memory_space=pl.ANY),
            scratch_shapes=[pltpu.VMEM((2, 2 * tm + RUN_SLACK, d), F32), pltpu.VMEM((MOE_BLOCK, d), F32),
                            pltpu.SemaphoreType.DMA((2,)), pltpu.SemaphoreType.DMA],
        ),
        out_shape=jax.ShapeDtypeStruct((rows, d), F32),
        compiler_params=_cparams("arbitrary"),
        name="moe_dispatch",
    )(*runs, zblocks, x, g.reshape(1, d), mod, ids)


def _expert_kernel(be_ref, na_ref, x_ref, w1_ref, w3_ref, w2_ref, o_ref):
    active = pl.program_id(0) < na_ref[0]

    @pl.when(active)
    def _():
        xb = x_ref[...].astype(BF16)
        h1 = jnp.dot(xb, w1_ref[...].astype(BF16), preferred_element_type=F32)
        h3 = jnp.dot(xb, w3_ref[...].astype(BF16), preferred_element_type=F32)
        act = h1 * _sigmoid(h1) * h3
        o_ref[...] = _dot(act, w2_ref[...])

    @pl.when(jnp.logical_not(active))
    def _():
        o_ref[...] = jnp.zeros_like(o_ref)


def _experts(block_e, nact, xbuf, w1, w3, w2, layer):
    rows, d = xbuf.shape
    nb = rows // MOE_BLOCK
    de = w1.shape[3]
    blk = lambda i, be, na: (jnp.minimum(i, na[0] - 1), 0)
    wmap = lambda i, be, na: (layer, be[jnp.minimum(i, na[0] - 1)], 0, 0)
    return pl.pallas_call(
        _expert_kernel,
        grid_spec=pltpu.PrefetchScalarGridSpec(
            num_scalar_prefetch=2,
            grid=(nb,),
            in_specs=[pl.BlockSpec((MOE_BLOCK, d), blk),
                      pl.BlockSpec((None, None, d, de), wmap),
                      pl.BlockSpec((None, None, d, de), wmap),
                      pl.BlockSpec((None, None, de, d), wmap)],
            out_specs=pl.BlockSpec((MOE_BLOCK, d), lambda i, be, na: (i, 0)),
        ),
        out_shape=jax.ShapeDtypeStruct((rows, d), F32),
        compiler_params=_cparams("arbitrary"),
        name="moe_experts",
    )(block_e, nact, xbuf, w1, w3, w2)


def _combine_kernel(len_ref, pos_ref, dst_ref, tot_ref, y_ref, q_ref, gates_ref, x_ref, m_ref, fn_ref, o_ref,
                    w_ref, sem, *, tm, n_exp, final):
    step = pl.program_id(0)

    slot = lax.rem(step, 2)

    def run_copy(buf_slot):
        def make(tile_pos, buf_row, size):
            return pltpu.make_async_copy(y_ref.at[pl.ds(buf_row, size)],
                                         w_ref.at[buf_slot, pl.ds(tile_pos, size)], sem.at[buf_slot])
        return make

    @pl.when(step == 0)
    def _():
        w_ref[...] = jnp.zeros_like(w_ref)
        _run_copies(len_ref, pos_ref, dst_ref, step, n_exp, 2 * tm, run_copy(slot), False)

    @pl.when(step + 1 < pl.num_programs(0))
    def _():
        _run_copies(len_ref, pos_ref, dst_ref, step + 1, n_exp, 2 * tm, run_copy(1 - slot), False)

    _wait_rows(tot_ref[step], 2 * tm, run_copy(slot))

    wrows = w_ref.shape[1]
    pos = lax.broadcasted_iota(jnp.int32, (tm, wrows), 1)
    q = q_ref[...]
    sel = jnp.concatenate([jnp.where(pos == q[:, 0:1], 1.0, 0.0), jnp.where(pos == q[:, 1:2], 1.0, 0.0)],
                          axis=0).astype(BF16)
    picked = jnp.dot(sel, w_ref[slot].astype(BF16), preferred_element_type=F32)
    gates = gates_ref[...]
    y = gates[:, 0:1] * picked[:tm] + gates[:, 1:2] * picked[tm:]
    out = x_ref[...] + m_ref[5:6, :] * y
    if final:
        ms = jnp.mean(out * out, axis=-1, keepdims=True)
        out = out * lax.rsqrt(ms + NORM_EPS) * fn_ref[...]
    o_ref[...] = out


def _combine(runs, ybuf, qcols, gates, x, mod, mod_base, mod_stride, fnorm, n_exp, t, tm, final):
    n, d = x.shape
    tpb = t // tm
    full = pl.BlockSpec((tm, d), lambda i, *_: (i, 0))
    pair = pl.BlockSpec((tm, 2), lambda i, *_: (i, 0))
    return pl.pallas_call(
        functools.partial(_combine_kernel, tm=tm, n_exp=n_exp, final=final),
        grid_spec=pltpu.PrefetchScalarGridSpec(
            num_scalar_prefetch=4,
            grid=(n // tm,),
            in_specs=[pl.BlockSpec(memory_space=pl.ANY), pair, pair, full,
                      pl.BlockSpec((None, 6, d), lambda i, *_: (mod_base + (i // tpb) * mod_stride, 0, 0)),
                      pl.BlockSpec((1, d), lambda i, *_: (0, 0))],
            out_specs=full,
            scratch_shapes=[pltpu.VMEM((2, 2 * tm + RUN_SLACK, d), F32), pltpu.SemaphoreType.DMA((2,))],
        ),
        out_shape=jax.ShapeDtypeStruct((n, d), F32),
        input_output_aliases={7: 0},
        compiler_params=_cparams("arbitrary"),
        name="moe_combine",
    )(*runs, ybuf, qcols, gates, x, mod, fnorm)


def _moe(x, g, mod, mod_base, mod_stride, rt, w1, w3, w2, layer, fnorm, t, tm, final):
    n, d = x.shape
    n_exp = w1.shape[1]
    wr3, br, tri = rt
    assert n_exp * (SUBLANES - 1) <= RUN_SLACK
    ids, gates, counts, tbase, trun = _router(x, g, mod, mod_base, mod_stride, wr3, br, tri, t, tm)
    experts = slice(MOE_GROUPS, MOE_GROUPS + n_exp)
    counts = counts[experts, 0].astype(jnp.int32)
    padded = (counts + MOE_BLOCK - 1) // MOE_BLOCK * MOE_BLOCK
    pend = jnp.cumsum(padded)
    pstart = pend - padded
    run_len = trun[:, experts, 0].astype(jnp.int32)
    run_pos = jnp.cumsum(run_len, axis=1) - run_len
    run_dst = pstart[None, :] + tbase[:, experts, 0].astype(jnp.int32)
    runs = (run_len.reshape(-1), run_pos.reshape(-1), run_dst.reshape(-1), jnp.sum(run_len, axis=1))
    ntiles = n // tm
    nb = -(-(2 * n + ntiles * n_exp * (SUBLANES - 1)) // MOE_BLOCK) + n_exp
    blk_start = jnp.arange(nb, dtype=jnp.int32) * MOE_BLOCK
    block_e = jnp.minimum(jnp.sum((pend[None, :] <= blk_start[:, None]).astype(jnp.int32), axis=1), n_exp - 1)
    nact = pend[-1:] // MOE_BLOCK
    partial = jnp.where(padded > counts, pend // MOE_BLOCK - 1, -1)
    tail = nact + jnp.arange(nb - (2 * n) // MOE_BLOCK, dtype=jnp.int32)
    zblocks = jnp.concatenate([partial, jnp.where(tail < nb, tail, -1)]).astype(jnp.int32)
    xbuf = _dispatch(runs, zblocks, x, g, mod, mod_base, mod_stride, ids, nb * MOE_BLOCK, n_exp, t, tm)
    ybuf = _experts(block_e, nact, xbuf, w1, w3, w2, layer)
    return _combine(runs, ybuf, ids[2:4].T, gates[0:2].T, x, mod, mod_base, mod_stride, fnorm, n_exp, t, tm, final)


def _seg_ones(width):
    idx = np.arange(width) // HEAD_DIM
    return jnp.asarray((idx[:, None] == idx[None, :]).astype(np.float32), dtype=BF16)


def _even_weights(j, mu_x, mu_p, decay_w0, decay_w1, decay_w2, lr_a0, lr_a1, lr_a2, gate_g1, gate_g2,
                  k_k, k_a, r_k, pool_w, pool_scale, seg):
    wa = mu_p.shape[-1]

    def cat1(w):
        return jnp.concatenate([w[0], w[1]], axis=1).astype(BF16)

    def pad2(w):
        z = jnp.zeros_like(w[0])
        return jnp.stack([jnp.concatenate([w[0], z], axis=0), jnp.concatenate([z, w[1]], axis=0)]).astype(BF16)

    return (mu_x[j], mu_p[j], decay_w0[j], cat1(decay_w1[j]), pad2(decay_w2[j]),
            lr_a0[j], cat1(lr_a1[j]), pad2(lr_a2[j]),
            gate_g1[j].astype(BF16), gate_g2[j].astype(BF16),
            k_k[j].reshape(1, wa), k_a[j].reshape(1, wa), r_k[j].reshape(1, wa),
            pool_w[j].astype(BF16), pool_scale[j].reshape(1, wa), seg)


def kernel(x, c, ctx, c_ctx, ada_w, ada_b, norm_mix, norm_ffn, w_in, mu_x, mu_p, decay_w0, decay_w1, decay_w2, lr_a0, lr_a1, lr_a2, gate_g1, gate_g2, k_k, k_a, r_k, gn_w, gn_b, pool_w, pool_scale, w_out, w_fourier, router_c, router_c_b, router_f, router_f_b, moe_w1, moe_w3, moe_w2, final_norm):
    batch, t, d = x.shape
    tc = ctx.shape[1]
    depth = ada_w.shape[0]
    wa = d // 2
    n, ncx = batch * t, batch * tc
    assert batch <= 4 and t % 512 == 0 and tc % 256 == 0 and d % 512 == 0

    cond8 = jnp.zeros((8, d), F32).at[:batch].set(c).at[4].set(c_ctx)
    mod_all = _ada(cond8, ada_w, ada_b).reshape(depth, 8, 6, d)
    seg = _seg_ones(wa)
    fnorm = final_norm.reshape(1, d)

    fgroups = 4
    gc = d // fgroups
    cc, sc = _trig(gc, gc, gc)
    cs = (jnp.concatenate([cc, sc], axis=1) * gc ** -0.5).astype(BF16)
    stage_m, stage_cs, stage_twc, stage_tws = _two_stage_tables(t)
    tab_ctx = _time_tables(tc, tc ** -0.5)

    n_exp = moe_w1.shape[1]
    tri = {tm: jnp.asarray(np.triu(np.ones((tm, tm), np.float32), 1), dtype=BF16) for tm in (512, 256)}

    lat = x.reshape(n, d)
    cx = ctx.reshape(ncx, d)
    last_read = 2 * ((depth - 1) // 2)
    npair = wa // PAIR

    for i in range(depth):
        ctx_in = i <= last_read
        ctx_out = i < last_read
        mod = mod_all[i]
        j = i // 2
        if i % 2 == 0:
            wts = _even_weights(j, mu_x, mu_p, decay_w0, decay_w1, decay_w2, lr_a0, lr_a1, lr_a2,
                                gate_g1, gate_g2, k_k, k_a, r_k, pool_w, pool_scale, seg)
            w_in_b = w_in[j].astype(BF16)
            w_out_b = w_out[j].astype(BF16)
            gnw, gnb = gn_w[j].reshape(1, wa), gn_b[j].reshape(1, wa)
            pc = _proj(cx, norm_mix[i], mod, w_in_b, tc, 4, 0, 0, 256, BF16)
            fc = _rwkv_prep(cx, pc, norm_mix[i], mod, 4, 0, wts, batch, tc, False)
            ryc, pqc = _wkv_chunks(*fc[:6], batch, tc)
            g0 = jnp.zeros((2, batch, npair, PAIR, PAIR), F32)
            yc0, yc1, gctx = _wkv_scan(ryc, pqc, g0)
            pl_ = _proj(lat, norm_mix[i], mod, w_in_b, t, 0, 1, 0, 512, BF16)
            fl = _rwkv_prep(lat, pl_, norm_mix[i], mod, 0, 1, wts, batch, t, True)
            ryl, pql = _wkv_chunks(*fl[:6], batch, t)
            yl0, yl1, _ = _wkv_scan(ryl, pql, gctx)
            lat = _mix_out(yl0.reshape(n, wa), yl1.reshape(n, wa), fl[6], fl[7], fl[8], lat, mod, 0, 1,
                           gnw, gnb, seg, w_out_b, t, 512)
            if ctx_out:
                cx = _mix_out(yc0.reshape(ncx, wa), yc1.reshape(ncx, wa), fc[6], fc[7], fc[8], cx, mod, 4, 0,
                              gnw, gnb, seg, w_out_b, tc, 256)
        else:
            wf_b = w_fourier[j].astype(BF16)
            xc, xs = _chan_dft(lat, norm_mix[i], mod, 0, 1, cs, t, 512, fgroups, F32)
            ar, ai = _dft_stage_a(stage_m, xc, xs, batch, t)
            yf = _dft_stage_c(stage_cs, stage_twc, stage_tws, ar, ai, batch, t)
            lat = _resid_matmul(yf, lat, mod, 0, 1, wf_b, t, 512)
            if ctx_out:
                xcc, xsc = _chan_dft(cx, norm_mix[i], mod, 4, 0, cs, tc, 256, fgroups, BF16)
                cx = _time_dft(tab_ctx, xcc, xsc, cx, mod, 4, 0, wf_b, batch, tc, tc, tc)
        wr = jnp.zeros((LANES, d), F32).at[:MOE_GROUPS].set(router_c[i].T).at[MOE_GROUPS:MOE_GROUPS + n_exp].set(router_f[i].T)
        br = jnp.zeros((LANES, 1), F32).at[:MOE_GROUPS, 0].set(router_c_b[i]).at[MOE_GROUPS:MOE_GROUPS + n_exp, 0].set(router_f_b[i])
        wr3 = jnp.stack(_split3(wr))
        final = i == depth - 1
        lat = _moe(lat, norm_ffn[i], mod, 0, 1, (wr3, br, tri[512]), moe_w1, moe_w3, moe_w2, i, fnorm,
                   t, 512, final)
        if ctx_out:
            cx = _moe(cx, norm_ffn[i], mod, 4, 0, (wr3, br, tri[256]), moe_w1, moe_w3, moe_w2, i, fnorm,
                      tc, 256, False)
    return lat.reshape(batch, t, d)
```
